```python
import math
import jax, jax.numpy as jnp
from jax import lax
import numpy as np

D_MODEL = 1024
BATCH = 2
SEQ = 8192
DEPTH = 4
DEC_BATCH = 128
DEC_SEQ = 8
PAST_LEN = 2048
PAGE_SIZE = 128

N_HEADS = 8
HEAD_DIM = 64
N_KV = 2
GROUP = N_HEADS // N_KV
ATT_DIM = N_HEADS * HEAD_DIM
KV_DIM = N_KV * HEAD_DIM
ROPE_DIM = HEAD_DIM // 4
ROPE_THETA = 500000.0
CMP_BLOCK = 32
CMP_STRIDE = 16
CMP_HIDDEN = 2 * HEAD_DIM
SEL_BLOCK = 64
SEL_TOPK = 16
N_LOCAL_BLOCKS = 2
WINDOW = 512
Q_BLOCK = 128
CONV_DIM = 512
CONV_WIDTH = 3
D_FF = ((8 * D_MODEL // 3 + 255) // 256) * 256
IN_COLS = ATT_DIM + 6 * KV_DIM + 3 * N_HEADS + 3 * CONV_DIM + 2 * D_MODEL
EPS = 1e-6
NEG = -1e30
FORCE_SCORE = 1e4
PAD_POS = -(2 ** 30)

kernel_name = "nsa_shortconv_gated_hybrid_step"


def _rms_norm(x, g):
    x32 = x.astype(jnp.float32)
    y = x32 * lax.rsqrt(jnp.mean(x32 * x32, axis=-1, keepdims=True) + EPS)
    return y.astype(x.dtype) * g


def _rope(x, pos):
    half = ROPE_DIM // 2
    inv = 1.0 / (ROPE_THETA ** (jnp.arange(half, dtype=jnp.float32) * (2.0 / ROPE_DIM)))
    ang = pos.astype(jnp.float32)[:, None] * inv[None, :]
    cos = jnp.cos(ang)[:, None, :].astype(x.dtype)
    sin = jnp.sin(ang)[:, None, :].astype(x.dtype)
    x1 = x[..., :half]
    x2 = x[..., half:ROPE_DIM]
    return jnp.concatenate([x1 * cos - x2 * sin, x2 * cos + x1 * sin, x[..., ROPE_DIM:]], axis=-1)


def _split_cols(z):
    sizes = [ATT_DIM, KV_DIM, KV_DIM, KV_DIM, KV_DIM, KV_DIM, KV_DIM, 3 * N_HEADS,
             CONV_DIM, CONV_DIM, CONV_DIM, D_MODEL, D_MODEL]
    out = []
    o = 0
    for s in sizes:
        out.append(z[..., o:o + s])
        o += s
    return out


def _masked_softmax(s, mask):
    s = jnp.where(mask, s.astype(jnp.float32), NEG)
    return jnp.where(mask, jax.nn.softmax(s, axis=-1), 0.0)


def _compress(kv, pe, w1, w2):
    nb, L = kv.shape[0], kv.shape[1]
    nc = (L - CMP_BLOCK) // CMP_STRIDE + 1
    idx = jnp.arange(nc)[:, None] * CMP_STRIDE + jnp.arange(CMP_BLOCK)[None, :]
    blk = kv[:, idx] + pe[:, None, :]
    blk = jnp.transpose(blk, (0, 1, 3, 2, 4)).reshape(nb, nc, N_KV, CMP_BLOCK * HEAD_DIM)
    return jax.nn.silu(blk @ w1) @ w2


def _nsa(q, gates, pos0, kc_all, vc_all, ks_all, vs_all, kw_all, vw_all, kw_pos0, cmp_pe, cmp_w1, cmp_w2):
    nb, t = q.shape[0], q.shape[1]
    L = kc_all.shape[1]
    kcmp = _compress(kc_all, cmp_pe[0], cmp_w1[0], cmp_w2[0])
    vcmp = _compress(vc_all, cmp_pe[1], cmp_w1[1], cmp_w2[1])
    nc = kcmp.shape[1]
    c_end = jnp.arange(nc, dtype=jnp.int32) * CMP_STRIDE + (CMP_BLOCK - 1)
    ns = -(-L // SEL_BLOCK)
    padn = ns * SEL_BLOCK - L

    def to_blocks(a):
        a = jnp.pad(a, ((0, 0), (0, padn), (0, 0), (0, 0)))
        return a.reshape(nb, ns, SEL_BLOCK, N_KV, HEAD_DIM).transpose(0, 3, 1, 2, 4)

    ksb, vsb = to_blocks(ks_all), to_blocks(vs_all)
    ci = jnp.arange(nc)[:, None] * CMP_STRIDE
    sj = jnp.arange(ns)[None, :] * SEL_BLOCK
    overlap = ((ci <= sj + SEL_BLOCK - 1) & (ci + CMP_BLOCK - 1 >= sj)).astype(jnp.float32)
    sel_ids = jnp.arange(ns, dtype=jnp.int32)
    topk = min(SEL_TOPK, ns)
    lw = kw_all.shape[1]
    gap = pos0 - kw_pos0
    padw = max(0, WINDOW - 1 - gap)
    kw_pad = jnp.pad(kw_all, ((0, 0), (padw, 0), (0, 0), (0, 0)))
    vw_pad = jnp.pad(vw_all, ((0, 0), (padw, 0), (0, 0), (0, 0)))
    kw_pos = jnp.concatenate([jnp.full((padw,), PAD_POS, jnp.int32),
                              kw_pos0 + jnp.arange(lw, dtype=jnp.int32)])
    win_off = gap - (WINDOW - 1) + padw
    qb = Q_BLOCK if t % Q_BLOCK == 0 else 1
    span = qb + WINDOW - 1
    b_ix = jnp.arange(nb)[:, None, None, None]
    k_ix = jnp.arange(N_KV)[None, :, None, None]
    qg_all = (q * (HEAD_DIM ** -0.5)).reshape(nb, t, N_KV, GROUP, HEAD_DIM)
    g_all = gates.reshape(nb, t, N_KV, GROUP, 3)

    def block(bi):
        i0 = bi * qb
        qg = lax.dynamic_slice_in_dim(qg_all, i0, qb, axis=1)
        g = lax.dynamic_slice_in_dim(g_all, i0, qb, axis=1)
        qp = pos0 + i0 + jnp.arange(qb, dtype=jnp.int32)
        pc = _masked_softmax(jnp.einsum('bqkgd,bckd->bkgqc', qg, kcmp), c_end[None, :] <= qp[:, None])
        o_c = jnp.einsum('bkgqc,bckd->bqkgd', pc.astype(vcmp.dtype), vcmp)
        imp = jnp.einsum('bkgqc,cs->bkqs', pc, overlap)
        cur = qp // SEL_BLOCK
        forced = (sel_ids[None, :] == 0) | ((sel_ids[None, :] <= cur[:, None]) &
                                            (sel_ids[None, :] > cur[:, None] - N_LOCAL_BLOCKS))
        avail = sel_ids[None, :] <= cur[:, None]
        imp = jnp.where(avail, jnp.where(forced, FORCE_SCORE, imp), NEG)
        top_s, top_i = lax.top_k(imp, topk)
        ok = top_s > 0.5 * NEG
        k_sel = ksb[b_ix, k_ix, top_i]
        v_sel = vsb[b_ix, k_ix, top_i]
        kpos = top_i[..., None] * SEL_BLOCK + jnp.arange(SEL_BLOCK, dtype=jnp.int32)
        msel = ok[..., None] & (kpos <= qp[None, None, :, None, None])
        s = jnp.einsum('bqkgd,bkqnld->bkgqnl', qg, k_sel)
        ps = _masked_softmax(s.reshape(nb, N_KV, GROUP, qb, -1), msel.reshape(nb, N_KV, 1, qb, -1))
        o_s = jnp.einsum('bkgqnl,bkqnld->bqkgd', ps.reshape(s.shape).astype(v_sel.dtype), v_sel)
        st = i0 + win_off
        kwb = lax.dynamic_slice_in_dim(kw_pad, st, span, axis=1)
        vwb = lax.dynamic_slice_in_dim(vw_pad, st, span, axis=1)
        dq = qp[:, None] - lax.dynamic_slice_in_dim(kw_pos, st, span)[None, :]
        pw = _masked_softmax(jnp.einsum('bqkgd,bjkd->bkgqj', qg, kwb), (dq >= 0) & (dq < WINDOW))
        o_w = jnp.einsum('bkgqj,bjkd->bqkgd', pw.astype(vwb.dtype), vwb)
        o = g[..., 0:1] * o_c + g[..., 1:2] * o_s + g[..., 2:3] * o_w
        return o.reshape(nb, qb, ATT_DIM)

    out = lax.map(block, jnp.arange(t // qb))
    return out.transpose(1, 0, 2, 3).reshape(nb, t, ATT_DIM)


def _layer(x, pos0, past, w_in, cmp_pe, cmp_w1, cmp_w2, w_att_out, conv_w, w_conv_out, w_o,
           norm_mix, norm_ffn, w_ffn_in, w_ffn_out):
    nb, t = x.shape[0], x.shape[1]
    h = _rms_norm(x, norm_mix)
    q, kc, vc, ks, vs, kw, vw, gl, cu, cb, cc, ga, gb = _split_cols(h @ w_in)
    pos = pos0 + jnp.arange(t, dtype=jnp.int32)
    q = _rope(q.reshape(nb, t, N_HEADS, HEAD_DIM), pos)
    kc, ks, kw = (_rope(a.reshape(nb, t, N_KV, HEAD_DIM), pos) for a in (kc, ks, kw))
    vc, vs, vw = (a.reshape(nb, t, N_KV, HEAD_DIM) for a in (vc, vs, vw))
    u = cc * cu
    if past is None:
        kc_all, vc_all, ks_all, vs_all, kw_all, vw_all = kc, vc, ks, vs, kw, vw
        kw_pos0 = pos0
        u_all = jnp.pad(u, ((0, 0), (CONV_WIDTH - 1, 0), (0, 0)))
    else:
        pk_c, pv_c, pk_s, pv_s, bk_w, bv_w, conv_buf = past
        cat = lambda a, b: jnp.concatenate([a, b], axis=1)
        kc_all, vc_all = cat(pk_c, kc), cat(pv_c, vc)
        ks_all, vs_all = cat(pk_s, ks), cat(pv_s, vs)
        kw_all, vw_all = cat(bk_w, kw), cat(bv_w, vw)
        kw_pos0 = pos0 - bk_w.shape[1]
        u_all = cat(conv_buf, u)
    o_att = _nsa(q, jax.nn.sigmoid(gl), pos0, kc_all, vc_all, ks_all, vs_all, kw_all, vw_all,
                 kw_pos0, cmp_pe, cmp_w1, cmp_w2)
    y_conv = sum(conv_w[j] * u_all[:, j:j + t] for j in range(CONV_WIDTH))
    a_br = o_att @ w_att_out
    c_br = (cb * y_conv) @ w_conv_out
    x = x + (jax.nn.sigmoid(ga) * a_br + jax.nn.sigmoid(gb) * c_br) @ w_o
    h2 = _rms_norm(x, norm_ffn)
    gate, up = jnp.split(h2 @ w_ffn_in, 2, axis=-1)
    x = x + (jax.nn.silu(gate) * up) @ w_ffn_out
    keep = min(WINDOW, kw_all.shape[1])
    new_state = (kc, vc, ks, vs, kw_all[:, -keep:], vw_all[:, -keep:], u_all[:, -(CONV_WIDTH - 1):])
    return x, new_state


def _gather_pages(pool, page_table):
    pages = pool[page_table]
    return pages.reshape(page_table.shape[0], -1, N_KV, HEAD_DIM)


def setup_inputs(seed: int = 0) -> dict:
    key = jax.random.key(seed)
    ks = jax.random.split(key, 24)
    n_pages = PAST_LEN // PAGE_SIZE
    n_pool = (DEC_BATCH * n_pages * 5) // 4
    win_buf = min(WINDOW, PAST_LEN)
    f32 = jnp.float32
    nrm = lambda k, shape, s=1.0: jax.random.normal(k, shape, f32) * s
    paged = (DEPTH, n_pool, PAGE_SIZE, N_KV, HEAD_DIM)
    win = (DEPTH, DEC_BATCH, win_buf, N_KV, HEAD_DIM)
    perm = jax.random.permutation(ks[9], n_pool)
    page_table = perm[:DEC_BATCH * n_pages].reshape(DEC_BATCH, n_pages).astype(jnp.int32)
    res_scale = (2 * DEPTH) ** -0.5
    return {
        "x_prompt": nrm(ks[0], (BATCH, SEQ, D_MODEL)),
        "x_sample": nrm(ks[1], (DEC_BATCH, DEC_SEQ, D_MODEL)),
        "cache_cmp_k": nrm(ks[2], paged),
        "cache_cmp_v": nrm(ks[3], paged),
        "cache_slc_k": nrm(ks[4], paged),
        "cache_slc_v": nrm(ks[5], paged),
        "state_win_k": nrm(ks[6], win),
        "state_win_v": nrm(ks[7], win),
        "state_conv": nrm(ks[8], (DEPTH, DEC_BATCH, CONV_WIDTH - 1, CONV_DIM)),
        "page_table": page_table,
        "w_in": nrm(ks[10], (DEPTH, D_MODEL, IN_COLS), D_MODEL ** -0.5),
        "cmp_pe": nrm(ks[11], (DEPTH, 2, CMP_BLOCK, HEAD_DIM), 0.1),
        "cmp_w1": nrm(ks[12], (DEPTH, 2, CMP_BLOCK * HEAD_DIM, CMP_HIDDEN), (CMP_BLOCK * HEAD_DIM) ** -0.5),
        "cmp_w2": nrm(ks[13], (DEPTH, 2, CMP_HIDDEN, HEAD_DIM), CMP_HIDDEN ** -0.5),
        "w_att_out": nrm(ks[14], (DEPTH, ATT_DIM, D_MODEL), ATT_DIM ** -0.5),
        "conv_w": nrm(ks[15], (DEPTH, CONV_WIDTH, CONV_DIM), CONV_WIDTH ** -0.5),
        "w_conv_out": nrm(ks[16], (DEPTH, CONV_DIM, D_MODEL), CONV_DIM ** -0.5),
        "w_o": nrm(ks[17], (DEPTH, D_MODEL, D_MODEL), D_MODEL ** -0.5 * res_scale),
        "norm_mix": 1.0 + nrm(ks[18], (DEPTH, D_MODEL), 0.01),
        "norm_ffn": 1.0 + nrm(ks[19], (DEPTH, D_MODEL), 0.01),
        "w_ffn_in": nrm(ks[20], (DEPTH, D_MODEL, 2 * D_FF), D_MODEL ** -0.5),
        "w_ffn_out": nrm(ks[21], (DEPTH, D_FF, D_MODEL), D_FF ** -0.5 * res_scale),
        "norm_final": 1.0 + nrm(ks[22], (D_MODEL,), 0.01),
    }


def reference(x_prompt, x_sample, cache_cmp_k, cache_cmp_v, cache_slc_k, cache_slc_v,
              state_win_k, state_win_v, state_conv, page_table,
              w_in, cmp_pe, cmp_w1, cmp_w2, w_att_out, conv_w, w_conv_out, w_o,
              norm_mix, norm_ffn, w_ffn_in, w_ffn_out, norm_final):
    yp, ys = x_prompt, x_sample
    st_p = [[] for _ in range(7)]
    st_s = [[] for _ in range(7)]
    for l in range(DEPTH):
        w = (w_in[l], cmp_pe[l], cmp_w1[l], cmp_w2[l], w_att_out[l], conv_w[l], w_conv_out[l], w_o[l],
             norm_mix[l], norm_ffn[l], w_ffn_in[l], w_ffn_out[l])
        yp, new_p = _layer(yp, 0, None, *w)
        past = (_gather_pages(cache_cmp_k[l], page_table), _gather_pages(cache_cmp_v[l], page_table),
                _gather_pages(cache_slc_k[l], page_table), _gather_pages(cache_slc_v[l], page_table),
                state_win_k[l], state_win_v[l], state_conv[l])
        ys, new_s = _layer(ys, PAST_LEN, past, *w)
        for i in range(7):
            st_p[i].append(new_p[i])
            st_s[i].append(new_s[i])
    y_prompt = _rms_norm(yp, norm_final)
    y_sample = _rms_norm(ys, norm_final)
    p_cmp_k, p_cmp_v, p_slc_k, p_slc_v, p_win_k, p_win_v, p_conv = [jnp.stack(a) for a in st_p]
    s_cmp_k, s_cmp_v, s_slc_k, s_slc_v, s_win_k, s_win_v, s_conv = [jnp.stack(a) for a in st_s]
    return (y_prompt, y_sample, p_cmp_k, p_cmp_v, p_slc_k, p_slc_v, p_win_k, p_win_v, p_conv,
            s_cmp_k, s_cmp_v, s_slc_k, s_slc_v, s_win_k, s_win_v, s_conv)
```

```python
import functools
import math

import jax
import jax.numpy as jnp
from jax import lax
from jax.experimental import pallas as pl
from jax.experimental.pallas import tpu as pltpu

N_HEADS = 8
HEAD_DIM = 64
N_KV = 2
GROUP = N_HEADS // N_KV
ATT_DIM = N_HEADS * HEAD_DIM
KV_DIM = N_KV * HEAD_DIM
ROPE_DIM = HEAD_DIM // 4
ROPE_HALF = ROPE_DIM // 2
ROPE_THETA = 500000.0
CMP_BLOCK = 32
CMP_STRIDE = 16
CMP_HIDDEN = 2 * HEAD_DIM
SEL_BLOCK = 64
SEL_TOPK = 16
N_LOCAL_BLOCKS = 2
WINDOW = 512
CONV_DIM = 512
CONV_WIDTH = 3
EPS = 1e-6
NEG = -1e30
FORCE_SCORE = 1e4

LANES = 128
SUBLANES = 8
Q_TILE = 128
SEL_CHUNK = 256
WIN_SPAN = WINDOW + Q_TILE
ROW_TILE = 512
NS_PAD = LANES
VMEM_LIMIT = 56 * 1024 * 1024

C_Q = 0
C_KV = C_Q + ATT_DIM
C_GL = C_KV + 6 * KV_DIM
C_CU = C_GL + N_KV * LANES
C_CB = C_CU + CONV_DIM
C_CC = C_CB + CONV_DIM
F32 = jnp.float32
BF16 = jnp.bfloat16


def _nt(a, b):
    return lax.dot_general(a, b, (((1,), (1,)), ((), ())), preferred_element_type=F32)


def _nn(a, b):
    return jnp.dot(a, b, preferred_element_type=F32)


def _split3(x):
    hi = x.astype(BF16)
    r1 = x - hi.astype(F32)
    mid = r1.astype(BF16)
    lo = (r1 - mid.astype(F32)).astype(BF16)
    return hi, mid, lo


def _params(sem):
    return pltpu.CompilerParams(dimension_semantics=sem, vmem_limit_bytes=VMEM_LIMIT)


def _inproj_body(x_ref, g_ref, w_ref, rc_ref, rs1_ref, rs2_ref,
                 q_ref, kv_ref, gate_ref, u_ref, cb_ref, sga_ref, sgb_ref, *, d_model):
    x = x_ref[...]
    h = x * lax.rsqrt(jnp.mean(x * x, axis=-1, keepdims=True) + EPS)
    hb = (h * g_ref[...]).astype(BF16)
    rc, rs1, rs2 = rc_ref[...], rs1_ref[...], rs2_ref[...]

    def mm(lo, width):
        return _nn(hb, w_ref[:, lo:lo + width])

    def rope(z):
        return z * rc + pltpu.roll(z, LANES - ROPE_HALF, 1) * rs1 + pltpu.roll(z, ROPE_HALF, 1) * rs2

    for j in range(ATT_DIM // 256):
        z = mm(C_Q + 256 * j, 256)
        for t in range(2):
            q_ref[:, 256 * j + LANES * t:256 * j + LANES * (t + 1)] = (
                rope(z[:, LANES * t:LANES * (t + 1)]) * (HEAD_DIM ** -0.5))
    for j in range(3):
        z = mm(C_KV + 256 * j, 256)
        kv_ref[:, 256 * j:256 * j + LANES] = rope(z[:, :LANES])
        kv_ref[:, 256 * j + LANES:256 * (j + 1)] = z[:, LANES:]
    gate_ref[...] = jax.nn.sigmoid(mm(C_GL, N_KV * LANES))
    for j in range(CONV_DIM // 256):
        u_ref[:, 256 * j:256 * (j + 1)] = mm(C_CC + 256 * j, 256) * mm(C_CU + 256 * j, 256)
        cb_ref[:, 256 * j:256 * (j + 1)] = mm(C_CB + 256 * j, 256)
    c_ga = C_CC + CONV_DIM
    c_gb = c_ga + d_model
    for j in range(d_model // 256):
        sga_ref[:, 256 * j:256 * (j + 1)] = jax.nn.sigmoid(mm(c_ga + 256 * j, 256))
        sgb_ref[:, 256 * j:256 * (j + 1)] = jax.nn.sigmoid(mm(c_gb + 256 * j, 256))


def _inproj(x, g, w, rc, rs1, rs2, tm):
    t, d = x.shape
    ncols = w.shape[1]
    row = lambda width: pl.BlockSpec((tm, width), lambda i: (i, 0))
    fixed = lambda shape: pl.BlockSpec(shape, lambda i: (0, 0))
    widths = (ATT_DIM, 6 * KV_DIM, N_KV * LANES, CONV_DIM, CONV_DIM, d, d)
    return pl.pallas_call(
        functools.partial(_inproj_body, d_model=d),
        grid=(t // tm,),
        in_specs=[row(d), fixed((1, d)), fixed((d, ncols)), row(LANES), row(LANES), row(LANES)],
        out_specs=[row(wd) for wd in widths],
        out_shape=[jax.ShapeDtypeStruct((t, wd), F32) for wd in widths],
        compiler_params=_params(("arbitrary",)),
    )(x, g, w, rc, rs1, rs2)


CHUNK = CMP_STRIDE * KV_DIM
HID2 = N_KV * CMP_HIDDEN


def _cmp_ab_body(x_ref, pe_ref, w_ref, ab_ref):
    x = x_ref[0]
    ab_ref[:, :HID2] = _nn((x + pe_ref[0:1, :]).astype(BF16), w_ref[0])
    ab_ref[:, HID2:] = _nn((x + pe_ref[1:2, :]).astype(BF16), w_ref[1])


def _cmp_ab(x3, layer, pe, w, tr):
    rows = x3.shape[1]
    return pl.pallas_call(
        _cmp_ab_body,
        grid=(rows // tr,),
        in_specs=[pl.BlockSpec((1, tr, CHUNK), lambda i: (layer, i, 0)),
                  pl.BlockSpec((2, CHUNK), lambda i: (0, 0)),
                  pl.BlockSpec((2, CHUNK, HID2), lambda i: (0, 0, 0))],
        out_specs=pl.BlockSpec((tr, 2 * HID2), lambda i: (i, 0)),
        out_shape=jax.ShapeDtypeStruct((rows, 2 * HID2), F32),
        compiler_params=_params(("arbitrary",)),
    )(x3, pe, w)


def _cmp_fin_body(abk_ref, abv_ref, w2k_ref, w2v_ref, ok_ref, ov_ref, *, nch):
    rows = abk_ref.shape[0]
    valid = lax.broadcasted_iota(jnp.int32, (rows, 1), 0) % nch < nch - 1

    def fin(ab_ref, w2_ref, o_ref):
        ab = ab_ref[...]
        hid = ab[:, :HID2] + pltpu.roll(ab[:, HID2:], rows - 1, 0)
        act = hid * jax.nn.sigmoid(hid)
        o_ref[...] = jnp.where(valid, _nn(act.astype(BF16), w2_ref[...]), 0.0)

    fin(abk_ref, w2k_ref, ok_ref)
    fin(abv_ref, w2v_ref, ov_ref)


def _cmp_fin(abk, abv, w2k, w2v, nch, tr):
    rows = abk.shape[0]
    ab_spec = pl.BlockSpec((tr, 2 * HID2), lambda i: (i, 0))
    w_spec = pl.BlockSpec((HID2, KV_DIM), lambda i: (0, 0))
    o_spec = pl.BlockSpec((tr, KV_DIM), lambda i: (i, 0))
    return pl.pallas_call(
        functools.partial(_cmp_fin_body, nch=nch),
        grid=(rows // tr,),
        in_specs=[ab_spec, ab_spec, w_spec, w_spec],
        out_specs=[o_spec, o_spec],
        out_shape=[jax.ShapeDtypeStruct((rows, KV_DIM), F32)] * 2,
        compiler_params=_params(("arbitrary",)),
    )(abk, abv, w2k, w2v)


def _select_bias(imp_t, qpos, ntop):
    shape = imp_t.shape
    s_io = lax.broadcasted_iota(jnp.int32, shape, 0)
    cur = qpos // SEL_BLOCK
    avail = s_io <= cur
    forced = (s_io == 0) | (avail & (s_io > cur - N_LOCAL_BLOCKS))
    val = jnp.where(avail, jnp.where(forced, FORCE_SCORE, imp_t), NEG)

    def pick_one(_, carry):
        val, sel = carry
        m = jnp.max(val, axis=0, keepdims=True)
        idx = jnp.min(jnp.where(val == m, s_io, NS_PAD), axis=0, keepdims=True)
        pick = s_io == idx
        return jnp.where(pick, -jnp.inf, val), jnp.where(pick, 1.0, sel)

    _, sel = lax.fori_loop(0, ntop, pick_one, (val, jnp.zeros(shape, F32)))
    return jnp.where((sel > 0.0) & avail, 0.0, NEG)


def _pattn_body(q_ref, g_ref, kc_ref, vc_ref, kaug_ref, vs_ref, kw_ref, vw_ref, ovt_ref, o_ref, *, seq):
    i = pl.program_id(2)
    nch = seq // CMP_STRIDE
    rows = GROUP * Q_TILE
    q = q_ref[...]
    qa = jnp.concatenate([q[:, HEAD_DIM * h:HEAD_DIM * (h + 1)] for h in range(GROUP)], axis=0).astype(BF16)
    qpos1 = i * Q_TILE + lax.broadcasted_iota(jnp.int32, (Q_TILE, 1), 0)
    qpos = jnp.concatenate([qpos1] * GROUP, axis=0)

    s = _nt(qa, kc_ref[0, 0])
    cmask = lax.broadcasted_iota(jnp.int32, (1, nch), 1) * CMP_STRIDE + (CMP_BLOCK - 1) <= qpos
    s = jnp.where(cmask, s, NEG)
    p = jnp.where(cmask, jnp.exp(s - jnp.max(s, axis=-1, keepdims=True)), 0.0)
    l = jnp.sum(p, axis=-1, keepdims=True)
    pc = p * jnp.where(l > 0.0, 1.0 / l, 0.0)
    o_c = _nn(pc.astype(BF16), vc_ref[0, 0])

    pcs = pc[0:Q_TILE]
    for h in range(1, GROUP):
        pcs = pcs + pc[h * Q_TILE:(h + 1) * Q_TILE]
    ovt = ovt_ref[...]
    imp_t = sum(_nt(ovt, part) for part in _split3(pcs))
    qpos_l = i * Q_TILE + lax.broadcasted_iota(jnp.int32, (1, Q_TILE), 1)
    bias = _select_bias(imp_t, qpos_l, min(SEL_TOPK, seq // SEL_BLOCK)).T
    bias4 = jnp.concatenate([bias.astype(BF16)] * GROUP, axis=0)
    q_aug = jnp.concatenate([qa, jnp.zeros((rows, LANES - HEAD_DIM), BF16), bias4], axis=1)

    def chunk(j, carry, causal):
        m, l, acc = carry
        k0 = pl.multiple_of(j * SEL_CHUNK, SEL_CHUNK)
        s = _nt(q_aug, kaug_ref[0, 0, pl.ds(k0, SEL_CHUNK), :])
        if causal:
            kpos = k0 + lax.broadcasted_iota(jnp.int32, (1, SEL_CHUNK), 1)
            s = jnp.where(kpos <= qpos, s, NEG)
        m_new = jnp.maximum(m, jnp.max(s, axis=-1, keepdims=True))
        alpha = jnp.exp(m - m_new)
        p = jnp.exp(s - m_new)
        l = alpha * l + jnp.sum(p, axis=-1, keepdims=True)
        acc = alpha * acc + _nn(p.astype(BF16), vs_ref[0, 0, pl.ds(k0, SEL_CHUNK), :])
        return m_new, l, acc

    n_full = (i * Q_TILE) // SEL_CHUNK
    init = (jnp.full((rows, 1), -3e38, F32), jnp.zeros((rows, 1), F32), jnp.zeros((rows, HEAD_DIM), F32))
    carry = lax.fori_loop(0, n_full, lambda j, c: chunk(j, c, False), init)
    _, l_s, acc_s = chunk(n_full, carry, True)
    o_s = acc_s / l_s

    w0 = pl.multiple_of(jnp.maximum(i * Q_TILE - WINDOW, 0), Q_TILE)
    kpos = w0 + lax.broadcasted_iota(jnp.int32, (1, WIN_SPAN), 1)
    wmask = (kpos <= qpos) & (kpos > qpos - WINDOW)
    s = jnp.where(wmask, _nt(qa, kw_ref[0, 0, pl.ds(w0, WIN_SPAN), :]), NEG)
    p = jnp.exp(s - jnp.max(s, axis=-1, keepdims=True))
    o_w = _nn(p.astype(BF16), vw_ref[0, 0, pl.ds(w0, WIN_SPAN), :]) / jnp.sum(p, axis=-1, keepdims=True)

    g = g_ref[...]
    outs = []
    for h in range(GROUP):
        r = slice(h * Q_TILE, (h + 1) * Q_TILE)
        outs.append(g[:, 3 * h:3 * h + 1] * o_c[r] + g[:, 3 * h + 1:3 * h + 2] * o_s[r]
                    + g[:, 3 * h + 2:3 * h + 3] * o_w[r])
    o_ref[...] = jnp.concatenate(outs, axis=1)


def _pattn(q, gates, kc_h, vc_h, kaug, vs_h, kw_h, vw_h, ovt, nb, seq):
    nq = seq // Q_TILE
    nch = seq // CMP_STRIDE
    qspec = pl.BlockSpec((Q_TILE, GROUP * HEAD_DIM), lambda b, k, i: (b * nq + i, k))
    gspec = pl.BlockSpec((Q_TILE, LANES), lambda b, k, i: (b * nq + i, k))
    per_head = lambda n, w: pl.BlockSpec((1, 1, n, w), lambda b, k, i: (b, k, 0, 0))
    return pl.pallas_call(
        functools.partial(_pattn_body, seq=seq),
        grid=(nb, N_KV, nq),
        in_specs=[qspec, gspec, per_head(nch, HEAD_DIM), per_head(nch, HEAD_DIM),
                  per_head(seq, LANES + NS_PAD), per_head(seq, HEAD_DIM),
                  per_head(seq, HEAD_DIM), per_head(seq, HEAD_DIM),
                  pl.BlockSpec((NS_PAD, nch), lambda b, k, i: (0, 0))],
        out_specs=qspec,
        out_shape=jax.ShapeDtypeStruct((nb * seq, ATT_DIM), F32),
        compiler_params=_params(("arbitrary", "arbitrary", "arbitrary")),
    )(q, gates, kc_h, vc_h, kaug, vs_h, kw_h, vw_h, ovt)


def _sattn_body(qt_ref, gt_ref, kc_ref, vc_ref, ks_ref, vs_ref, kw_ref, vw_ref,
                ksn_ref, vsn_ref, kwn_ref, vwn_ref, ovt_ref, gsum_ref, e_ref, en_ref, o_ref,
                *, past, dec, ntop):
    nchp = kc_ref.shape[1]
    lw = kw_ref.shape[2]
    qt = qt_ref[0]
    col = lax.broadcasted_iota(jnp.int32, (1, LANES), 1)
    tok = col % dec
    qpos = past + tok

    def pad_rows(x):
        return jnp.concatenate([x, jnp.zeros((LANES - dec, KV_DIM), x.dtype)], axis=0)

    new_row = lax.broadcasted_iota(jnp.int32, (LANES, 1), 0)
    new_mask = (new_row < dec) & (new_row <= tok)

    kc = kc_ref[0]
    s = _nn(kc.astype(BF16), qt)
    cmask = lax.broadcasted_iota(jnp.int32, (nchp, 1), 0) < nchp - 1
    s = jnp.where(cmask, s, NEG)
    p = jnp.where(cmask, jnp.exp(s - jnp.max(s, axis=0, keepdims=True)), 0.0)
    pc = p / jnp.sum(p, axis=0, keepdims=True)
    o_c = _nn(vc_ref[0].T.astype(BF16), pc.astype(BF16))

    gsum = gsum_ref[...]
    pcs = sum(_nn(part, gsum) for part in _split3(pc))
    ovt = ovt_ref[...]
    imp_t = sum(_nn(ovt, part) for part in _split3(pcs))
    bias = _select_bias(imp_t, qpos, ntop)
    w_aug = jnp.concatenate([qt, bias.astype(BF16)], axis=0)

    s_p = _nn(jnp.concatenate([ks_ref[0].astype(BF16), e_ref[...]], axis=1), w_aug)
    s_n = _nn(jnp.concatenate([pad_rows(ksn_ref[0]).astype(BF16), en_ref[...]], axis=1), w_aug)
    s_n = jnp.where(new_mask, s_n, NEG)
    m = jnp.maximum(jnp.max(s_p, axis=0, keepdims=True), jnp.max(s_n, axis=0, keepdims=True))
    p_p = jnp.exp(s_p - m)
    p_n = jnp.exp(s_n - m)
    l = jnp.sum(p_p, axis=0, keepdims=True) + jnp.sum(p_n, axis=0, keepdims=True)
    o_s = (_nn(vs_ref[0].T.astype(BF16), p_p.astype(BF16))
           + _nn(pad_rows(vsn_ref[0]).T.astype(BF16), p_n.astype(BF16))) / l

    kpos = (past - lw) + lax.broadcasted_iota(jnp.int32, (lw, 1), 0)
    s_p = jnp.where(kpos > qpos - WINDOW, _nn(kw_ref[0, 0].astype(BF16), qt), NEG)
    s_n = jnp.where(new_mask, _nn(pad_rows(kwn_ref[0]).astype(BF16), qt), NEG)
    m = jnp.maximum(jnp.max(s_p, axis=0, keepdims=True), jnp.max(s_n, axis=0, keepdims=True))
    p_p = jnp.exp(s_p - m)
    p_n = jnp.exp(s_n - m)
    l = jnp.sum(p_p, axis=0, keepdims=True) + jnp.sum(p_n, axis=0, keepdims=True)
    o_w = (_nn(vw_ref[0, 0].T.astype(BF16), p_p.astype(BF16))
           + _nn(pad_rows(vwn_ref[0]).T.astype(BF16), p_n.astype(BF16))) / l

    gt = gt_ref[0]
    o_ref[0] = (gt[0:1] * o_c + gt[1:2] * o_s + gt[2:3] * o_w).T


def _sattn(qt, gt, kc, vc, ks, vs, win_k, win_v, layer, ksn, vsn, kwn, vwn, ovt, gsum, e_past, e_new, past, dec):
    nseq = qt.shape[0]
    nchp = kc.shape[1]
    lw = win_k.shape[2]
    ntop = min(SEL_TOPK, -(-(past + dec) // SEL_BLOCK))
    seq3 = lambda n, w: pl.BlockSpec((1, n, w), lambda b: (b, 0, 0))
    fixed = lambda r, c: pl.BlockSpec((r, c), lambda b: (0, 0))
    win = pl.BlockSpec((1, 1, lw, KV_DIM), lambda b: (layer, b, 0, 0))
    return pl.pallas_call(
        functools.partial(_sattn_body, past=past, dec=dec, ntop=ntop),
        grid=(nseq,),
        in_specs=[seq3(KV_DIM, LANES), seq3(SUBLANES, LANES), seq3(nchp, KV_DIM), seq3(nchp, KV_DIM),
                  seq3(past, KV_DIM), seq3(past, KV_DIM), win, win,
                  seq3(dec, KV_DIM), seq3(dec, KV_DIM), seq3(dec, KV_DIM), seq3(dec, KV_DIM),
                  fixed(NS_PAD, nchp), fixed(LANES, LANES), fixed(past, NS_PAD), fixed(LANES, NS_PAD)],
        out_specs=seq3(LANES, KV_DIM),
        out_shape=jax.ShapeDtypeStruct((nseq, LANES, KV_DIM), F32),
        compiler_params=_params(("arbitrary",)),
    )(qt, gt, kc, vc, ks, vs, win_k, win_v, ksn, vsn, kwn, vwn, ovt, gsum, e_past, e_new)


def _mix_body(x_ref, o_ref, u_ref, uprev_ref, cb_ref, sga_ref, sgb_ref, s1_ref, s2_ref,
              cw_ref, wa_ref, wc_ref, wo_ref, y_ref, ubuf, *, tm, n_ptiles, seq, dec):
    i = pl.program_id(0)
    u = u_ref[...]
    ubuf[0:SUBLANES, :] = uprev_ref[...]
    ubuf[SUBLANES:, :] = u
    row = i * tm + lax.broadcasted_iota(jnp.int32, (tm, 1), 0)
    pos = jnp.where(i < n_ptiles, row % seq, row % dec)
    prev1 = jnp.where(pos >= 1, ubuf[SUBLANES - 1:SUBLANES - 1 + tm, :], s1_ref[...])
    prev2 = jnp.where(pos >= 2, ubuf[SUBLANES - 2:SUBLANES - 2 + tm, :], s2_ref[...])
    cw = cw_ref[...]
    y_conv = cw[0:1] * prev2 + cw[1:2] * prev1 + cw[2:3] * u
    a_br = _nn(o_ref[...].astype(BF16), wa_ref[...])
    c_br = _nn((cb_ref[...] * y_conv).astype(BF16), wc_ref[...])
    mix = sga_ref[...] * a_br + sgb_ref[...] * c_br
    y_ref[...] = x_ref[...] + _nn(mix.astype(BF16), wo_ref[...])


def _mix(x, o_att, u, cb, sga, sgb, side1, side2, conv_w, wa, wc, wo, tm, n_ptiles, seq, dec):
    t, d = x.shape
    row = lambda width: pl.BlockSpec((tm, width), lambda i: (i, 0))
    fixed = lambda shape: pl.BlockSpec(shape, lambda i: (0, 0))
    n_side = side1.shape[0] // tm
    side = pl.BlockSpec((tm, CONV_DIM), lambda i: (jnp.where(i < n_ptiles, n_side - 1, i - n_ptiles), 0))
    prev = pl.BlockSpec((SUBLANES, CONV_DIM), lambda i: (jnp.maximum(i * (tm // SUBLANES) - 1, 0), 0))
    return pl.pallas_call(
        functools.partial(_mix_body, tm=tm, n_ptiles=n_ptiles, seq=seq, dec=dec),
        grid=(t // tm,),
        in_specs=[row(d), row(ATT_DIM), row(CONV_DIM), prev, row(CONV_DIM), row(d), row(d), side, side,
                  fixed((SUBLANES, CONV_DIM)), fixed((ATT_DIM, d)), fixed((CONV_DIM, d)), fixed((d, d))],
        out_specs=row(d),
        out_shape=jax.ShapeDtypeStruct((t, d), F32),
        scratch_shapes=[pltpu.VMEM((tm + SUBLANES, CONV_DIM), F32)],
        compiler_params=_params(("arbitrary",)),
    )(x, o_att, u, u, cb, sga, sgb, side1, side2, conv_w, wa, wc, wo)


def _ffn_body(x_ref, g_ref, win_ref, wout_ref, gf_ref, y_ref, *, d_ff, fc, final):
    x = x_ref[...]
    h = x * lax.rsqrt(jnp.mean(x * x, axis=-1, keepdims=True) + EPS)
    hb = (h * g_ref[...]).astype(BF16)
    acc = x
    for c in range(d_ff // fc):
        gate = _nn(hb, win_ref[:, c * fc:(c + 1) * fc])
        up = _nn(hb, win_ref[:, d_ff + c * fc:d_ff + (c + 1) * fc])
        act = (gate * jax.nn.sigmoid(gate) * up).astype(BF16)
        acc = acc + _nn(act, wout_ref[c * fc:(c + 1) * fc, :])
    if final:
        acc = acc * lax.rsqrt(jnp.mean(acc * acc, axis=-1, keepdims=True) + EPS) * gf_ref[...]
    y_ref[...] = acc


def _ffn(x, g, w_in, w_out, g_final, tm, final):
    t, d = x.shape
    d_ff = w_out.shape[0]
    fc = 256 if d_ff % 256 == 0 else LANES
    row = pl.BlockSpec((tm, d), lambda i: (i, 0))
    fixed = lambda shape: pl.BlockSpec(shape, lambda i: (0, 0))
    return pl.pallas_call(
        functools.partial(_ffn_body, d_ff=d_ff, fc=fc, final=final),
        grid=(t // tm,),
        in_specs=[row, fixed((1, d)), fixed((d, 2 * d_ff)), fixed((d_ff, d)), fixed((1, d))],
        out_specs=row,
        out_shape=jax.ShapeDtypeStruct((t, d), F32),
        compiler_params=_params(("arbitrary",)),
    )(x, g, w_in, w_out, g_final)


def _rope_tables(pos):
    inv = 1.0 / (ROPE_THETA ** (jnp.arange(ROPE_HALF, dtype=F32) * (2.0 / ROPE_DIM)))
    ang = pos.astype(F32)[:, None] * inv[None, :]
    cos, sin = jnp.cos(ang), jnp.sin(ang)
    t = pos.shape[0]
    pad = jnp.zeros((t, HEAD_DIM - ROPE_DIM), F32)
    rc = jnp.concatenate([cos, cos, pad + 1.0], axis=1)
    rs1 = jnp.concatenate([-sin, jnp.zeros_like(sin), pad], axis=1)
    rs2 = jnp.concatenate([jnp.zeros_like(sin), sin, pad], axis=1)
    rep = LANES // HEAD_DIM
    return tuple(jnp.tile(a, (1, rep)) for a in (rc, rs1, rs2))


def _pack_w_in(w):
    d = w.shape[0]
    o_gl = ATT_DIM + 6 * KV_DIM
    gl = w[:, o_gl:o_gl + 3 * N_HEADS].reshape(d, N_KV, 3 * GROUP)
    gl = jnp.pad(gl, ((0, 0), (0, 0), (0, LANES - 3 * GROUP))).reshape(d, N_KV * LANES)
    return jnp.concatenate([w[:, :o_gl], gl, w[:, o_gl + 3 * N_HEADS:]], axis=1).astype(BF16)


def _pack_cmp(pe, w1, w2):
    eye = jnp.eye(N_KV, dtype=F32)
    pe2 = jnp.broadcast_to(pe.reshape(2, CMP_STRIDE, 1, HEAD_DIM), (2, CMP_STRIDE, N_KV, HEAD_DIM)).reshape(2, CHUNK)
    w1r = w1.reshape(2, CMP_STRIDE, HEAD_DIM, CMP_HIDDEN)
    w1p = jnp.einsum('pjdh,kc->pjkdch', w1r, eye).reshape(2, CHUNK, HID2).astype(BF16)
    w2p = jnp.einsum('hd,kc->khcd', w2, eye).reshape(HID2, KV_DIM).astype(BF16)
    return pe2, w1p, w2p


def _overlap_t(nch):
    ci = jnp.arange(nch)[None, :] * CMP_STRIDE
    sj = jnp.arange(NS_PAD)[:, None] * SEL_BLOCK
    return ((ci <= sj + SEL_BLOCK - 1) & (ci + CMP_BLOCK - 1 >= sj)).astype(BF16)


def _heads(a, nb, seq):
    return a.reshape(nb, seq, N_KV, HEAD_DIM).transpose(0, 2, 1, 3).astype(BF16)


def _prompt_attention(q, gates, kv6, cmpk, cmpv, nb, seq):
    kc, vc, ks, vs, kw, vw = kv6
    nch = seq // CMP_STRIDE
    tr = math.gcd(nb * nch, ROW_TILE)
    abk = _cmp_ab(kc.reshape(1, nb * nch, CHUNK), 0, cmpk[0], cmpk[1], tr)
    abv = _cmp_ab(vc.reshape(1, nb * nch, CHUNK), 0, cmpv[0], cmpv[1], tr)
    kcmp, vcmp = _cmp_fin(abk, abv, cmpk[2], cmpv[2], nch, nch)
    e_sel = jnp.concatenate([jnp.zeros((seq, LANES - HEAD_DIM), BF16),
                             jax.nn.one_hot(jnp.arange(seq) // SEL_BLOCK, NS_PAD, dtype=BF16)], axis=1)
    kaug = jnp.concatenate([_heads(ks, nb, seq), jnp.broadcast_to(e_sel, (nb, N_KV) + e_sel.shape)], axis=-1)
    return _pattn(q, gates, _heads(kcmp, nb, nch), _heads(vcmp, nb, nch), kaug, _heads(vs, nb, seq),
                  _heads(kw, nb, seq), _heads(vw, nb, seq), _overlap_t(nch), nb, seq)


def _sample_attention(q, gates, new4, cmpk, cmpv, pool_k, pool_v, slc_k, slc_v, win_k4, win_v4, page_table, layer, dec):
    ndb, n_pages = page_table.shape
    n_pool, page = slc_k.shape[1:3]
    cpp = page // CMP_STRIDE
    past = n_pages * page
    nchp = past // CMP_STRIDE
    ncol = N_KV * GROUP * dec
    eye_kv = jnp.eye(N_KV, dtype=F32)
    pool_tile = math.gcd(n_pool * cpp, ROW_TILE)
    abk = _cmp_ab(pool_k, layer, cmpk[0], cmpk[1], pool_tile).reshape(n_pool, cpp, 2 * HID2)[page_table]
    abv = _cmp_ab(pool_v, layer, cmpv[0], cmpv[1], pool_tile).reshape(n_pool, cpp, 2 * HID2)[page_table]
    kcmp, vcmp = _cmp_fin(abk.reshape(ndb * nchp, 2 * HID2), abv.reshape(ndb * nchp, 2 * HID2), cmpk[2], cmpv[2],
                          nchp, math.gcd(ndb, SUBLANES) * nchp)
    q_s = q.reshape(ndb, dec, N_KV, GROUP, HEAD_DIM).transpose(0, 2, 4, 3, 1)
    qt = (q_s[:, :, :, None] * eye_kv[None, :, None, :, None, None]).reshape(ndb, KV_DIM, ncol)
    qt = jnp.pad(qt, ((0, 0), (0, 0), (0, LANES - ncol))).astype(BF16)
    g_s = gates.reshape(ndb, dec, N_KV, LANES)[..., :3 * GROUP].reshape(ndb, dec, N_KV, GROUP, 3)
    gt = jnp.pad(g_s.transpose(0, 4, 2, 3, 1).reshape(ndb, 3, ncol), ((0, 0), (0, SUBLANES - 3), (0, LANES - ncol)))
    ks_past = slc_k[layer][page_table].reshape(ndb, past, KV_DIM)
    vs_past = slc_v[layer][page_table].reshape(ndb, past, KV_DIM)
    e_past = jax.nn.one_hot(jnp.arange(past) // SEL_BLOCK, NS_PAD, dtype=BF16)
    e_new = jax.nn.one_hot((past + jnp.arange(LANES)) // SEL_BLOCK, NS_PAD, dtype=BF16)
    colid = jnp.arange(LANES)
    gsum = ((colid[:, None] // (GROUP * dec) == colid[None, :] // (GROUP * dec))
            & (colid[:, None] % dec == colid[None, :] % dec)).astype(BF16)
    ksn, vsn, kwn, vwn = (a.reshape(ndb, dec, KV_DIM) for a in new4)
    o_t = _sattn(qt, gt, kcmp.reshape(ndb, nchp, KV_DIM), vcmp.reshape(ndb, nchp, KV_DIM), ks_past, vs_past,
                 win_k4, win_v4, layer, ksn, vsn, kwn, vwn, _overlap_t(nchp), gsum, e_past, e_new, past, dec)
    o_t = o_t[:, :ncol].reshape(ndb, N_KV, GROUP, dec, N_KV, HEAD_DIM)
    return jnp.einsum('bkgtcd,kc->btkgd', o_t, eye_kv).reshape(ndb * dec, ATT_DIM)


def kernel(x_prompt, x_sample, cache_cmp_k, cache_cmp_v, cache_slc_k, cache_slc_v, state_win_k, state_win_v,
           state_conv, page_table, w_in, cmp_pe, cmp_w1, cmp_w2, w_att_out, conv_w, w_conv_out, w_o,
           norm_mix, norm_ffn, w_ffn_in, w_ffn_out, norm_final):
    nb, seq, d = x_prompt.shape
    ndb, dec, _ = x_sample.shape
    depth, n_pool, page = cache_cmp_k.shape[:3]
    n_pages = page_table.shape[1]
    past = n_pages * page
    lw = state_win_k.shape[2]
    tp, ts = nb * seq, ndb * dec
    tm = math.gcd(ROW_TILE, math.gcd(tp, ts))
    n_ptiles = tp // tm
    nchp = past // CMP_STRIDE
    cpp = page // CMP_STRIDE
    assert seq % SEL_CHUNK == 0 and seq >= WIN_SPAN and seq // SEL_BLOCK <= NS_PAD
    assert dec <= SUBLANES and N_KV * GROUP * dec <= LANES and tm % SUBLANES == 0 and seq % tm == 0
    assert (past + dec - CMP_BLOCK) // CMP_STRIDE + 1 == nchp - 1
    assert -(-(past + dec) // SEL_BLOCK) <= NS_PAD and past >= WINDOW and lw == WINDOW

    pos = jnp.concatenate([jnp.tile(jnp.arange(seq, dtype=jnp.int32), nb),
                           jnp.tile(past + jnp.arange(dec, dtype=jnp.int32), ndb)])
    rc, rs1, rs2 = _rope_tables(pos)
    pool_k = cache_cmp_k.reshape(depth, n_pool * cpp, CHUNK)
    pool_v = cache_cmp_v.reshape(depth, n_pool * cpp, CHUNK)
    win_k4 = state_win_k.reshape(depth, ndb, lw, KV_DIM)
    win_v4 = state_win_v.reshape(depth, ndb, lw, KV_DIM)

    x = jnp.concatenate([x_prompt.reshape(tp, d), x_sample.reshape(ts, d)], axis=0)
    st_p = [[] for _ in range(7)]
    st_s = [[] for _ in range(7)]
    for l in range(depth):
        q, kv6, gates, u, cb, sga, sgb = _inproj(x, norm_mix[l][None, :], _pack_w_in(w_in[l]), rc, rs1, rs2, tm)
        kc, vc, ks, vs, kw, vw = (kv6[:, KV_DIM * j:KV_DIM * (j + 1)] for j in range(6))
        pek, w1k, w2k = _pack_cmp(cmp_pe[l, 0], cmp_w1[l, 0], cmp_w2[l, 0])
        pev, w1v, w2v = _pack_cmp(cmp_pe[l, 1], cmp_w1[l, 1], cmp_w2[l, 1])

        o_p = _prompt_attention(q[:tp], gates[:tp], [a[:tp] for a in (kc, vc, ks, vs, kw, vw)],
                                (pek, w1k, w2k), (pev, w1v, w2v), nb, seq)
        o_s = _sample_attention(q[tp:], gates[tp:], [a[tp:] for a in (ks, vs, kw, vw)], (pek, w1k, w2k), (pev, w1v, w2v),
                                pool_k, pool_v, cache_slc_k, cache_slc_v, win_k4, win_v4, page_table, l, dec)
        o_att = jnp.concatenate([o_p, o_s], axis=0)

        cbuf = state_conv[l]
        zero = jnp.zeros((ndb, dec - 1, CONV_DIM), F32)
        side1 = jnp.concatenate([cbuf[:, 1:2], zero], axis=1).reshape(ts, CONV_DIM)
        side2 = jnp.concatenate([cbuf[:, 0:2], zero[:, 1:]], axis=1).reshape(ts, CONV_DIM)
        side1 = jnp.concatenate([side1, jnp.zeros((tm, CONV_DIM), F32)], axis=0)
        side2 = jnp.concatenate([side2, jnp.zeros((tm, CONV_DIM), F32)], axis=0)
        cw = jnp.pad(conv_w[l], ((0, SUBLANES - CONV_WIDTH), (0, 0)))
        x = _mix(x, o_att, u, cb, sga, sgb, side1, side2, cw, w_att_out[l].astype(BF16),
                 w_conv_out[l].astype(BF16), w_o[l].astype(BF16), tm, n_ptiles, seq, dec)
        x = _ffn(x, norm_ffn[l][None, :], w_ffn_in[l].astype(BF16), w_ffn_out[l].astype(BF16),
                 norm_final[None, :], tm, l == depth - 1)

        r5 = lambda a, n, t: a.reshape(n, t, N_KV, HEAD_DIM)
        keep = min(WINDOW, seq)
        for j, a in enumerate((kc, vc, ks, vs)):
            st_p[j].append(r5(a[:tp], nb, seq))
            st_s[j].append(r5(a[tp:], ndb, dec))
        st_p[4].append(r5(kw[:tp], nb, seq)[:, -keep:])
        st_p[5].append(r5(vw[:tp], nb, seq)[:, -keep:])
        st_p[6].append(u[:tp].reshape(nb, seq, CONV_DIM)[:, -(CONV_WIDTH - 1):])
        st_s[4].append(jnp.concatenate([state_win_k[l], r5(kw[tp:], ndb, dec)], axis=1)[:, -lw:])
        st_s[5].append(jnp.concatenate([state_win_v[l], r5(vw[tp:], ndb, dec)], axis=1)[:, -lw:])
        st_s[6].append(jnp.concatenate([cbuf, u[tp:].reshape(ndb, dec, CONV_DIM)], axis=1)[:, -(CONV_WIDTH - 1):])

    y_prompt = x[:tp].reshape(nb, seq, d)
    y_sample = x[tp:].reshape(ndb, dec, d)
    return (y_prompt, y_sample, *[jnp.stack(a) for a in st_p], *[jnp.stack(a) for a in st_s])
```

```python
import functools
import math

import jax
import jax.numpy as jnp
from jax import lax
from jax.experimental import pallas as pl
from jax.experimental.pallas import tpu as pltpu

N_HEADS = 8
HEAD_DIM = 64
N_KV = 2
GROUP = N_HEADS // N_KV
ATT_DIM = N_HEADS * HEAD_DIM
KV_DIM = N_KV * HEAD_DIM
ROPE_DIM = HEAD_DIM // 4
ROPE_HALF = ROPE_DIM // 2
ROPE_THETA = 500000.0
CMP_BLOCK = 32
CMP_STRIDE = 16
CMP_HIDDEN = 2 * HEAD_DIM
SEL_BLOCK = 64
SEL_TOPK = 16
N_LOCAL_BLOCKS = 2
WINDOW = 512
CONV_DIM = 512
CONV_WIDTH = 3
EPS = 1e-6
NEG = -1e30
FORCE_SCORE = 1e4

LANES = 128
SUBLANES = 8
Q_TILE = 128
SEL_CHUNK = 256
SEL_GROUP = 4 * SEL_CHUNK
LOG2E = 1.4426950408889634
WIN_SPAN = WINDOW + Q_TILE
ROW_TILE = 512
POOL_PAGES = 32
NS_PAD = LANES
VMEM_LIMIT = 56 * 1024 * 1024

C_Q = 0
C_KV = C_Q + ATT_DIM
C_GL = C_KV + 6 * KV_DIM
C_CU = C_GL + N_KV * LANES
C_CB = C_CU + CONV_DIM
C_CC = C_CB + CONV_DIM
F32 = jnp.float32
BF16 = jnp.bfloat16


def _nt(a, b):
    return lax.dot_general(a, b, (((1,), (1,)), ((), ())), preferred_element_type=F32)


def _nn(a, b):
    return jnp.dot(a, b, preferred_element_type=F32)


def _split3(x):
    hi = x.astype(BF16)
    r1 = x - hi.astype(F32)
    mid = r1.astype(BF16)
    lo = (r1 - mid.astype(F32)).astype(BF16)
    return hi, mid, lo


def _params(sem):
    return pltpu.CompilerParams(dimension_semantics=sem, vmem_limit_bytes=VMEM_LIMIT)


def _inproj_body(x_ref, g_ref, w_ref, rc_ref, rs1_ref, rs2_ref,
                 q_ref, kcvc_ref, kskw_ref, vt_ref, st_ref, gate_ref, u_ref, cb_ref, sga_ref, sgb_ref, *, d_model):
    x = x_ref[...]
    h = x * lax.rsqrt(jnp.mean(x * x, axis=-1, keepdims=True) + EPS)
    hb = (h * g_ref[...]).astype(BF16)
    rc, rs1, rs2 = rc_ref[...], rs1_ref[...], rs2_ref[...]

    def mm(lo, width):
        return _nt(hb, w_ref[lo:lo + width, :])

    def rope(z):
        return z * rc + pltpu.roll(z, LANES - ROPE_HALF, 1) * rs1 + pltpu.roll(z, ROPE_HALF, 1) * rs2

    for j in range(ATT_DIM // 256):
        z = mm(C_Q + 256 * j, 256)
        for t in range(2):
            q_ref[:, 256 * j + LANES * t:256 * j + LANES * (t + 1)] = (
                rope(z[:, LANES * t:LANES * (t + 1)]) * (HEAD_DIM ** -0.5))
    for j in range(3):
        z = mm(C_KV + 256 * j, 256)
        k, v = rope(z[:, :LANES]), z[:, LANES:]
        kt, vt = k.T, v.T
        st_ref[2 * j, 0] = kt
        st_ref[2 * j + 1, 0] = vt
        if j == 0:
            kcvc_ref[:, :LANES] = k
            kcvc_ref[:, LANES:] = v
        else:
            kskw_ref[:, LANES * (j - 1):LANES * j] = k.astype(BF16)
            vt_ref[j - 1, 0] = vt.astype(BF16)
    gate_ref[...] = jax.nn.sigmoid(mm(C_GL, N_KV * LANES))
    for j in range(CONV_DIM // 256):
        u_ref[:, 256 * j:256 * (j + 1)] = mm(C_CC + 256 * j, 256) * mm(C_CU + 256 * j, 256)
        cb_ref[:, 256 * j:256 * (j + 1)] = mm(C_CB + 256 * j, 256)
    c_ga = C_CC + CONV_DIM
    c_gb = c_ga + d_model
    for j in range(d_model // 256):
        sga_ref[:, 256 * j:256 * (j + 1)] = jax.nn.sigmoid(mm(c_ga + 256 * j, 256))
        sgb_ref[:, 256 * j:256 * (j + 1)] = jax.nn.sigmoid(mm(c_gb + 256 * j, 256))


def _inproj(x, g, w, rope, nseq, tm):
    t, d = x.shape
    slen = t // nseq
    tps = slen // tm
    nrows = w.shape[0]
    row = lambda width: pl.BlockSpec((tm, width), lambda i: (i, 0))
    fixed = lambda shape: pl.BlockSpec(shape, lambda i: (0, 0))
    tab = pl.BlockSpec((tm, LANES), lambda i: (i % tps, 0))
    tr = lambda n: pl.BlockSpec((n, 1, KV_DIM, tm), lambda i: (0, i // tps, 0, i % tps))
    sds = jax.ShapeDtypeStruct
    return pl.pallas_call(
        functools.partial(_inproj_body, d_model=d),
        grid=(t // tm,),
        in_specs=[row(d), fixed((1, d)), fixed((nrows, d)), tab, tab, tab],
        out_specs=[row(ATT_DIM), row(2 * KV_DIM), row(2 * KV_DIM), tr(2), tr(6), row(N_KV * LANES),
                   row(CONV_DIM), row(CONV_DIM), row(d), row(d)],
        out_shape=[sds((t, ATT_DIM), F32), sds((t, 2 * KV_DIM), F32), sds((t, 2 * KV_DIM), BF16),
                   sds((2, nseq, KV_DIM, slen), BF16), sds((6, nseq, KV_DIM, slen), F32), sds((t, N_KV * LANES), F32),
                   sds((t, CONV_DIM), F32), sds((t, CONV_DIM), F32), sds((t, d), F32), sds((t, d), F32)],
        compiler_params=_params(("arbitrary",)),
    )(x, g, w, *rope)


CHUNK = CMP_STRIDE * KV_DIM
HID2 = N_KV * CMP_HIDDEN


def _cmp_partials(src, pe_ref, w_ref, ab_ref):
    n = src.shape[0] // CMP_STRIDE
    acc_a = jnp.zeros((n, HID2), F32)
    acc_b = jnp.zeros((n, HID2), F32)
    for jp in range(CMP_STRIDE // 2):
        x = jnp.concatenate([src[pl.ds(2 * jp, n, stride=CMP_STRIDE), :],
                             src[pl.ds(2 * jp + 1, n, stride=CMP_STRIDE), :]], axis=1)
        cols = slice(2 * KV_DIM * jp, 2 * KV_DIM * (jp + 1))
        acc_a = acc_a + _nn((x + pe_ref[0:1, cols]).astype(BF16), w_ref[0, cols, :])
        acc_b = acc_b + _nn((x + pe_ref[1:2, cols]).astype(BF16), w_ref[1, cols, :])
    ab_ref[:, :HID2] = acc_a
    ab_ref[:, HID2:] = acc_b


def _cmp_ab_rows_body(x_ref, pe_ref, w_ref, ab_ref):
    _cmp_partials(x_ref, pe_ref, w_ref, ab_ref)


def _cmp_ab_rows(x, rows, col, pe, w, tr):
    return pl.pallas_call(
        _cmp_ab_rows_body,
        grid=(rows // tr,),
        in_specs=[pl.BlockSpec((tr, KV_DIM), lambda i: (i, col)),
                  pl.BlockSpec((2, CHUNK), lambda i: (0, 0)),
                  pl.BlockSpec((2, CHUNK, HID2), lambda i: (0, 0, 0))],
        out_specs=pl.BlockSpec((tr // CMP_STRIDE, 2 * HID2), lambda i: (i, 0)),
        out_shape=jax.ShapeDtypeStruct((rows // CMP_STRIDE, 2 * HID2), F32),
        compiler_params=_params(("arbitrary",)),
    )(x, pe, w)


def _cmp_ab_pages_body(x_ref, pe_ref, w_ref, ab_ref, rows_scr):
    page = x_ref.shape[3]
    for p in range(x_ref.shape[1]):
        rows_scr[page * p:page * (p + 1), :] = x_ref[0, p].T
    _cmp_partials(rows_scr, pe_ref, w_ref, ab_ref)


def _cmp_ab_pages(pool, layer, pe, w, pp):
    _, n_pool, _, page = pool.shape
    cpp = page // CMP_STRIDE
    return pl.pallas_call(
        _cmp_ab_pages_body,
        grid=(n_pool // pp,),
        in_specs=[pl.BlockSpec((1, pp, KV_DIM, page), lambda i: (layer, i, 0, 0)),
                  pl.BlockSpec((2, CHUNK), lambda i: (0, 0)),
                  pl.BlockSpec((2, CHUNK, HID2), lambda i: (0, 0, 0))],
        out_specs=pl.BlockSpec((pp * cpp, 2 * HID2), lambda i: (i, 0)),
        out_shape=jax.ShapeDtypeStruct((n_pool * cpp, 2 * HID2), F32),
        scratch_shapes=[pltpu.VMEM((pp * page, KV_DIM), F32)],
        compiler_params=_params(("arbitrary",)),
    )(pool, pe, w)


def _cmp_fin_body(abk_ref, abv_ref, w2k_ref, w2v_ref, ok_ref, ov_ref, *, nch):
    rows = abk_ref.shape[0]
    valid = lax.broadcasted_iota(jnp.int32, (rows, 1), 0) % nch < nch - 1

    def fin(ab_ref, w2_ref, o_ref):
        ab = ab_ref[...]
        hid = ab[:, :HID2] + pltpu.roll(ab[:, HID2:], rows - 1, 0)
        act = hid * jax.nn.sigmoid(hid)
        o_ref[...] = jnp.where(valid, _nn(act.astype(BF16), w2_ref[...]), 0.0)

    fin(abk_ref, w2k_ref, ok_ref)
    fin(abv_ref, w2v_ref, ov_ref)


def _cmp_fin(abk, abv, w2k, w2v, nch, tr):
    rows = abk.shape[0]
    ab_spec = pl.BlockSpec((tr, 2 * HID2), lambda i: (i, 0))
    w_spec = pl.BlockSpec((HID2, KV_DIM), lambda i: (0, 0))
    o_spec = pl.BlockSpec((tr, KV_DIM), lambda i: (i, 0))
    return pl.pallas_call(
        functools.partial(_cmp_fin_body, nch=nch),
        grid=(rows // tr,),
        in_specs=[ab_spec, ab_spec, w_spec, w_spec],
        out_specs=[o_spec, o_spec],
        out_shape=[jax.ShapeDtypeStruct((rows, KV_DIM), F32)] * 2,
        compiler_params=_params(("arbitrary",)),
    )(abk, abv, w2k, w2v)


def _select_bias(imp_t, qpos, ntop):
    shape = imp_t.shape
    s_io = lax.broadcasted_iota(jnp.int32, shape, 0)
    cur = qpos // SEL_BLOCK
    avail = s_io <= cur
    forced = (s_io == 0) | (avail & (s_io > cur - N_LOCAL_BLOCKS))
    val = jnp.where(avail, jnp.where(forced, FORCE_SCORE, imp_t), NEG)

    def pick_one(_, carry):
        val, sel = carry
        m = jnp.max(val, axis=0, keepdims=True)
        idx = jnp.min(jnp.where(val == m, s_io, NS_PAD), axis=0, keepdims=True)
        pick = s_io == idx
        return jnp.where(pick, -jnp.inf, val), jnp.where(pick, 1.0, sel)

    _, sel = lax.fori_loop(0, ntop, pick_one, (val, jnp.zeros(shape, F32)))
    return jnp.where((sel > 0.0) & avail, 0.0, NEG)


def _pattn_body(q_ref, g_ref, kc_ref, vct_ref, ks_ref, kw_ref, vst_ref, vwt_ref, ovt_ref, o_ref, bias_scr, *, seq):
    kv = pl.program_id(1)
    i = pl.program_id(2)
    nch = seq // CMP_STRIDE
    cols = GROUP * Q_TILE
    q = q_ref[...]
    qt = jnp.concatenate([q[:, :LANES].T, q[:, LANES:].T], axis=0)
    qt4 = jnp.concatenate([qt[HEAD_DIM * h:HEAD_DIM * (h + 1)] for h in range(GROUP)], axis=1)
    row_kv = lax.broadcasted_iota(jnp.int32, (KV_DIM, 1), 0) // HEAD_DIM
    qt_pad = jnp.where(row_kv == kv, jnp.concatenate([qt4] * N_KV, axis=0) * LOG2E, 0.0).astype(BF16)
    qpos1 = i * Q_TILE + lax.broadcasted_iota(jnp.int32, (1, Q_TILE), 1)
    qpos = jnp.concatenate([qpos1] * GROUP, axis=1)

    s = _nn(kc_ref[0], qt_pad)
    cmask = lax.broadcasted_iota(jnp.int32, (nch, 1), 0) * CMP_STRIDE + (CMP_BLOCK - 1) <= qpos
    s = jnp.where(cmask, s, NEG)
    p = jnp.where(cmask, jnp.exp2(s - jnp.max(s, axis=0, keepdims=True)), 0.0)
    l = jnp.sum(p, axis=0, keepdims=True)
    pc = p * jnp.where(l > 0.0, 1.0 / l, 0.0)
    o_c = _nn(vct_ref[0], pc.astype(BF16))

    pcs = pc[:, :Q_TILE]
    for h in range(1, GROUP):
        pcs = pcs + pc[:, h * Q_TILE:(h + 1) * Q_TILE]
    ovt = ovt_ref[...]
    imp_t = sum(_nn(ovt, part) for part in _split3(pcs))
    bias_scr[...] = _select_bias(imp_t, qpos1, min(SEL_TOPK, seq // SEL_BLOCK))

    def group(gj, carry, causal):
        m, l, acc = carry
        g0 = pl.multiple_of(gj * SEL_GROUP, SEL_GROUP)
        s_all = _nn(ks_ref[pl.ds(g0, SEL_GROUP), :], qt_pad)
        for c in range(SEL_GROUP // SEL_CHUNK):
            parts = []
            for r in range(SEL_CHUNK // SEL_BLOCK):
                row = SEL_CHUNK * c + SEL_BLOCK * r
                b_row = bias_scr[pl.ds(gj * (SEL_GROUP // SEL_BLOCK) + row // SEL_BLOCK, 1), :]
                parts.append(s_all[row:row + SEL_BLOCK] + jnp.concatenate([b_row] * GROUP, axis=1))
            s = jnp.concatenate(parts, axis=0)
            k0 = pl.multiple_of(g0 + SEL_CHUNK * c, SEL_CHUNK)
            if causal:
                kpos = k0 + lax.broadcasted_iota(jnp.int32, (SEL_CHUNK, 1), 0)
                s = jnp.where(kpos <= qpos, s, NEG)
            m_new = jnp.maximum(m, jnp.max(s, axis=0, keepdims=True))
            alpha = jnp.exp2(m - m_new)
            p = jnp.exp2(s - m_new)
            l = alpha * l + jnp.sum(p, axis=0, keepdims=True)
            acc = alpha * acc + _nn(vst_ref[0, 0, :, pl.ds(k0, SEL_CHUNK)], p.astype(BF16))
            m = m_new
        return m, l, acc

    n_below = (i * Q_TILE) // SEL_GROUP
    init = (jnp.full((1, cols), -3e38, F32), jnp.zeros((1, cols), F32), jnp.zeros((HEAD_DIM, cols), F32))
    carry = lax.fori_loop(0, n_below, lambda gj, c: group(gj, c, False), init)
    _, l_s, acc_s = group(n_below, carry, True)
    o_s = acc_s / l_s

    w0 = pl.multiple_of(jnp.maximum(i * Q_TILE - WINDOW, 0), Q_TILE)
    kpos = w0 + lax.broadcasted_iota(jnp.int32, (WIN_SPAN, 1), 0)
    wmask = (kpos <= qpos) & (kpos > qpos - WINDOW)
    s = jnp.where(wmask, _nn(kw_ref[pl.ds(w0, WIN_SPAN), :], qt_pad), NEG)
    p = jnp.exp2(s - jnp.max(s, axis=0, keepdims=True))
    o_w = _nn(vwt_ref[0, 0, :, pl.ds(w0, WIN_SPAN)], p.astype(BF16)) / jnp.sum(p, axis=0, keepdims=True)

    gt = g_ref[...].T
    outs = []
    for h in range(GROUP):
        c = slice(h * Q_TILE, (h + 1) * Q_TILE)
        outs.append(gt[3 * h:3 * h + 1] * o_c[:, c] + gt[3 * h + 1:3 * h + 2] * o_s[:, c]
                    + gt[3 * h + 2:3 * h + 3] * o_w[:, c])
    for t in range(GROUP // 2):
        o_ref[:, LANES * t:LANES * (t + 1)] = jnp.concatenate(outs[2 * t:2 * t + 2], axis=0).T


def _pattn(q, gates, kcmp, vcmp_t, kskw, vt, ovt, nb, seq):
    nq = seq // Q_TILE
    nch = seq // CMP_STRIDE
    qspec = pl.BlockSpec((Q_TILE, GROUP * HEAD_DIM), lambda b, k, i: (b * nq + i, k))
    gspec = pl.BlockSpec((Q_TILE, LANES), lambda b, k, i: (b * nq + i, k))
    kspec = lambda which: pl.BlockSpec((seq, KV_DIM), lambda b, k, i: (b, which))
    vspec = lambda which: pl.BlockSpec((1, 1, HEAD_DIM, seq), lambda b, k, i: (which, b, k, 0))
    return pl.pallas_call(
        functools.partial(_pattn_body, seq=seq),
        grid=(nb, N_KV, nq),
        in_specs=[qspec, gspec,
                  pl.BlockSpec((1, nch, KV_DIM), lambda b, k, i: (b, 0, 0)),
                  pl.BlockSpec((1, HEAD_DIM, nch), lambda b, k, i: (b, k, 0)),
                  kspec(0), kspec(1), vspec(0), vspec(1),
                  pl.BlockSpec((NS_PAD, nch), lambda b, k, i: (0, 0))],
        out_specs=qspec,
        out_shape=jax.ShapeDtypeStruct((nb * seq, ATT_DIM), F32),
        scratch_shapes=[pltpu.VMEM((NS_PAD, Q_TILE), F32)],
        compiler_params=_params(("arbitrary", "arbitrary", "arbitrary")),
    )(q, gates, kcmp, vcmp_t, kskw, kskw, vt, vt, ovt)


def _sattn_body(qp_ref, g_ref, kc_ref, vc_ref, ks_ref, vs_ref, kw_ref, vw_ref,
                ksn_ref, vsn_ref, kwn_ref, vwn_ref, ovt_ref, gsum_ref, et_ref, en_ref, o_ref,
                *, past, dec, ntop):
    nchp = kc_ref.shape[1]
    n_pages = ks_ref.shape[1]
    lw = kw_ref.shape[3]
    qp = qp_ref[0]
    tok_r = lax.broadcasted_iota(jnp.int32, (LANES, 1), 0) % dec
    tok_l = lax.broadcasted_iota(jnp.int32, (1, LANES), 1) % dec
    new_j = lax.broadcasted_iota(jnp.int32, (1, LANES), 1)
    new_mask = (new_j < dec) & (new_j <= tok_r)

    def pad_rows(x):
        return jnp.concatenate([x.astype(F32), jnp.zeros((LANES - dec, KV_DIM), F32)], axis=0).astype(BF16)

    def softmax2(s_p, s_n):
        m = jnp.maximum(jnp.max(s_p, axis=-1, keepdims=True), jnp.max(s_n, axis=-1, keepdims=True))
        p_p, p_n = jnp.exp(s_p - m), jnp.exp(s_n - m)
        l = jnp.sum(p_p, axis=-1, keepdims=True) + jnp.sum(p_n, axis=-1, keepdims=True)
        return p_p.astype(BF16), p_n.astype(BF16), l

    s = _nn(qp, kc_ref[0].T.astype(BF16))
    cmask = lax.broadcasted_iota(jnp.int32, (1, nchp), 1) < nchp - 1
    s = jnp.where(cmask, s, NEG)
    p = jnp.where(cmask, jnp.exp(s - jnp.max(s, axis=-1, keepdims=True)), 0.0)
    pc = p / jnp.sum(p, axis=-1, keepdims=True)
    o_c = _nn(pc.astype(BF16), vc_ref[0].astype(BF16))

    gsum = gsum_ref[...]
    pcs = sum(_nn(gsum, part) for part in _split3(pc))
    ovt = ovt_ref[...]
    imp_t = sum(_nt(ovt, part) for part in _split3(pcs))
    bias = _select_bias(imp_t, past + tok_l, ntop).T.astype(BF16)

    kt = jnp.concatenate([ks_ref[0, pg] for pg in range(n_pages)], axis=1).astype(BF16)
    vt = jnp.concatenate([vs_ref[0, pg] for pg in range(n_pages)], axis=1).astype(BF16)
    s_p = _nn(qp, kt) + _nn(bias, et_ref[...])
    s_n = jnp.where(new_mask, _nt(qp, pad_rows(ksn_ref[0])) + _nt(bias, en_ref[...]), NEG)
    p_p, p_n, l = softmax2(s_p, s_n)
    o_s = (_nt(p_p, vt) + _nn(p_n, pad_rows(vsn_ref[0]))) / l

    kpos = (past - lw) + lax.broadcasted_iota(jnp.int32, (1, lw), 1)
    s_p = jnp.where(kpos > past + tok_r - WINDOW, _nn(qp, kw_ref[0, 0].astype(BF16)), NEG)
    s_n = jnp.where(new_mask, _nt(qp, pad_rows(kwn_ref[0])), NEG)
    p_p, p_n, l = softmax2(s_p, s_n)
    o_w = (_nt(p_p, vw_ref[0, 0].astype(BF16)) + _nn(p_n, pad_rows(vwn_ref[0]))) / l

    g = g_ref[0]
    o_ref[0] = g[:, 0:1] * o_c + g[:, 1:2] * o_s + g[:, 2:3] * o_w


def _sattn(qp, g, kc, vc, ks, vs, win_k, win_v, layer, ksn, vsn, kwn, vwn, ovt, gsum, e_t, e_new, past, dec):
    nseq = qp.shape[0]
    nchp = kc.shape[1]
    n_pages, _, page = ks.shape[1:]
    lw = win_k.shape[3]
    ntop = min(SEL_TOPK, -(-(past + dec) // SEL_BLOCK))
    seq3 = lambda n, w: pl.BlockSpec((1, n, w), lambda b: (b, 0, 0))
    fixed = lambda r, c: pl.BlockSpec((r, c), lambda b: (0, 0))
    pages = pl.BlockSpec((1, n_pages, KV_DIM, page), lambda b: (b, 0, 0, 0))
    win = pl.BlockSpec((1, 1, KV_DIM, lw), lambda b: (layer, b, 0, 0))
    return pl.pallas_call(
        functools.partial(_sattn_body, past=past, dec=dec, ntop=ntop),
        grid=(nseq,),
        in_specs=[seq3(LANES, KV_DIM), seq3(LANES, SUBLANES), seq3(nchp, KV_DIM), seq3(nchp, KV_DIM),
                  pages, pages, win, win,
                  seq3(dec, KV_DIM), seq3(dec, KV_DIM), seq3(dec, KV_DIM), seq3(dec, KV_DIM),
                  fixed(NS_PAD, nchp), fixed(LANES, LANES), fixed(NS_PAD, past), fixed(LANES, NS_PAD)],
        out_specs=seq3(LANES, KV_DIM),
        out_shape=jax.ShapeDtypeStruct((nseq, LANES, KV_DIM), F32),
        compiler_params=_params(("arbitrary",)),
    )(qp, g, kc, vc, ks, vs, win_k, win_v, ksn, vsn, kwn, vwn, ovt, gsum, e_t, e_new)


def _mix_body(x_ref, o_ref, u_ref, uprev_ref, cb_ref, sga_ref, sgb_ref, s1_ref, s2_ref,
              cw_ref, wa_ref, wc_ref, wo_ref, y_ref, ubuf, *, tm, slen):
    i = pl.program_id(0)
    u = u_ref[...]
    ubuf[0:SUBLANES, :] = uprev_ref[...]
    ubuf[SUBLANES:, :] = u
    pos = (i * tm + lax.broadcasted_iota(jnp.int32, (tm, 1), 0)) % slen
    prev1 = jnp.where(pos >= 1, ubuf[SUBLANES - 1:SUBLANES - 1 + tm, :], s1_ref[...])
    prev2 = jnp.where(pos >= 2, ubuf[SUBLANES - 2:SUBLANES - 2 + tm, :], s2_ref[...])
    cw = cw_ref[...]
    y_conv = cw[0:1] * prev2 + cw[1:2] * prev1 + cw[2:3] * u
    a_br = _nn(o_ref[...].astype(BF16), wa_ref[...])
    c_br = _nn((cb_ref[...] * y_conv).astype(BF16), wc_ref[...])
    mix = sga_ref[...] * a_br + sgb_ref[...] * c_br
    y_ref[...] = x_ref[...] + _nn(mix.astype(BF16), wo_ref[...])


def _mix(x, o_att, u, cb, sga, sgb, side1, side2, conv_w, wa, wc, wo, tm, slen):
    t, d = x.shape
    row = lambda width: pl.BlockSpec((tm, width), lambda i: (i, 0))
    fixed = lambda shape: pl.BlockSpec(shape, lambda i: (0, 0))
    n_side = side1.shape[0] // tm
    side = pl.BlockSpec((tm, CONV_DIM), lambda i: (i % n_side, 0))
    prev = pl.BlockSpec((SUBLANES, CONV_DIM), lambda i: (jnp.maximum(i * (tm // SUBLANES) - 1, 0), 0))
    return pl.pallas_call(
        functools.partial(_mix_body, tm=tm, slen=slen),
        grid=(t // tm,),
        in_specs=[row(d), row(ATT_DIM), row(CONV_DIM), prev, row(CONV_DIM), row(d), row(d), side, side,
                  fixed((SUBLANES, CONV_DIM)), fixed((ATT_DIM, d)), fixed((CONV_DIM, d)), fixed((d, d))],
        out_specs=row(d),
        out_shape=jax.ShapeDtypeStruct((t, d), F32),
        scratch_shapes=[pltpu.VMEM((tm + SUBLANES, CONV_DIM), F32)],
        compiler_params=_params(("arbitrary",)),
    )(x, o_att, u, u, cb, sga, sgb, side1, side2, conv_w, wa, wc, wo)


def _ffn_body(x_ref, g_ref, win_ref, wout_ref, gf_ref, y_ref, *, d_ff, fc, final):
    x = x_ref[...]
    h = x * lax.rsqrt(jnp.mean(x * x, axis=-1, keepdims=True) + EPS)
    hb = (h * g_ref[...]).astype(BF16)
    acc = x
    for c in range(d_ff // fc):
        gate = _nn(hb, win_ref[:, c * fc:(c + 1) * fc])
        up = _nn(hb, win_ref[:, d_ff + c * fc:d_ff + (c + 1) * fc])
        act = (gate * jax.nn.sigmoid(gate) * up).astype(BF16)
        acc = acc + _nn(act, wout_ref[c * fc:(c + 1) * fc, :])
    if final:
        acc = acc * lax.rsqrt(jnp.mean(acc * acc, axis=-1, keepdims=True) + EPS) * gf_ref[...]
    y_ref[...] = acc


def _ffn(x, g, w_in, w_out, g_final, tm, final):
    t, d = x.shape
    d_ff = w_out.shape[0]
    fc = 256 if d_ff % 256 == 0 else LANES
    row = pl.BlockSpec((tm, d), lambda i: (i, 0))
    fixed = lambda shape: pl.BlockSpec(shape, lambda i: (0, 0))
    return pl.pallas_call(
        functools.partial(_ffn_body, d_ff=d_ff, fc=fc, final=final),
        grid=(t // tm,),
        in_specs=[row, fixed((1, d)), fixed((d, 2 * d_ff)), fixed((d_ff, d)), fixed((1, d))],
        out_specs=row,
        out_shape=jax.ShapeDtypeStruct((t, d), F32),
        compiler_params=_params(("arbitrary",)),
    )(x, g, w_in, w_out, g_final)


def _rope_tables(pos):
    inv = 1.0 / (ROPE_THETA ** (jnp.arange(ROPE_HALF, dtype=F32) * (2.0 / ROPE_DIM)))
    ang = pos.astype(F32)[:, None] * inv[None, :]
    cos, sin = jnp.cos(ang), jnp.sin(ang)
    t = pos.shape[0]
    pad = jnp.zeros((t, HEAD_DIM - ROPE_DIM), F32)
    rc = jnp.concatenate([cos, cos, pad + 1.0], axis=1)
    rs1 = jnp.concatenate([-sin, jnp.zeros_like(sin), pad], axis=1)
    rs2 = jnp.concatenate([jnp.zeros_like(sin), sin, pad], axis=1)
    rep = LANES // HEAD_DIM
    return tuple(jnp.tile(a, (1, rep)) for a in (rc, rs1, rs2))


def _pack_w_in(w):
    wt = w.T
    d = wt.shape[1]
    o_gl = ATT_DIM + 6 * KV_DIM
    gl = wt[o_gl:o_gl + 3 * N_HEADS].reshape(N_KV, 3 * GROUP, d)
    gl = jnp.pad(gl, ((0, 0), (0, LANES - 3 * GROUP), (0, 0))).reshape(N_KV * LANES, d)
    return jnp.concatenate([wt[:o_gl], gl, wt[o_gl + 3 * N_HEADS:]], axis=0).astype(BF16)


def _pack_cmp(pe, w1, w2):
    eye = jnp.eye(N_KV, dtype=F32)
    pe2 = jnp.broadcast_to(pe.reshape(2, CMP_STRIDE, 1, HEAD_DIM), (2, CMP_STRIDE, N_KV, HEAD_DIM)).reshape(2, CHUNK)
    w1r = w1.reshape(2, CMP_STRIDE, HEAD_DIM, CMP_HIDDEN)
    w1p = jnp.einsum('pjdh,kc->pjkdch', w1r, eye).reshape(2, CHUNK, HID2).astype(BF16)
    w2p = jnp.einsum('hd,kc->khcd', w2, eye).reshape(HID2, KV_DIM).astype(BF16)
    return pe2, w1p, w2p


def _overlap_t(nch):
    ci = jnp.arange(nch)[None, :] * CMP_STRIDE
    sj = jnp.arange(NS_PAD)[:, None] * SEL_BLOCK
    return ((ci <= sj + SEL_BLOCK - 1) & (ci + CMP_BLOCK - 1 >= sj)).astype(BF16)


def _dim_major(a):
    lead = a.shape[:-3]
    n = len(lead)
    return a.transpose(*range(n), n + 1, n + 2, n).reshape(*lead, KV_DIM, a.shape[-3])


def _pos_major(a):
    lead = a.shape[:-2]
    n = len(lead)
    return a.reshape(*lead, N_KV, HEAD_DIM, a.shape[-1]).transpose(*range(n), n + 2, n, n + 1)


def _prompt_attention(q, gates, kcvc, kskw, vt, cmpk, cmpv, nb, seq):
    nch = seq // CMP_STRIDE
    abk = _cmp_ab_rows(kcvc, nb * seq, 0, cmpk[0], cmpk[1], seq)
    abv = _cmp_ab_rows(kcvc, nb * seq, 1, cmpv[0], cmpv[1], seq)
    kcmp, vcmp = _cmp_fin(abk, abv, cmpk[2], cmpv[2], nch, nch)
    kcmp = kcmp.reshape(nb, nch, KV_DIM).astype(BF16)
    vcmp_t = vcmp.reshape(nb, nch, KV_DIM).transpose(0, 2, 1).astype(BF16)
    return _pattn(q, gates, kcmp, vcmp_t, kskw, vt, _overlap_t(nch), nb, seq)


def _sample_attention(q, gates, new4, cmpk, cmpv, pool_k, pool_v, slc_k, slc_v, win_k, win_v, page_table, layer, dec):
    ndb, n_pages = page_table.shape
    n_pool, _, page = slc_k.shape[1:]
    cpp = page // CMP_STRIDE
    past = n_pages * page
    nchp = past // CMP_STRIDE
    ncol = N_KV * GROUP * dec
    eye_kv = jnp.eye(N_KV, dtype=F32)
    pp = math.gcd(n_pool, POOL_PAGES)
    abk = _cmp_ab_pages(pool_k, layer, cmpk[0], cmpk[1], pp).reshape(n_pool, cpp, 2 * HID2)[page_table]
    abv = _cmp_ab_pages(pool_v, layer, cmpv[0], cmpv[1], pp).reshape(n_pool, cpp, 2 * HID2)[page_table]
    kcmp, vcmp = _cmp_fin(abk.reshape(ndb * nchp, 2 * HID2), abv.reshape(ndb * nchp, 2 * HID2), cmpk[2], cmpv[2],
                          nchp, math.gcd(ndb, SUBLANES) * nchp)
    q_s = q.reshape(ndb, dec, N_KV, GROUP, HEAD_DIM).transpose(0, 2, 3, 1, 4)
    qp = (q_s[:, :, :, :, None] * eye_kv[None, :, None, None, :, None]).reshape(ndb, ncol, KV_DIM)
    qp = jnp.pad(qp, ((0, 0), (0, LANES - ncol), (0, 0))).astype(BF16)
    g_s = gates.reshape(ndb, dec, N_KV, LANES)[..., :3 * GROUP].reshape(ndb, dec, N_KV, GROUP, 3)
    g_s = jnp.pad(g_s.transpose(0, 2, 3, 1, 4).reshape(ndb, ncol, 3), ((0, 0), (0, LANES - ncol), (0, SUBLANES - 3)))
    e_t = jax.nn.one_hot(jnp.arange(past) // SEL_BLOCK, NS_PAD, dtype=BF16).T
    e_new = jax.nn.one_hot((past + jnp.arange(LANES)) // SEL_BLOCK, NS_PAD, dtype=BF16)
    colid = jnp.arange(LANES)
    gsum = ((colid[:, None] // (GROUP * dec) == colid[None, :] // (GROUP * dec))
            & (colid[:, None] % dec == colid[None, :] % dec)).astype(BF16)
    ksn, vsn, kwn, vwn = (a.reshape(ndb, dec, KV_DIM) for a in new4)
    pages = layer * n_pool + page_table
    gather = lambda pool: pool.reshape(-1, KV_DIM, page)[pages]
    o_t = _sattn(qp, g_s, kcmp.reshape(ndb, nchp, KV_DIM), vcmp.reshape(ndb, nchp, KV_DIM),
                 gather(slc_k), gather(slc_v), win_k, win_v, layer,
                 ksn, vsn, kwn, vwn, _overlap_t(nchp), gsum, e_t, e_new, past, dec)
    o_t = o_t[:, :ncol].reshape(ndb, N_KV, GROUP, dec, N_KV, HEAD_DIM)
    return jnp.einsum('bkgtcd,kc->btkgd', o_t, eye_kv).reshape(ndb * dec, ATT_DIM)


def kernel(x_prompt, x_sample, cache_cmp_k, cache_cmp_v, cache_slc_k, cache_slc_v, state_win_k, state_win_v,
           state_conv, page_table, w_in, cmp_pe, cmp_w1, cmp_w2, w_att_out, conv_w, w_conv_out, w_o,
           norm_mix, norm_ffn, w_ffn_in, w_ffn_out, norm_final):
    nb, seq, d = x_prompt.shape
    ndb, dec, _ = x_sample.shape
    depth, n_pool, page = cache_cmp_k.shape[:3]
    n_pages = page_table.shape[1]
    past = n_pages * page
    lw = state_win_k.shape[2]
    tp, ts = nb * seq, ndb * dec
    tm_p, tm_s = math.gcd(ROW_TILE, seq), math.gcd(ROW_TILE, ts)
    nchp = past // CMP_STRIDE
    keep = min(WINDOW, seq)
    assert seq % SEL_GROUP == 0 and seq >= WIN_SPAN and seq // SEL_BLOCK <= NS_PAD
    assert dec <= SUBLANES and N_KV * GROUP * dec <= LANES and tm_s % SUBLANES == 0 and tm_s % dec == 0
    assert (past + dec - CMP_BLOCK) // CMP_STRIDE + 1 == nchp - 1
    assert -(-(past + dec) // SEL_BLOCK) <= NS_PAD and past >= WINDOW and lw == WINDOW

    rope_p = _rope_tables(jnp.arange(seq, dtype=jnp.int32))
    rope_s = _rope_tables(jnp.tile(past + jnp.arange(dec, dtype=jnp.int32), ndb))
    pool_k, pool_v, slc_k, slc_v, win_k, win_v = (
        _dim_major(a) for a in (cache_cmp_k, cache_cmp_v, cache_slc_k, cache_slc_v, state_win_k, state_win_v))
    no_side = jnp.zeros((tm_p, CONV_DIM), F32)
    zero = jnp.zeros((ndb, dec - 1, CONV_DIM), F32)

    xp, xs = x_prompt.reshape(tp, d), x_sample.reshape(ts, d)
    st_p = [[] for _ in range(7)]
    st_s = [[] for _ in range(7)]
    for l in range(depth):
        w_l = _pack_w_in(w_in[l])
        g_l = norm_mix[l][None, :]
        cmpk = _pack_cmp(cmp_pe[l, 0], cmp_w1[l, 0], cmp_w2[l, 0])
        cmpv = _pack_cmp(cmp_pe[l, 1], cmp_w1[l, 1], cmp_w2[l, 1])
        cw = jnp.pad(conv_w[l], ((0, SUBLANES - CONV_WIDTH), (0, 0)))
        w_mix = (w_att_out[l].astype(BF16), w_conv_out[l].astype(BF16), w_o[l].astype(BF16))
        w_ffn = (norm_ffn[l][None, :], w_ffn_in[l].astype(BF16), w_ffn_out[l].astype(BF16), norm_final[None, :])
        final = l == depth - 1

        q, kcvc, kskw, vt, stp, gates, u, cb, sga, sgb = _inproj(xp, g_l, w_l, rope_p, nb, tm_p)
        o_att = _prompt_attention(q, gates, kcvc, kskw, vt, cmpk, cmpv, nb, seq)
        xp = _mix(xp, o_att, u, cb, sga, sgb, no_side, no_side, cw, *w_mix, tm_p, seq)
        xp = _ffn(xp, *w_ffn, tm_p, final)
        for j in range(4):
            st_p[j].append(stp[j])
        st_p[4].append(stp[4][:, :, seq - keep:])
        st_p[5].append(stp[5][:, :, seq - keep:])
        st_p[6].append(jnp.stack([u[(b + 1) * seq - (CONV_WIDTH - 1):(b + 1) * seq] for b in range(nb)]))

        q, kcvc, kskw, vt, sts, gates, u, cb, sga, sgb = _inproj(xs, g_l, w_l, rope_s, 1, tm_s)
        new_t = sts[:, 0].reshape(6, KV_DIM, ndb, dec).transpose(0, 2, 1, 3)
        new_rows = lambda j: sts[j, 0].T
        o_att = _sample_attention(q, gates, [kskw[:, :KV_DIM], new_rows(3), kskw[:, KV_DIM:], new_rows(5)],
                                  cmpk, cmpv, pool_k, pool_v, slc_k, slc_v, win_k, win_v, page_table, l, dec)
        cbuf = state_conv[l]
        side1 = jnp.concatenate([cbuf[:, 1:2], zero], axis=1).reshape(ts, CONV_DIM)
        side2 = jnp.concatenate([cbuf[:, 0:2], zero[:, 1:]], axis=1).reshape(ts, CONV_DIM)
        xs = _mix(xs, o_att, u, cb, sga, sgb, side1, side2, cw, *w_mix, tm_s, dec)
        xs = _ffn(xs, *w_ffn, tm_s, final)
        for j in range(6):
            st_s[j].append(new_t[j])
        st_s[6].append(jnp.concatenate([cbuf, u.reshape(ndb, dec, CONV_DIM)], axis=1)[:, -(CONV_WIDTH - 1):])

    kv_p = [_pos_major(jnp.stack(a)) for a in st_p[:6]]
    kv_s = [jnp.stack(a) for a in st_s[:6]]
    kv_s[4] = jnp.concatenate([win_k[:, :, :, dec:], kv_s[4]], axis=-1)
    kv_s[5] = jnp.concatenate([win_v[:, :, :, dec:], kv_s[5]], axis=-1)
    kv_s = [_pos_major(a) for a in kv_s]
    return (xp.reshape(nb, seq, d), xs.reshape(ndb, dec, d), *kv_p, jnp.stack(st_p[6]),
            *kv_s, jnp.stack(st_s[6]))
```

```python
import functools
import math

import jax
import jax.numpy as jnp
from jax import lax
from jax.experimental import pallas as pl
from jax.experimental.pallas import tpu as pltpu

N_HEADS = 8
HEAD_DIM = 64
N_KV = 2
GROUP = N_HEADS // N_KV
ATT_DIM = N_HEADS * HEAD_DIM
KV_DIM = N_KV * HEAD_DIM
ROPE_DIM = HEAD_DIM // 4
ROPE_HALF = ROPE_DIM // 2
ROPE_THETA = 500000.0
CMP_BLOCK = 32
CMP_STRIDE = 16
CMP_HIDDEN = 2 * HEAD_DIM
SEL_BLOCK = 64
SEL_TOPK = 16
N_LOCAL_BLOCKS = 2
WINDOW = 512
CONV_DIM = 512
CONV_WIDTH = 3
EPS = 1e-6
NEG = -1e30
FORCE_SCORE = 1e4

LANES = 128
SUBLANES = 8
Q_TILE = 128
SEL_CHUNK = 256
SEL_GROUP = 4 * SEL_CHUNK
LOG2E = 1.4426950408889634
VT_ROWS = HEAD_DIM + 16
WIN_SPAN = WINDOW + Q_TILE
ROW_TILE = 512
POOL_PAGES = 32
NS_PAD = LANES
VMEM_LIMIT = 56 * 1024 * 1024

C_Q = 0
C_KV = C_Q + ATT_DIM
C_GL = C_KV + 6 * KV_DIM
C_CU = C_GL + N_KV * LANES
C_CB = C_CU + CONV_DIM
C_CC = C_CB + CONV_DIM
F32 = jnp.float32
BF16 = jnp.bfloat16


def _nt(a, b):
    return lax.dot_general(a, b, (((1,), (1,)), ((), ())), preferred_element_type=F32)


def _nn(a, b):
    return jnp.dot(a, b, preferred_element_type=F32)


def _split3(x):
    hi = x.astype(BF16)
    r1 = x - hi.astype(F32)
    mid = r1.astype(BF16)
    lo = (r1 - mid.astype(F32)).astype(BF16)
    return hi, mid, lo


def _params(sem):
    return pltpu.CompilerParams(dimension_semantics=sem, vmem_limit_bytes=VMEM_LIMIT)


def _inproj_body(x_ref, g_ref, w_ref, rc_ref, rs1_ref, rs2_ref,
                 q_ref, kcvc_ref, kskw_ref, vt_ref, st_ref, gate_ref, u_ref, cb_ref, sga_ref, sgb_ref, *, d_model):
    x = x_ref[...]
    h = x * lax.rsqrt(jnp.mean(x * x, axis=-1, keepdims=True) + EPS)
    hb = (h * g_ref[...]).astype(BF16)
    rc, rs1, rs2 = rc_ref[...], rs1_ref[...], rs2_ref[...]

    def mm(lo, width):
        return _nt(hb, w_ref[lo:lo + width, :])

    def rope(z):
        return z * rc + pltpu.roll(z, LANES - ROPE_HALF, 1) * rs1 + pltpu.roll(z, ROPE_HALF, 1) * rs2

    for j in range(ATT_DIM // 256):
        z = mm(C_Q + 256 * j, 256)
        for t in range(2):
            q_ref[:, 256 * j + LANES * t:256 * j + LANES * (t + 1)] = (
                rope(z[:, LANES * t:LANES * (t + 1)]) * (HEAD_DIM ** -0.5))
    for j in range(3):
        z = mm(C_KV + 256 * j, 256)
        k, v = rope(z[:, :LANES]), z[:, LANES:]
        kt, vt = k.T, v.T
        st_ref[2 * j, 0] = kt
        st_ref[2 * j + 1, 0] = vt
        if j == 0:
            kcvc_ref[:, :LANES] = k
            kcvc_ref[:, LANES:] = v
        else:
            kskw_ref[:, LANES * (j - 1):LANES * j] = k.astype(BF16)
            ones_row = (lax.broadcasted_iota(jnp.int32, (VT_ROWS - HEAD_DIM, vt.shape[1]), 0) == 0).astype(BF16)
            for kvh in range(N_KV):
                vt_ref[j - 1, 0, VT_ROWS * kvh:VT_ROWS * kvh + HEAD_DIM] = vt[HEAD_DIM * kvh:HEAD_DIM * (kvh + 1)].astype(BF16)
                vt_ref[j - 1, 0, VT_ROWS * kvh + HEAD_DIM:VT_ROWS * (kvh + 1)] = ones_row
    gate_ref[...] = jax.nn.sigmoid(mm(C_GL, N_KV * LANES))
    for j in range(CONV_DIM // 256):
        u_ref[:, 256 * j:256 * (j + 1)] = mm(C_CC + 256 * j, 256) * mm(C_CU + 256 * j, 256)
        cb_ref[:, 256 * j:256 * (j + 1)] = mm(C_CB + 256 * j, 256)
    c_ga = C_CC + CONV_DIM
    c_gb = c_ga + d_model
    for j in range(d_model // 256):
        sga_ref[:, 256 * j:256 * (j + 1)] = jax.nn.sigmoid(mm(c_ga + 256 * j, 256))
        sgb_ref[:, 256 * j:256 * (j + 1)] = jax.nn.sigmoid(mm(c_gb + 256 * j, 256))


def _inproj(x, g, w, rope, nseq, tm):
    t, d = x.shape
    slen = t // nseq
    tps = slen // tm
    nrows = w.shape[0]
    row = lambda width: pl.BlockSpec((tm, width), lambda i: (i, 0))
    fixed = lambda shape: pl.BlockSpec(shape, lambda i: (0, 0))
    tab = pl.BlockSpec((tm, LANES), lambda i: (i % tps, 0))
    tr = lambda n, rows: pl.BlockSpec((n, 1, rows, tm), lambda i: (0, i // tps, 0, i % tps))
    sds = jax.ShapeDtypeStruct
    return pl.pallas_call(
        functools.partial(_inproj_body, d_model=d),
        grid=(t // tm,),
        in_specs=[row(d), fixed((1, d)), fixed((nrows, d)), tab, tab, tab],
        out_specs=[row(ATT_DIM), row(2 * KV_DIM), row(2 * KV_DIM), tr(2, N_KV * VT_ROWS), tr(6, KV_DIM), row(N_KV * LANES),
                   row(CONV_DIM), row(CONV_DIM), row(d), row(d)],
        out_shape=[sds((t, ATT_DIM), F32), sds((t, 2 * KV_DIM), F32), sds((t, 2 * KV_DIM), BF16),
                   sds((2, nseq, N_KV * VT_ROWS, slen), BF16), sds((6, nseq, KV_DIM, slen), F32), sds((t, N_KV * LANES), F32),
                   sds((t, CONV_DIM), F32), sds((t, CONV_DIM), F32), sds((t, d), F32), sds((t, d), F32)],
        compiler_params=_params(("arbitrary",)),
    )(x, g, w, *rope)


CHUNK = CMP_STRIDE * KV_DIM
HID2 = N_KV * CMP_HIDDEN


def _cmp_partials(src, pe_ref, w_ref, ab_ref):
    n = src.shape[0] // CMP_STRIDE
    acc_a = jnp.zeros((n, HID2), F32)
    acc_b = jnp.zeros((n, HID2), F32)
    for jp in range(CMP_STRIDE // 2):
        x = jnp.concatenate([src[pl.ds(2 * jp, n, stride=CMP_STRIDE), :],
                             src[pl.ds(2 * jp + 1, n, stride=CMP_STRIDE), :]], axis=1)
        cols = slice(2 * KV_DIM * jp, 2 * KV_DIM * (jp + 1))
        acc_a = acc_a + _nn((x + pe_ref[0:1, cols]).astype(BF16), w_ref[0, cols, :])
        acc_b = acc_b + _nn((x + pe_ref[1:2, cols]).astype(BF16), w_ref[1, cols, :])
    ab_ref[:, :HID2] = acc_a
    ab_ref[:, HID2:] = acc_b


def _cmp_ab_rows_body(x_ref, pe_ref, w_ref, ab_ref):
    _cmp_partials(x_ref, pe_ref, w_ref, ab_ref)


def _cmp_ab_rows(x, rows, col, pe, w, tr):
    return pl.pallas_call(
        _cmp_ab_rows_body,
        grid=(rows // tr,),
        in_specs=[pl.BlockSpec((tr, KV_DIM), lambda i: (i, col)),
                  pl.BlockSpec((2, CHUNK), lambda i: (0, 0)),
                  pl.BlockSpec((2, CHUNK, HID2), lambda i: (0, 0, 0))],
        out_specs=pl.BlockSpec((tr // CMP_STRIDE, 2 * HID2), lambda i: (i, 0)),
        out_shape=jax.ShapeDtypeStruct((rows // CMP_STRIDE, 2 * HID2), F32),
        compiler_params=_params(("arbitrary",)),
    )(x, pe, w)


def _cmp_ab_pages_body(x_ref, pe_ref, w_ref, ab_ref, rows_scr):
    page = x_ref.shape[3]
    for p in range(x_ref.shape[1]):
        rows_scr[page * p:page * (p + 1), :] = x_ref[0, p].T
    _cmp_partials(rows_scr, pe_ref, w_ref, ab_ref)


def _cmp_ab_pages(pool, layer, pe, w, pp):
    _, n_pool, _, page = pool.shape
    cpp = page // CMP_STRIDE
    return pl.pallas_call(
        _cmp_ab_pages_body,
        grid=(n_pool // pp,),
        in_specs=[pl.BlockSpec((1, pp, KV_DIM, page), lambda i: (layer, i, 0, 0)),
                  pl.BlockSpec((2, CHUNK), lambda i: (0, 0)),
                  pl.BlockSpec((2, CHUNK, HID2), lambda i: (0, 0, 0))],
        out_specs=pl.BlockSpec((pp * cpp, 2 * HID2), lambda i: (i, 0)),
        out_shape=jax.ShapeDtypeStruct((n_pool * cpp, 2 * HID2), F32),
        scratch_shapes=[pltpu.VMEM((pp * page, KV_DIM), F32)],
        compiler_params=_params(("arbitrary",)),
    )(pool, pe, w)


def _cmp_fin_body(abk_ref, abv_ref, w2k_ref, w2v_ref, ok_ref, ov_ref, *, nch):
    rows = abk_ref.shape[0]
    valid = lax.broadcasted_iota(jnp.int32, (rows, 1), 0) % nch < nch - 1

    def fin(ab_ref, w2_ref, o_ref):
        ab = ab_ref[...]
        hid = ab[:, :HID2] + pltpu.roll(ab[:, HID2:], rows - 1, 0)
        act = hid * jax.nn.sigmoid(hid)
        o_ref[...] = jnp.where(valid, _nn(act.astype(BF16), w2_ref[...]), 0.0)

    fin(abk_ref, w2k_ref, ok_ref)
    fin(abv_ref, w2v_ref, ov_ref)


def _cmp_fin(abk, abv, w2k, w2v, nch, tr):
    rows = abk.shape[0]
    ab_spec = pl.BlockSpec((tr, 2 * HID2), lambda i: (i, 0))
    w_spec = pl.BlockSpec((HID2, KV_DIM), lambda i: (0, 0))
    o_spec = pl.BlockSpec((tr, KV_DIM), lambda i: (i, 0))
    return pl.pallas_call(
        functools.partial(_cmp_fin_body, nch=nch),
        grid=(rows // tr,),
        in_specs=[ab_spec, ab_spec, w_spec, w_spec],
        out_specs=[o_spec, o_spec],
        out_shape=[jax.ShapeDtypeStruct((rows, KV_DIM), F32)] * 2,
        compiler_params=_params(("arbitrary",)),
    )(abk, abv, w2k, w2v)


def _select_bias(imp_t, qpos, ntop):
    shape = imp_t.shape
    s_io = lax.broadcasted_iota(jnp.int32, shape, 0)
    cur = qpos // SEL_BLOCK
    avail = s_io <= cur
    forced = (s_io == 0) | (avail & (s_io > cur - N_LOCAL_BLOCKS))
    val = jnp.where(avail, jnp.where(forced, FORCE_SCORE, imp_t), NEG)

    def pick_one(_, carry):
        val, sel = carry
        m = jnp.max(val, axis=0, keepdims=True)
        idx = jnp.min(jnp.where(val == m, s_io, NS_PAD), axis=0, keepdims=True)
        pick = s_io == idx
        return jnp.where(pick, -jnp.inf, val), jnp.where(pick, 1.0, sel)

    _, sel = lax.fori_loop(0, ntop, pick_one, (val, jnp.zeros(shape, F32)))
    return jnp.where((sel > 0.0) & avail, 0.0, NEG)


def _pattn_body(q_ref, g_ref, kc_ref, vct_ref, ks_ref, kw_ref, vst_ref, vwt_ref, ovt_ref, o_ref, bias_scr, *, seq):
    kv = pl.program_id(1)
    i = pl.program_id(2)
    nch = seq // CMP_STRIDE
    cols = GROUP * Q_TILE
    q = q_ref[...]
    qt = jnp.concatenate([q[:, :LANES].T, q[:, LANES:].T], axis=0)
    qt4 = jnp.concatenate([qt[HEAD_DIM * h:HEAD_DIM * (h + 1)] for h in range(GROUP)], axis=1)
    row_kv = lax.broadcasted_iota(jnp.int32, (KV_DIM, 1), 0) // HEAD_DIM
    qt_pad = jnp.where(row_kv == kv, jnp.concatenate([qt4] * N_KV, axis=0) * LOG2E, 0.0).astype(BF16)
    qpos1 = i * Q_TILE + lax.broadcasted_iota(jnp.int32, (1, Q_TILE), 1)
    qpos = jnp.concatenate([qpos1] * GROUP, axis=1)

    s = _nn(kc_ref[0], qt_pad)
    cmask = lax.broadcasted_iota(jnp.int32, (nch, 1), 0) * CMP_STRIDE + (CMP_BLOCK - 1) <= qpos
    s = jnp.where(cmask, s, NEG)
    p = jnp.where(cmask, jnp.exp2(s - jnp.max(s, axis=0, keepdims=True)), 0.0)
    l = jnp.sum(p, axis=0, keepdims=True)
    pc = p * jnp.where(l > 0.0, 1.0 / l, 0.0)
    o_c = _nn(vct_ref[0], pc.astype(BF16))

    pcs = pc[:, :Q_TILE]
    for h in range(1, GROUP):
        pcs = pcs + pc[:, h * Q_TILE:(h + 1) * Q_TILE]
    ovt = ovt_ref[...]
    imp_t = sum(_nn(ovt, part) for part in _split3(pcs))
    bias_scr[...] = _select_bias(imp_t, qpos1, min(SEL_TOPK, seq // SEL_BLOCK))

    def group(gj, carry, causal):
        m, acc = carry
        g0 = pl.multiple_of(gj * SEL_GROUP, SEL_GROUP)
        s_all = _nn(ks_ref[pl.ds(g0, SEL_GROUP), :], qt_pad)
        for c in range(SEL_GROUP // SEL_CHUNK):
            parts = []
            for r in range(SEL_CHUNK // SEL_BLOCK):
                row = SEL_CHUNK * c + SEL_BLOCK * r
                b_row = bias_scr[pl.ds(gj * (SEL_GROUP // SEL_BLOCK) + row // SEL_BLOCK, 1), :]
                parts.append(s_all[row:row + SEL_BLOCK] + jnp.concatenate([b_row] * GROUP, axis=1))
            s = jnp.concatenate(parts, axis=0)
            k0 = pl.multiple_of(g0 + SEL_CHUNK * c, SEL_CHUNK)
            if causal:
                kpos = k0 + lax.broadcasted_iota(jnp.int32, (SEL_CHUNK, 1), 0)
                s = jnp.where(kpos <= qpos, s, NEG)
            m_new = jnp.maximum(m, jnp.max(s, axis=0, keepdims=True))
            alpha = jnp.exp2(m - m_new)
            p = jnp.exp2(s - m_new)
            acc = alpha * acc + _nn(vst_ref[0, 0, :, pl.ds(k0, SEL_CHUNK)], p.astype(BF16))
            m = m_new
        return m, acc

    n_below = (i * Q_TILE) // SEL_GROUP
    init = (jnp.full((1, cols), -3e38, F32), jnp.zeros((VT_ROWS, cols), F32))
    carry = lax.fori_loop(0, n_below, lambda gj, c: group(gj, c, False), init)
    _, acc_s = group(n_below, carry, True)
    o_s = acc_s[:HEAD_DIM] / acc_s[HEAD_DIM:HEAD_DIM + 1]

    w0 = pl.multiple_of(jnp.maximum(i * Q_TILE - WINDOW, 0), Q_TILE)
    kpos = w0 + lax.broadcasted_iota(jnp.int32, (WIN_SPAN, 1), 0)
    wmask = (kpos <= qpos) & (kpos > qpos - WINDOW)
    s = jnp.where(wmask, _nn(kw_ref[pl.ds(w0, WIN_SPAN), :], qt_pad), NEG)
    p = jnp.exp2(s - jnp.max(s, axis=0, keepdims=True))
    acc_w = _nn(vwt_ref[0, 0, :, pl.ds(w0, WIN_SPAN)], p.astype(BF16))
    o_w = acc_w[:HEAD_DIM] / acc_w[HEAD_DIM:HEAD_DIM + 1]

    gt = g_ref[...].T
    outs = []
    for h in range(GROUP):
        c = slice(h * Q_TILE, (h + 1) * Q_TILE)
        outs.append(gt[3 * h:3 * h + 1] * o_c[:, c] + gt[3 * h + 1:3 * h + 2] * o_s[:, c]
                    + gt[3 * h + 2:3 * h + 3] * o_w[:, c])
    for t in range(GROUP // 2):
        o_ref[:, LANES * t:LANES * (t + 1)] = jnp.concatenate(outs[2 * t:2 * t + 2], axis=0).T


def _pattn(q, gates, kcmp, vcmp_t, kskw, vt, ovt, nb, seq):
    nq = seq // Q_TILE
    nch = seq // CMP_STRIDE
    qspec = pl.BlockSpec((Q_TILE, GROUP * HEAD_DIM), lambda b, k, i: (b * nq + i, k))
    gspec = pl.BlockSpec((Q_TILE, LANES), lambda b, k, i: (b * nq + i, k))
    kspec = lambda which: pl.BlockSpec((seq, KV_DIM), lambda b, k, i: (b, which))
    vspec = lambda which: pl.BlockSpec((1, 1, VT_ROWS, seq), lambda b, k, i: (which, b, k, 0))
    return pl.pallas_call(
        functools.partial(_pattn_body, seq=seq),
        grid=(nb, N_KV, nq),
        in_specs=[qspec, gspec,
                  pl.BlockSpec((1, nch, KV_DIM), lambda b, k, i: (b, 0, 0)),
                  pl.BlockSpec((1, HEAD_DIM, nch), lambda b, k, i: (b, k, 0)),
                  kspec(0), kspec(1), vspec(0), vspec(1),
                  pl.BlockSpec((NS_PAD, nch), lambda b, k, i: (0, 0))],
        out_specs=qspec,
        out_shape=jax.ShapeDtypeStruct((nb * seq, ATT_DIM), F32),
        scratch_shapes=[pltpu.VMEM((NS_PAD, Q_TILE), F32)],
        compiler_params=_params(("arbitrary", "arbitrary", "arbitrary")),
    )(q, gates, kcmp, vcmp_t, kskw, kskw, vt, vt, ovt)


def _sattn_body(qp_ref, g_ref, kc_ref, vc_ref, ks_ref, vs_ref, kw_ref, vw_ref,
                ksn_ref, vsn_ref, kwn_ref, vwn_ref, ovt_ref, gsum_ref, et_ref, en_ref, o_ref,
                *, past, dec, ntop):
    nchp = kc_ref.shape[1]
    n_pages = ks_ref.shape[1]
    lw = kw_ref.shape[3]
    qp = qp_ref[0]
    nrow = qp.shape[0]
    tok_r = lax.broadcasted_iota(jnp.int32, (nrow, 1), 0) % dec
    tok_l = lax.broadcasted_iota(jnp.int32, (1, LANES), 1) % dec
    new_j = lax.broadcasted_iota(jnp.int32, (1, LANES), 1)
    new_mask = (new_j < dec) & (new_j <= tok_r)

    def pad_rows(x):
        return jnp.concatenate([x.astype(F32), jnp.zeros((LANES - dec, KV_DIM), F32)], axis=0).astype(BF16)

    def softmax2(s_p, s_n):
        m = jnp.maximum(jnp.max(s_p, axis=-1, keepdims=True), jnp.max(s_n, axis=-1, keepdims=True))
        p_p, p_n = jnp.exp(s_p - m), jnp.exp(s_n - m)
        l = jnp.sum(p_p, axis=-1, keepdims=True) + jnp.sum(p_n, axis=-1, keepdims=True)
        return p_p.astype(BF16), p_n.astype(BF16), l

    s = _nn(qp, kc_ref[0].T.astype(BF16))
    cmask = lax.broadcasted_iota(jnp.int32, (1, nchp), 1) < nchp - 1
    s = jnp.where(cmask, s, NEG)
    p = jnp.where(cmask, jnp.exp(s - jnp.max(s, axis=-1, keepdims=True)), 0.0)
    pc = p / jnp.sum(p, axis=-1, keepdims=True)
    o_c = _nn(pc.astype(BF16), vc_ref[0].astype(BF16))

    gsum = gsum_ref[...]
    pcs = sum(_nn(gsum, part) for part in _split3(pc))
    pcs = jnp.concatenate([pcs, jnp.zeros((LANES - nrow, nchp), F32)], axis=0)
    ovt = ovt_ref[...]
    imp_t = sum(_nt(ovt, part) for part in _split3(pcs))
    bias = _select_bias(imp_t, past + tok_l, ntop).T[:nrow].astype(BF16)

    kt = jnp.concatenate([ks_ref[0, pg] for pg in range(n_pages)], axis=1).astype(BF16)
    vt = jnp.concatenate([vs_ref[0, pg] for pg in range(n_pages)], axis=1).astype(BF16)
    s_p = _nn(qp, kt) + _nn(bias, et_ref[...])
    s_n = jnp.where(new_mask, _nt(qp, pad_rows(ksn_ref[0])) + _nt(bias, en_ref[...]), NEG)
    p_p, p_n, l = softmax2(s_p, s_n)
    o_s = (_nt(p_p, vt) + _nn(p_n, pad_rows(vsn_ref[0]))) / l

    kpos = (past - lw) + lax.broadcasted_iota(jnp.int32, (1, lw), 1)
    s_p = jnp.where(kpos > past + tok_r - WINDOW, _nn(qp, kw_ref[0, 0].astype(BF16)), NEG)
    s_n = jnp.where(new_mask, _nt(qp, pad_rows(kwn_ref[0])), NEG)
    p_p, p_n, l = softmax2(s_p, s_n)
    o_w = (_nt(p_p, vw_ref[0, 0].astype(BF16)) + _nn(p_n, pad_rows(vwn_ref[0]))) / l

    g = g_ref[0]
    o_ref[0] = g[:, 0:1] * o_c + g[:, 1:2] * o_s + g[:, 2:3] * o_w


def _sattn(qp, g, kc, vc, ks, vs, win_k, win_v, layer, ksn, vsn, kwn, vwn, ovt, gsum, e_t, e_new, past, dec):
    nseq, nrow = qp.shape[:2]
    nchp = kc.shape[1]
    n_pages, _, page = ks.shape[1:]
    lw = win_k.shape[3]
    ntop = min(SEL_TOPK, -(-(past + dec) // SEL_BLOCK))
    seq3 = lambda n, w: pl.BlockSpec((1, n, w), lambda b: (b, 0, 0))
    fixed = lambda r, c: pl.BlockSpec((r, c), lambda b: (0, 0))
    pages = pl.BlockSpec((1, n_pages, KV_DIM, page), lambda b: (b, 0, 0, 0))
    win = pl.BlockSpec((1, 1, KV_DIM, lw), lambda b: (layer, b, 0, 0))
    return pl.pallas_call(
        functools.partial(_sattn_body, past=past, dec=dec, ntop=ntop),
        grid=(nseq,),
        in_specs=[seq3(nrow, KV_DIM), seq3(nrow, SUBLANES), seq3(nchp, KV_DIM), seq3(nchp, KV_DIM),
                  pages, pages, win, win,
                  seq3(dec, KV_DIM), seq3(dec, KV_DIM), seq3(dec, KV_DIM), seq3(dec, KV_DIM),
                  fixed(NS_PAD, nchp), fixed(nrow, nrow), fixed(NS_PAD, past), fixed(LANES, NS_PAD)],
        out_specs=seq3(nrow, KV_DIM),
        out_shape=jax.ShapeDtypeStruct((nseq, nrow, KV_DIM), F32),
        compiler_params=_params(("arbitrary",)),
    )(qp, g, kc, vc, ks, vs, win_k, win_v, ksn, vsn, kwn, vwn, ovt, gsum, e_t, e_new)


def _mix_body(x_ref, o_ref, u_ref, uprev_ref, cb_ref, sga_ref, sgb_ref, s1_ref, s2_ref,
              cw_ref, wa_ref, wc_ref, wo_ref, y_ref, ubuf, *, tm, slen):
    i = pl.program_id(0)
    u = u_ref[...]
    ubuf[0:SUBLANES, :] = uprev_ref[...]
    ubuf[SUBLANES:, :] = u
    pos = (i * tm + lax.broadcasted_iota(jnp.int32, (tm, 1), 0)) % slen
    prev1 = jnp.where(pos >= 1, ubuf[SUBLANES - 1:SUBLANES - 1 + tm, :], s1_ref[...])
    prev2 = jnp.where(pos >= 2, ubuf[SUBLANES - 2:SUBLANES - 2 + tm, :], s2_ref[...])
    cw = cw_ref[...]
    y_conv = cw[0:1] * prev2 + cw[1:2] * prev1 + cw[2:3] * u
    a_br = _nn(o_ref[...].astype(BF16), wa_ref[...])
    c_br = _nn((cb_ref[...] * y_conv).astype(BF16), wc_ref[...])
    mix = sga_ref[...] * a_br + sgb_ref[...] * c_br
    y_ref[...] = x_ref[...] + _nn(mix.astype(BF16), wo_ref[...])


def _mix(x, o_att, u, cb, sga, sgb, side1, side2, conv_w, wa, wc, wo, tm, slen):
    t, d = x.shape
    row = lambda width: pl.BlockSpec((tm, width), lambda i: (i, 0))
    fixed = lambda shape: pl.BlockSpec(shape, lambda i: (0, 0))
    n_side = side1.shape[0] // tm
    side = pl.BlockSpec((tm, CONV_DIM), lambda i: (i % n_side, 0))
    prev = pl.BlockSpec((SUBLANES, CONV_DIM), lambda i: (jnp.maximum(i * (tm // SUBLANES) - 1, 0), 0))
    return pl.pallas_call(
        functools.partial(_mix_body, tm=tm, slen=slen),
        grid=(t // tm,),
        in_specs=[row(d), row(ATT_DIM), row(CONV_DIM), prev, row(CONV_DIM), row(d), row(d), side, side,
                  fixed((SUBLANES, CONV_DIM)), fixed((ATT_DIM, d)), fixed((CONV_DIM, d)), fixed((d, d))],
        out_specs=row(d),
        out_shape=jax.ShapeDtypeStruct((t, d), F32),
        scratch_shapes=[pltpu.VMEM((tm + SUBLANES, CONV_DIM), F32)],
        compiler_params=_params(("arbitrary",)),
    )(x, o_att, u, u, cb, sga, sgb, side1, side2, conv_w, wa, wc, wo)


def _ffn_body(x_ref, g_ref, win_ref, wout_ref, gf_ref, y_ref, *, d_ff, fc, final):
    x = x_ref[...]
    h = x * lax.rsqrt(jnp.mean(x * x, axis=-1, keepdims=True) + EPS)
    hb = (h * g_ref[...]).astype(BF16)
    acc = x
    for c in range(d_ff // fc):
        gate = _nn(hb, win_ref[:, c * fc:(c + 1) * fc])
        up = _nn(hb, win_ref[:, d_ff + c * fc:d_ff + (c + 1) * fc])
        act = (gate * jax.nn.sigmoid(gate) * up).astype(BF16)
        acc = acc + _nn(act, wout_ref[c * fc:(c + 1) * fc, :])
    if final:
        acc = acc * lax.rsqrt(jnp.mean(acc * acc, axis=-1, keepdims=True) + EPS) * gf_ref[...]
    y_ref[...] = acc


def _ffn(x, g, w_in, w_out, g_final, tm, final):
    t, d = x.shape
    d_ff = w_out.shape[0]
    fc = 256 if d_ff % 256 == 0 else LANES
    row = pl.BlockSpec((tm, d), lambda i: (i, 0))
    fixed = lambda shape: pl.BlockSpec(shape, lambda i: (0, 0))
    return pl.pallas_call(
        functools.partial(_ffn_body, d_ff=d_ff, fc=fc, final=final),
        grid=(t // tm,),
        in_specs=[row, fixed((1, d)), fixed((d, 2 * d_ff)), fixed((d_ff, d)), fixed((1, d))],
        out_specs=row,
        out_shape=jax.ShapeDtypeStruct((t, d), F32),
        compiler_params=_params(("arbitrary",)),
    )(x, g, w_in, w_out, g_final)


def _rope_tables(pos):
    inv = 1.0 / (ROPE_THETA ** (jnp.arange(ROPE_HALF, dtype=F32) * (2.0 / ROPE_DIM)))
    ang = pos.astype(F32)[:, None] * inv[None, :]
    cos, sin = jnp.cos(ang), jnp.sin(ang)
    t = pos.shape[0]
    pad = jnp.zeros((t, HEAD_DIM - ROPE_DIM), F32)
    rc = jnp.concatenate([cos, cos, pad + 1.0], axis=1)
    rs1 = jnp.concatenate([-sin, jnp.zeros_like(sin), pad], axis=1)
    rs2 = jnp.concatenate([jnp.zeros_like(sin), sin, pad], axis=1)
    rep = LANES // HEAD_DIM
    return tuple(jnp.tile(a, (1, rep)) for a in (rc, rs1, rs2))


def _pack_w_in(w):
    wt = w.T
    d = wt.shape[1]
    o_gl = ATT_DIM + 6 * KV_DIM
    gl = wt[o_gl:o_gl + 3 * N_HEADS].reshape(N_KV, 3 * GROUP, d)
    gl = jnp.pad(gl, ((0, 0), (0, LANES - 3 * GROUP), (0, 0))).reshape(N_KV * LANES, d)
    return jnp.concatenate([wt[:o_gl], gl, wt[o_gl + 3 * N_HEADS:]], axis=0).astype(BF16)


def _pack_cmp(pe, w1, w2):
    eye = jnp.eye(N_KV, dtype=F32)
    pe2 = jnp.broadcast_to(pe.reshape(2, CMP_STRIDE, 1, HEAD_DIM), (2, CMP_STRIDE, N_KV, HEAD_DIM)).reshape(2, CHUNK)
    w1r = w1.reshape(2, CMP_STRIDE, HEAD_DIM, CMP_HIDDEN)
    w1p = jnp.einsum('pjdh,kc->pjkdch', w1r, eye).reshape(2, CHUNK, HID2).astype(BF16)
    w2p = jnp.einsum('hd,kc->khcd', w2, eye).reshape(HID2, KV_DIM).astype(BF16)
    return pe2, w1p, w2p


def _overlap_t(nch):
    ci = jnp.arange(nch)[None, :] * CMP_STRIDE
    sj = jnp.arange(NS_PAD)[:, None] * SEL_BLOCK
    return ((ci <= sj + SEL_BLOCK - 1) & (ci + CMP_BLOCK - 1 >= sj)).astype(BF16)


def _dim_major(a):
    lead = a.shape[:-3]
    n = len(lead)
    return a.transpose(*range(n), n + 1, n + 2, n).reshape(*lead, KV_DIM, a.shape[-3])


def _pos_major(a):
    lead = a.shape[:-2]
    n = len(lead)
    return a.reshape(*lead, N_KV, HEAD_DIM, a.shape[-1]).transpose(*range(n), n + 2, n, n + 1)


def _prompt_attention(q, gates, kcvc, kskw, vt, cmpk, cmpv, nb, seq):
    nch = seq // CMP_STRIDE
    abk = _cmp_ab_rows(kcvc, nb * seq, 0, cmpk[0], cmpk[1], seq)
    abv = _cmp_ab_rows(kcvc, nb * seq, 1, cmpv[0], cmpv[1], seq)
    kcmp, vcmp = _cmp_fin(abk, abv, cmpk[2], cmpv[2], nch, nch)
    kcmp = kcmp.reshape(nb, nch, KV_DIM).astype(BF16)
    vcmp_t = vcmp.reshape(nb, nch, KV_DIM).transpose(0, 2, 1).astype(BF16)
    return _pattn(q, gates, kcmp, vcmp_t, kskw, vt, _overlap_t(nch), nb, seq)


def _sample_cache(cmpk, cmpv, pool_k, pool_v, slc_k, slc_v, page_table, layer):
    ndb, n_pages = page_table.shape
    n_pool, _, page = slc_k.shape[1:]
    cpp = page // CMP_STRIDE
    nchp = n_pages * cpp
    pp = math.gcd(n_pool, POOL_PAGES)
    abk = _cmp_ab_pages(pool_k, layer, cmpk[0], cmpk[1], pp).reshape(n_pool, cpp, 2 * HID2)[page_table]
    abv = _cmp_ab_pages(pool_v, layer, cmpv[0], cmpv[1], pp).reshape(n_pool, cpp, 2 * HID2)[page_table]
    kcmp, vcmp = _cmp_fin(abk.reshape(ndb * nchp, 2 * HID2), abv.reshape(ndb * nchp, 2 * HID2), cmpk[2], cmpv[2],
                          nchp, math.gcd(ndb, SUBLANES) * nchp)
    pages = layer * n_pool + page_table
    gather = lambda pool: pool.reshape(-1, KV_DIM, page)[pages]
    return kcmp.reshape(ndb, nchp, KV_DIM), vcmp.reshape(ndb, nchp, KV_DIM), gather(slc_k), gather(slc_v)


def _sample_attention(q, gates, new4, cache, win_k, win_v, layer, past, dec):
    kcmp, vcmp, ks_pages, vs_pages = cache
    ndb, nchp = kcmp.shape[:2]
    nrow = N_KV * GROUP * dec
    eye_kv = jnp.eye(N_KV, dtype=F32)
    q_s = q.reshape(ndb, dec, N_KV, GROUP, HEAD_DIM).transpose(0, 2, 3, 1, 4)
    qp = (q_s[:, :, :, :, None] * eye_kv[None, :, None, None, :, None]).reshape(ndb, nrow, KV_DIM).astype(BF16)
    g_s = gates.reshape(ndb, dec, N_KV, LANES)[..., :3 * GROUP].reshape(ndb, dec, N_KV, GROUP, 3)
    g_s = jnp.pad(g_s.transpose(0, 2, 3, 1, 4).reshape(ndb, nrow, 3), ((0, 0), (0, 0), (0, SUBLANES - 3)))
    e_t = jax.nn.one_hot(jnp.arange(past) // SEL_BLOCK, NS_PAD, dtype=BF16).T
    e_new = jax.nn.one_hot((past + jnp.arange(LANES)) // SEL_BLOCK, NS_PAD, dtype=BF16)
    rid = jnp.arange(nrow)
    gsum = ((rid[:, None] // (GROUP * dec) == rid[None, :] // (GROUP * dec))
            & (rid[:, None] % dec == rid[None, :] % dec)).astype(BF16)
    ksn, vsn, kwn, vwn = (a.reshape(ndb, dec, KV_DIM) for a in new4)
    o_t = _sattn(qp, g_s, kcmp, vcmp, ks_pages, vs_pages, win_k, win_v, layer,
                 ksn, vsn, kwn, vwn, _overlap_t(nchp), gsum, e_t, e_new, past, dec)
    o_t = o_t.reshape(ndb, N_KV, GROUP, dec, N_KV, HEAD_DIM)
    return jnp.einsum('bkgtcd,kc->btkgd', o_t, eye_kv).reshape(ndb * dec, ATT_DIM)


def kernel(x_prompt, x_sample, cache_cmp_k, cache_cmp_v, cache_slc_k, cache_slc_v, state_win_k, state_win_v,
           state_conv, page_table, w_in, cmp_pe, cmp_w1, cmp_w2, w_att_out, conv_w, w_conv_out, w_o,
           norm_mix, norm_ffn, w_ffn_in, w_ffn_out, norm_final):
    nb, seq, d = x_prompt.shape
    ndb, dec, _ = x_sample.shape
    depth, n_pool, page = cache_cmp_k.shape[:3]
    n_pages = page_table.shape[1]
    past = n_pages * page
    lw = state_win_k.shape[2]
    tp, ts = nb * seq, ndb * dec
    tm_p, tm_s = math.gcd(ROW_TILE, seq), math.gcd(ROW_TILE, ts)
    nchp = past // CMP_STRIDE
    keep = min(WINDOW, seq)
    assert seq % SEL_GROUP == 0 and seq >= WIN_SPAN and seq // SEL_BLOCK <= NS_PAD
    assert dec <= SUBLANES and N_KV * GROUP * dec <= LANES and (N_KV * GROUP * dec) % (2 * SUBLANES) == 0
    assert tm_s % SUBLANES == 0 and tm_s % dec == 0
    assert (past + dec - CMP_BLOCK) // CMP_STRIDE + 1 == nchp - 1
    assert -(-(past + dec) // SEL_BLOCK) <= NS_PAD and past >= WINDOW and lw == WINDOW

    rope_p = _rope_tables(jnp.arange(seq, dtype=jnp.int32))
    rope_s = _rope_tables(jnp.tile(past + jnp.arange(dec, dtype=jnp.int32), ndb))
    pool_k, pool_v, slc_k, slc_v, win_k, win_v = (
        _dim_major(a) for a in (cache_cmp_k, cache_cmp_v, cache_slc_k, cache_slc_v, state_win_k, state_win_v))
    no_side = jnp.zeros((tm_p, CONV_DIM), F32)
    zero = jnp.zeros((ndb, dec - 1, CONV_DIM), F32)

    cmp_w = [[_pack_cmp(cmp_pe[l, a], cmp_w1[l, a], cmp_w2[l, a]) for a in range(2)] for l in range(depth)]
    caches = [_sample_cache(*cmp_w[l], pool_k, pool_v, slc_k, slc_v, page_table, l) for l in range(depth)]

    xp, xs = x_prompt.reshape(tp, d), x_sample.reshape(ts, d)
    st_p = [[] for _ in range(7)]
    st_s = [[] for _ in range(7)]
    for l in range(depth):
        w_l = _pack_w_in(w_in[l])
        g_l = norm_mix[l][None, :]
        cmpk, cmpv = cmp_w[l]
        cw = jnp.pad(conv_w[l], ((0, SUBLANES - CONV_WIDTH), (0, 0)))
        w_mix = (w_att_out[l].astype(BF16), w_conv_out[l].astype(BF16), w_o[l].astype(BF16))
        w_ffn = (norm_ffn[l][None, :], w_ffn_in[l].astype(BF16), w_ffn_out[l].astype(BF16), norm_final[None, :])
        final = l == depth - 1

        q, kcvc, kskw, vt, stp, gates, u, cb, sga, sgb = _inproj(xp, g_l, w_l, rope_p, nb, tm_p)
        o_att = _prompt_attention(q, gates, kcvc, kskw, vt, cmpk, cmpv, nb, seq)
        xp = _mix(xp, o_att, u, cb, sga, sgb, no_side, no_side, cw, *w_mix, tm_p, seq)
        xp = _ffn(xp, *w_ffn, tm_p, final)
        for j in range(4):
            st_p[j].append(stp[j])
        st_p[4].append(stp[4][:, :, seq - keep:])
        st_p[5].append(stp[5][:, :, seq - keep:])
        st_p[6].append(jnp.stack([u[(b + 1) * seq - (CONV_WIDTH - 1):(b + 1) * seq] for b in range(nb)]))

        q, kcvc, kskw, vt, sts, gates, u, cb, sga, sgb = _inproj(xs, g_l, w_l, rope_s, 1, tm_s)
        new_t = sts[:, 0].reshape(6, KV_DIM, ndb, dec).transpose(0, 2, 1, 3)
        new_rows = lambda j: sts[j, 0].T
        o_att = _sample_attention(q, gates, [kskw[:, :KV_DIM], new_rows(3), kskw[:, KV_DIM:], new_rows(5)],
                                  caches[l], win_k, win_v, l, past, dec)
        cbuf = state_conv[l]
        side1 = jnp.concatenate([cbuf[:, 1:2], zero], axis=1).reshape(ts, CONV_DIM)
        side2 = jnp.concatenate([cbuf[:, 0:2], zero[:, 1:]], axis=1).reshape(ts, CONV_DIM)
        xs = _mix(xs, o_att, u, cb, sga, sgb, side1, side2, cw, *w_mix, tm_s, dec)
        xs = _ffn(xs, *w_ffn, tm_s, final)
        for j in range(6):
            st_s[j].append(new_t[j])
        st_s[6].append(jnp.concatenate([cbuf, u.reshape(ndb, dec, CONV_DIM)], axis=1)[:, -(CONV_WIDTH - 1):])

    kv_p = [_pos_major(jnp.stack(a)) for a in st_p[:6]]
    kv_s = [jnp.stack(a) for a in st_s[:6]]
    kv_s[4] = jnp.concatenate([win_k[:, :, :, dec:], kv_s[4]], axis=-1)
    kv_s[5] = jnp.concatenate([win_v[:, :, :, dec:], kv_s[5]], axis=-1)
    kv_s = [_pos_major(a) for a in kv_s]
    return (xp.reshape(nb, seq, d), xs.reshape(ndb, dec, d), *kv_p, jnp.stack(st_p[6]),
            *kv_s, jnp.stack(st_s[6]))
```

```python
import functools
import math

import jax
import jax.numpy as jnp
from jax import lax
from jax.experimental import pallas as pl
from jax.experimental.pallas import tpu as pltpu

N_HEADS = 8
HEAD_DIM = 64
N_KV = 2
GROUP = N_HEADS // N_KV
ATT_DIM = N_HEADS * HEAD_DIM
KV_DIM = N_KV * HEAD_DIM
ROPE_DIM = HEAD_DIM // 4
ROPE_HALF = ROPE_DIM // 2
ROPE_THETA = 500000.0
CMP_BLOCK = 32
CMP_STRIDE = 16
CMP_HIDDEN = 2 * HEAD_DIM
SEL_BLOCK = 64
SEL_TOPK = 16
N_LOCAL_BLOCKS = 2
WINDOW = 512
CONV_DIM = 512
CONV_WIDTH = 3
EPS = 1e-6
NEG = -1e30
FORCE_SCORE = 1e4

LANES = 128
SUBLANES = 8
Q_TILE = 128
SEL_CHUNK = 256
SEL_GROUP = 4 * SEL_CHUNK
LOG2E = 1.4426950408889634
VT_ROWS = HEAD_DIM + 16
WIN_SPAN = WINDOW + Q_TILE
ROW_TILE = 512
POOL_PAGES = 32
NS_PAD = LANES
VMEM_LIMIT = 56 * 1024 * 1024

C_Q = 0
C_KV = C_Q + ATT_DIM
C_GL = C_KV + 6 * KV_DIM
C_CU = C_GL + N_KV * LANES
C_CB = C_CU + CONV_DIM
C_CC = C_CB + CONV_DIM
F32 = jnp.float32
BF16 = jnp.bfloat16


def _nt(a, b):
    return lax.dot_general(a, b, (((1,), (1,)), ((), ())), preferred_element_type=F32)


def _nn(a, b):
    return jnp.dot(a, b, preferred_element_type=F32)


def _split3(x):
    hi = x.astype(BF16)
    r1 = x - hi.astype(F32)
    mid = r1.astype(BF16)
    lo = (r1 - mid.astype(F32)).astype(BF16)
    return hi, mid, lo


def _params(sem):
    return pltpu.CompilerParams(dimension_semantics=sem, vmem_limit_bytes=VMEM_LIMIT)


def _inproj_body(x_ref, g_ref, w_ref, rc_ref, rs1_ref, rs2_ref,
                 q_ref, kcvc_ref, kskw_ref, vt_ref, st_ref, gate_ref, u_ref, cb_ref, sga_ref, sgb_ref, *, d_model):
    x = x_ref[...]
    h = x * lax.rsqrt(jnp.mean(x * x, axis=-1, keepdims=True) + EPS)
    hb = (h * g_ref[...]).astype(BF16)
    rc, rs1, rs2 = rc_ref[...], rs1_ref[...], rs2_ref[...]

    def mm(lo, width):
        return _nt(hb, w_ref[lo:lo + width, :])

    def rope(z):
        return z * rc + pltpu.roll(z, LANES - ROPE_HALF, 1) * rs1 + pltpu.roll(z, ROPE_HALF, 1) * rs2

    for j in range(ATT_DIM // 256):
        z = mm(C_Q + 256 * j, 256)
        for t in range(2):
            q_ref[:, 256 * j + LANES * t:256 * j + LANES * (t + 1)] = (
                rope(z[:, LANES * t:LANES * (t + 1)]) * (HEAD_DIM ** -0.5))
    for j in range(3):
        z = mm(C_KV + 256 * j, 256)
        k, v = rope(z[:, :LANES]), z[:, LANES:]
        kt, vt = k.T, v.T
        st_ref[2 * j, 0] = kt
        st_ref[2 * j + 1, 0] = vt
        if j == 0:
            kcvc_ref[:, :LANES] = k
            kcvc_ref[:, LANES:] = v
        else:
            kskw_ref[:, LANES * (j - 1):LANES * j] = k.astype(BF16)
            ones_row = (lax.broadcasted_iota(jnp.int32, (VT_ROWS - HEAD_DIM, vt.shape[1]), 0) == 0).astype(BF16)
            for kvh in range(N_KV):
                vt_ref[j - 1, 0, VT_ROWS * kvh:VT_ROWS * kvh + HEAD_DIM] = vt[HEAD_DIM * kvh:HEAD_DIM * (kvh + 1)].astype(BF16)
                vt_ref[j - 1, 0, VT_ROWS * kvh + HEAD_DIM:VT_ROWS * (kvh + 1)] = ones_row
    gate_ref[...] = jax.nn.sigmoid(mm(C_GL, N_KV * LANES))
    for j in range(CONV_DIM // 256):
        u_ref[:, 256 * j:256 * (j + 1)] = mm(C_CC + 256 * j, 256) * mm(C_CU + 256 * j, 256)
        cb_ref[:, 256 * j:256 * (j + 1)] = mm(C_CB + 256 * j, 256)
    c_ga = C_CC + CONV_DIM
    c_gb = c_ga + d_model
    for j in range(d_model // 256):
        sga_ref[:, 256 * j:256 * (j + 1)] = jax.nn.sigmoid(mm(c_ga + 256 * j, 256))
        sgb_ref[:, 256 * j:256 * (j + 1)] = jax.nn.sigmoid(mm(c_gb + 256 * j, 256))


def _inproj(x, g, w, rope, nseq, tm):
    t, d = x.shape
    slen = t // nseq
    tps = slen // tm
    nrows = w.shape[0]
    row = lambda width: pl.BlockSpec((tm, width), lambda i: (i, 0))
    fixed = lambda shape: pl.BlockSpec(shape, lambda i: (0, 0))
    tab = pl.BlockSpec((tm, LANES), lambda i: (i % tps, 0))
    tr = lambda n, rows: pl.BlockSpec((n, 1, rows, tm), lambda i: (0, i // tps, 0, i % tps))
    sds = jax.ShapeDtypeStruct
    return pl.pallas_call(
        functools.partial(_inproj_body, d_model=d),
        grid=(t // tm,),
        in_specs=[row(d), fixed((1, d)), fixed((nrows, d)), tab, tab, tab],
        out_specs=[row(ATT_DIM), row(2 * KV_DIM), row(2 * KV_DIM), tr(2, N_KV * VT_ROWS), tr(6, KV_DIM), row(N_KV * LANES),
                   row(CONV_DIM), row(CONV_DIM), row(d), row(d)],
        out_shape=[sds((t, ATT_DIM), F32), sds((t, 2 * KV_DIM), F32), sds((t, 2 * KV_DIM), BF16),
                   sds((2, nseq, N_KV * VT_ROWS, slen), BF16), sds((6, nseq, KV_DIM, slen), F32), sds((t, N_KV * LANES), F32),
                   sds((t, CONV_DIM), F32), sds((t, CONV_DIM), F32), sds((t, d), F32), sds((t, d), F32)],
        compiler_params=_params(("arbitrary",)),
    )(x, g, w, *rope)


CHUNK = CMP_STRIDE * KV_DIM
HID2 = N_KV * CMP_HIDDEN


def _cmp_partials(src, pe_ref, w_ref, ab_ref):
    n = src.shape[0] // CMP_STRIDE
    acc_a = jnp.zeros((n, HID2), F32)
    acc_b = jnp.zeros((n, HID2), F32)
    for jp in range(CMP_STRIDE // 2):
        x = jnp.concatenate([src[pl.ds(2 * jp, n, stride=CMP_STRIDE), :],
                             src[pl.ds(2 * jp + 1, n, stride=CMP_STRIDE), :]], axis=1)
        cols = slice(2 * KV_DIM * jp, 2 * KV_DIM * (jp + 1))
        acc_a = acc_a + _nn((x + pe_ref[0:1, cols]).astype(BF16), w_ref[0, cols, :])
        acc_b = acc_b + _nn((x + pe_ref[1:2, cols]).astype(BF16), w_ref[1, cols, :])
    ab_ref[:, :HID2] = acc_a
    ab_ref[:, HID2:] = acc_b


def _cmp_ab_rows_body(x_ref, pe_ref, w_ref, ab_ref):
    _cmp_partials(x_ref, pe_ref, w_ref, ab_ref)


def _cmp_ab_rows(x, rows, col, pe, w, tr):
    return pl.pallas_call(
        _cmp_ab_rows_body,
        grid=(rows // tr,),
        in_specs=[pl.BlockSpec((tr, KV_DIM), lambda i: (i, col)),
                  pl.BlockSpec((2, CHUNK), lambda i: (0, 0)),
                  pl.BlockSpec((2, CHUNK, HID2), lambda i: (0, 0, 0))],
        out_specs=pl.BlockSpec((tr // CMP_STRIDE, 2 * HID2), lambda i: (i, 0)),
        out_shape=jax.ShapeDtypeStruct((rows // CMP_STRIDE, 2 * HID2), F32),
        compiler_params=_params(("arbitrary",)),
    )(x, pe, w)


def _cmp_ab_pages_body(x_ref, pe_ref, w_ref, ab_ref, rows_scr):
    page = x_ref.shape[3]
    for p in range(x_ref.shape[1]):
        rows_scr[page * p:page * (p + 1), :] = x_ref[0, p].T
    _cmp_partials(rows_scr, pe_ref, w_ref, ab_ref)


def _cmp_ab_pages(pool, layer, pe, w, pp):
    _, n_pool, _, page = pool.shape
    cpp = page // CMP_STRIDE
    return pl.pallas_call(
        _cmp_ab_pages_body,
        grid=(n_pool // pp,),
        in_specs=[pl.BlockSpec((1, pp, KV_DIM, page), lambda i: (layer, i, 0, 0)),
                  pl.BlockSpec((2, CHUNK), lambda i: (0, 0)),
                  pl.BlockSpec((2, CHUNK, HID2), lambda i: (0, 0, 0))],
        out_specs=pl.BlockSpec((pp * cpp, 2 * HID2), lambda i: (i, 0)),
        out_shape=jax.ShapeDtypeStruct((n_pool * cpp, 2 * HID2), F32),
        scratch_shapes=[pltpu.VMEM((pp * page, KV_DIM), F32)],
        compiler_params=_params(("arbitrary",)),
    )(pool, pe, w)


def _cmp_fin_body(abk_ref, abv_ref, w2k_ref, w2v_ref, ok_ref, ov_ref, *, nch):
    rows = abk_ref.shape[0]
    valid = lax.broadcasted_iota(jnp.int32, (rows, 1), 0) % nch < nch - 1

    def fin(ab_ref, w2_ref, o_ref):
        ab = ab_ref[...]
        hid = ab[:, :HID2] + pltpu.roll(ab[:, HID2:], rows - 1, 0)
        act = hid * jax.nn.sigmoid(hid)
        o_ref[...] = jnp.where(valid, _nn(act.astype(BF16), w2_ref[...]), 0.0)

    fin(abk_ref, w2k_ref, ok_ref)
    fin(abv_ref, w2v_ref, ov_ref)


def _cmp_fin(abk, abv, w2k, w2v, nch, tr):
    rows = abk.shape[0]
    ab_spec = pl.BlockSpec((tr, 2 * HID2), lambda i: (i, 0))
    w_spec = pl.BlockSpec((HID2, KV_DIM), lambda i: (0, 0))
    o_spec = pl.BlockSpec((tr, KV_DIM), lambda i: (i, 0))
    return pl.pallas_call(
        functools.partial(_cmp_fin_body, nch=nch),
        grid=(rows // tr,),
        in_specs=[ab_spec, ab_spec, w_spec, w_spec],
        out_specs=[o_spec, o_spec],
        out_shape=[jax.ShapeDtypeStruct((rows, KV_DIM), F32)] * 2,
        compiler_params=_params(("arbitrary",)),
    )(abk, abv, w2k, w2v)


def _select_bias(imp_t, qpos, ntop):
    shape = imp_t.shape
    s_io = lax.broadcasted_iota(jnp.int32, shape, 0)
    cur = qpos // SEL_BLOCK
    avail = s_io <= cur
    forced = (s_io == 0) | (avail & (s_io > cur - N_LOCAL_BLOCKS))
    val = jnp.where(avail, jnp.where(forced, FORCE_SCORE, imp_t), NEG)

    def pick_one(_, carry):
        val, sel = carry
        m = jnp.max(val, axis=0, keepdims=True)
        idx = jnp.min(jnp.where(val == m, s_io, NS_PAD), axis=0, keepdims=True)
        pick = s_io == idx
        return jnp.where(pick, -jnp.inf, val), jnp.where(pick, 1.0, sel)

    _, sel = lax.fori_loop(0, ntop, pick_one, (val, jnp.zeros(shape, F32)), unroll=True)
    return jnp.where((sel > 0.0) & avail, 0.0, NEG)


def _pattn_body(q_ref, g_ref, kc_ref, vct_ref, ks_ref, kw_ref, vst_ref, vwt_ref, ovt_ref, o_ref, bias_scr, *, seq):
    kv = pl.program_id(1)
    i = pl.program_id(2)
    nch = seq // CMP_STRIDE
    cols = GROUP * Q_TILE
    q = q_ref[...]
    qt = jnp.concatenate([q[:, :LANES].T, q[:, LANES:].T], axis=0)
    qt4 = jnp.concatenate([qt[HEAD_DIM * h:HEAD_DIM * (h + 1)] for h in range(GROUP)], axis=1)
    row_kv = lax.broadcasted_iota(jnp.int32, (KV_DIM, 1), 0) // HEAD_DIM
    qt_pad = jnp.where(row_kv == kv, jnp.concatenate([qt4] * N_KV, axis=0) * LOG2E, 0.0).astype(BF16)
    qpos1 = i * Q_TILE + lax.broadcasted_iota(jnp.int32, (1, Q_TILE), 1)
    qpos = jnp.concatenate([qpos1] * GROUP, axis=1)

    n_below = (i * Q_TILE) // SEL_GROUP
    w0 = pl.multiple_of(jnp.maximum(i * Q_TILE - WINDOW, 0), Q_TILE)
    s = _nn(kc_ref[0], qt_pad)
    s_win = _nn(kw_ref[pl.ds(w0, WIN_SPAN), :], qt_pad)
    s_diag = _nn(ks_ref[pl.ds(pl.multiple_of(n_below * SEL_GROUP, SEL_GROUP), SEL_GROUP), :], qt_pad)

    cmask = lax.broadcasted_iota(jnp.int32, (nch, 1), 0) * CMP_STRIDE + (CMP_BLOCK - 1) <= qpos
    s = jnp.where(cmask, s, NEG)
    p = jnp.where(cmask, jnp.exp2(s - jnp.max(s, axis=0, keepdims=True)), 0.0)
    l = jnp.sum(p, axis=0, keepdims=True)
    pc = p * jnp.where(l > 0.0, 1.0 / l, 0.0)
    o_c = _nn(vct_ref[0], pc.astype(BF16))

    pcs = pc[:, :Q_TILE]
    for h in range(1, GROUP):
        pcs = pcs + pc[:, h * Q_TILE:(h + 1) * Q_TILE]
    ovt = ovt_ref[...]
    imp_t = sum(_nn(ovt, part) for part in _split3(pcs))
    bias_scr[...] = _select_bias(imp_t, qpos1, min(SEL_TOPK, seq // SEL_BLOCK))

    kpos = w0 + lax.broadcasted_iota(jnp.int32, (WIN_SPAN, 1), 0)
    s = jnp.where((kpos <= qpos) & (kpos > qpos - WINDOW), s_win, NEG)
    p = jnp.exp2(s - jnp.max(s, axis=0, keepdims=True))
    acc_w = _nn(vwt_ref[0, 0, :, pl.ds(w0, WIN_SPAN)], p.astype(BF16))
    o_w = acc_w[:HEAD_DIM] / acc_w[HEAD_DIM:HEAD_DIM + 1]

    def group(gj, carry, s_all, causal):
        m, acc = carry
        g0 = pl.multiple_of(gj * SEL_GROUP, SEL_GROUP)
        if s_all is None:
            s_all = _nn(ks_ref[pl.ds(g0, SEL_GROUP), :], qt_pad)
        for c in range(SEL_GROUP // SEL_CHUNK):
            parts = []
            for r in range(SEL_CHUNK // SEL_BLOCK):
                row = SEL_CHUNK * c + SEL_BLOCK * r
                b_row = bias_scr[pl.ds(gj * (SEL_GROUP // SEL_BLOCK) + row // SEL_BLOCK, 1), :]
                parts.append(s_all[row:row + SEL_BLOCK] + jnp.concatenate([b_row] * GROUP, axis=1))
            s = jnp.concatenate(parts, axis=0)
            k0 = pl.multiple_of(g0 + SEL_CHUNK * c, SEL_CHUNK)
            if causal:
                kpos = k0 + lax.broadcasted_iota(jnp.int32, (SEL_CHUNK, 1), 0)
                s = jnp.where(kpos <= qpos, s, NEG)
            m_new = jnp.maximum(m, jnp.max(s, axis=0, keepdims=True))
            alpha = jnp.exp2(m - m_new)
            p = jnp.exp2(s - m_new)
            acc = alpha * acc + _nn(vst_ref[0, 0, :, pl.ds(k0, SEL_CHUNK)], p.astype(BF16))
            m = m_new
        return m, acc

    init = (jnp.full((1, cols), -3e38, F32), jnp.zeros((VT_ROWS, cols), F32))
    carry = group(n_below, init, s_diag, True)
    _, acc_s = lax.fori_loop(0, n_below, lambda gj, c: group(gj, c, None, False), carry)
    o_s = acc_s[:HEAD_DIM] / acc_s[HEAD_DIM:HEAD_DIM + 1]

    gt = g_ref[...].T
    outs = []
    for h in range(GROUP):
        c = slice(h * Q_TILE, (h + 1) * Q_TILE)
        outs.append(gt[3 * h:3 * h + 1] * o_c[:, c] + gt[3 * h + 1:3 * h + 2] * o_s[:, c]
                    + gt[3 * h + 2:3 * h + 3] * o_w[:, c])
    for t in range(GROUP // 2):
        o_ref[:, LANES * t:LANES * (t + 1)] = jnp.concatenate(outs[2 * t:2 * t + 2], axis=0).T


def _pattn(q, gates, kcmp, vcmp_t, kskw, vt, ovt, nb, seq):
    nq = seq // Q_TILE
    nch = seq // CMP_STRIDE
    qspec = pl.BlockSpec((Q_TILE, GROUP * HEAD_DIM), lambda b, k, i: (b * nq + i, k))
    gspec = pl.BlockSpec((Q_TILE, LANES), lambda b, k, i: (b * nq + i, k))
    kspec = lambda which: pl.BlockSpec((seq, KV_DIM), lambda b, k, i: (b, which))
    vspec = lambda which: pl.BlockSpec((1, 1, VT_ROWS, seq), lambda b, k, i: (which, b, k, 0))
    return pl.pallas_call(
        functools.partial(_pattn_body, seq=seq),
        grid=(nb, N_KV, nq),
        in_specs=[qspec, gspec,
                  pl.BlockSpec((1, nch, KV_DIM), lambda b, k, i: (b, 0, 0)),
                  pl.BlockSpec((1, HEAD_DIM, nch), lambda b, k, i: (b, k, 0)),
                  kspec(0), kspec(1), vspec(0), vspec(1),
                  pl.BlockSpec((NS_PAD, nch), lambda b, k, i: (0, 0))],
        out_specs=qspec,
        out_shape=jax.ShapeDtypeStruct((nb * seq, ATT_DIM), F32),
        scratch_shapes=[pltpu.VMEM((NS_PAD, Q_TILE), F32)],
        compiler_params=_params(("arbitrary", "arbitrary", "arbitrary")),
    )(q, gates, kcmp, vcmp_t, kskw, kskw, vt, vt, ovt)


def _sattn_body(qp_ref, g_ref, kc_ref, vc_ref, ks_ref, vs_ref, kw_ref, vw_ref,
                ksn_ref, vsn_ref, kwn_ref, vwn_ref, ovt_ref, gsum_ref, et_ref, en_ref, o_ref,
                *, past, dec, ntop):
    nchp = kc_ref.shape[1]
    n_pages = ks_ref.shape[1]
    lw = kw_ref.shape[3]
    qp = qp_ref[0]
    nrow = qp.shape[0]
    tok_r = lax.broadcasted_iota(jnp.int32, (nrow, 1), 0) % dec
    tok_l = lax.broadcasted_iota(jnp.int32, (1, LANES), 1) % dec
    new_j = lax.broadcasted_iota(jnp.int32, (1, LANES), 1)
    new_mask = (new_j < dec) & (new_j <= tok_r)

    def pad_rows(x):
        return jnp.concatenate([x.astype(F32), jnp.zeros((LANES - dec, KV_DIM), F32)], axis=0).astype(BF16)

    def softmax2(s_p, s_n):
        m = jnp.maximum(jnp.max(s_p, axis=-1, keepdims=True), jnp.max(s_n, axis=-1, keepdims=True))
        p_p, p_n = jnp.exp(s_p - m), jnp.exp(s_n - m)
        l = jnp.sum(p_p, axis=-1, keepdims=True) + jnp.sum(p_n, axis=-1, keepdims=True)
        return p_p.astype(BF16), p_n.astype(BF16), l

    s = _nn(qp, kc_ref[0].T.astype(BF16))
    sw_p = _nn(qp, kw_ref[0, 0].astype(BF16))
    sw_n = _nt(qp, pad_rows(kwn_ref[0]))
    kt = jnp.concatenate([ks_ref[0, pg] for pg in range(n_pages)], axis=1).astype(BF16)
    ss_p = _nn(qp, kt)
    ss_n = _nt(qp, pad_rows(ksn_ref[0]))

    cmask = lax.broadcasted_iota(jnp.int32, (1, nchp), 1) < nchp - 1
    s = jnp.where(cmask, s, NEG)
    p = jnp.where(cmask, jnp.exp(s - jnp.max(s, axis=-1, keepdims=True)), 0.0)
    pc = p / jnp.sum(p, axis=-1, keepdims=True)
    o_c = _nn(pc.astype(BF16), vc_ref[0].astype(BF16))

    gsum = gsum_ref[...]
    pcs = sum(_nn(gsum, part) for part in _split3(pc))
    pcs = jnp.concatenate([pcs, jnp.zeros((LANES - nrow, nchp), F32)], axis=0)
    ovt = ovt_ref[...]
    imp_t = sum(_nt(ovt, part) for part in _split3(pcs))
    bias = _select_bias(imp_t, past + tok_l, ntop).T[:nrow].astype(BF16)

    kpos = (past - lw) + lax.broadcasted_iota(jnp.int32, (1, lw), 1)
    p_p, p_n, l = softmax2(jnp.where(kpos > past + tok_r - WINDOW, sw_p, NEG), jnp.where(new_mask, sw_n, NEG))
    o_w = (_nt(p_p, vw_ref[0, 0].astype(BF16)) + _nn(p_n, pad_rows(vwn_ref[0]))) / l

    vt = jnp.concatenate([vs_ref[0, pg] for pg in range(n_pages)], axis=1).astype(BF16)
    s_p = ss_p + _nn(bias, et_ref[...])
    s_n = jnp.where(new_mask, ss_n + _nt(bias, en_ref[...]), NEG)
    p_p, p_n, l = softmax2(s_p, s_n)
    o_s = (_nt(p_p, vt) + _nn(p_n, pad_rows(vsn_ref[0]))) / l

    g = g_ref[0]
    o_ref[0] = g[:, 0:1] * o_c + g[:, 1:2] * o_s + g[:, 2:3] * o_w


def _sattn(qp, g, kc, vc, ks, vs, win_k, win_v, layer, ksn, vsn, kwn, vwn, ovt, gsum, e_t, e_new, past, dec):
    nseq, nrow = qp.shape[:2]
    nchp = kc.shape[1]
    n_pages, _, page = ks.shape[1:]
    lw = win_k.shape[3]
    ntop = min(SEL_TOPK, -(-(past + dec) // SEL_BLOCK))
    seq3 = lambda n, w: pl.BlockSpec((1, n, w), lambda b: (b, 0, 0))
    fixed = lambda r, c: pl.BlockSpec((r, c), lambda b: (0, 0))
    pages = pl.BlockSpec((1, n_pages, KV_DIM, page), lambda b: (b, 0, 0, 0))
    win = pl.BlockSpec((1, 1, KV_DIM, lw), lambda b: (layer, b, 0, 0))
    return pl.pallas_call(
        functools.partial(_sattn_body, past=past, dec=dec, ntop=ntop),
        grid=(nseq,),
        in_specs=[seq3(nrow, KV_DIM), seq3(nrow, SUBLANES), seq3(nchp, KV_DIM), seq3(nchp, KV_DIM),
                  pages, pages, win, win,
                  seq3(dec, KV_DIM), seq3(dec, KV_DIM), seq3(dec, KV_DIM), seq3(dec, KV_DIM),
                  fixed(NS_PAD, nchp), fixed(nrow, nrow), fixed(NS_PAD, past), fixed(LANES, NS_PAD)],
        out_specs=seq3(nrow, KV_DIM),
        out_shape=jax.ShapeDtypeStruct((nseq, nrow, KV_DIM), F32),
        compiler_params=_params(("arbitrary",)),
    )(qp, g, kc, vc, ks, vs, win_k, win_v, ksn, vsn, kwn, vwn, ovt, gsum, e_t, e_new)


def _mix_body(x_ref, o_ref, u_ref, uprev_ref, cb_ref, sga_ref, sgb_ref, s1_ref, s2_ref,
              cw_ref, wa_ref, wc_ref, wo_ref, y_ref, ubuf, *, tm, slen):
    i = pl.program_id(0)
    u = u_ref[...]
    ubuf[0:SUBLANES, :] = uprev_ref[...]
    ubuf[SUBLANES:, :] = u
    pos = (i * tm + lax.broadcasted_iota(jnp.int32, (tm, 1), 0)) % slen
    prev1 = jnp.where(pos >= 1, ubuf[SUBLANES - 1:SUBLANES - 1 + tm, :], s1_ref[...])
    prev2 = jnp.where(pos >= 2, ubuf[SUBLANES - 2:SUBLANES - 2 + tm, :], s2_ref[...])
    cw = cw_ref[...]
    y_conv = cw[0:1] * prev2 + cw[1:2] * prev1 + cw[2:3] * u
    a_br = _nn(o_ref[...].astype(BF16), wa_ref[...])
    c_br = _nn((cb_ref[...] * y_conv).astype(BF16), wc_ref[...])
    mix = sga_ref[...] * a_br + sgb_ref[...] * c_br
    y_ref[...] = x_ref[...] + _nn(mix.astype(BF16), wo_ref[...])


def _mix(x, o_att, u, cb, sga, sgb, side1, side2, conv_w, wa, wc, wo, tm, slen):
    t, d = x.shape
    row = lambda width: pl.BlockSpec((tm, width), lambda i: (i, 0))
    fixed = lambda shape: pl.BlockSpec(shape, lambda i: (0, 0))
    n_side = side1.shape[0] // tm
    side = pl.BlockSpec((tm, CONV_DIM), lambda i: (i % n_side, 0))
    prev = pl.BlockSpec((SUBLANES, CONV_DIM), lambda i: (jnp.maximum(i * (tm // SUBLANES) - 1, 0), 0))
    return pl.pallas_call(
        functools.partial(_mix_body, tm=tm, slen=slen),
        grid=(t // tm,),
        in_specs=[row(d), row(ATT_DIM), row(CONV_DIM), prev, row(CONV_DIM), row(d), row(d), side, side,
                  fixed((SUBLANES, CONV_DIM)), fixed((ATT_DIM, d)), fixed((CONV_DIM, d)), fixed((d, d))],
        out_specs=row(d),
        out_shape=jax.ShapeDtypeStruct((t, d), F32),
        scratch_shapes=[pltpu.VMEM((tm + SUBLANES, CONV_DIM), F32)],
        compiler_params=_params(("arbitrary",)),
    )(x, o_att, u, u, cb, sga, sgb, side1, side2, conv_w, wa, wc, wo)


def _ffn_body(x_ref, g_ref, win_ref, wout_ref, gf_ref, y_ref, *, d_ff, fc, final):
    x = x_ref[...]
    h = x * lax.rsqrt(jnp.mean(x * x, axis=-1, keepdims=True) + EPS)
    hb = (h * g_ref[...]).astype(BF16)
    acc = x
    for c in range(d_ff // fc):
        gate = _nn(hb, win_ref[:, c * fc:(c + 1) * fc])
        up = _nn(hb, win_ref[:, d_ff + c * fc:d_ff + (c + 1) * fc])
        act = (gate * jax.nn.sigmoid(gate) * up).astype(BF16)
        acc = acc + _nn(act, wout_ref[c * fc:(c + 1) * fc, :])
    if final:
        acc = acc * lax.rsqrt(jnp.mean(acc * acc, axis=-1, keepdims=True) + EPS) * gf_ref[...]
    y_ref[...] = acc


def _ffn(x, g, w_in, w_out, g_final, tm, final):
    t, d = x.shape
    d_ff = w_out.shape[0]
    fc = 256 if d_ff % 256 == 0 else LANES
    row = pl.BlockSpec((tm, d), lambda i: (i, 0))
    fixed = lambda shape: pl.BlockSpec(shape, lambda i: (0, 0))
    return pl.pallas_call(
        functools.partial(_ffn_body, d_ff=d_ff, fc=fc, final=final),
        grid=(t // tm,),
        in_specs=[row, fixed((1, d)), fixed((d, 2 * d_ff)), fixed((d_ff, d)), fixed((1, d))],
        out_specs=row,
        out_shape=jax.ShapeDtypeStruct((t, d), F32),
        compiler_params=_params(("arbitrary",)),
    )(x, g, w_in, w_out, g_final)


def _rope_tables(pos):
    inv = 1.0 / (ROPE_THETA ** (jnp.arange(ROPE_HALF, dtype=F32) * (2.0 / ROPE_DIM)))
    ang = pos.astype(F32)[:, None] * inv[None, :]
    cos, sin = jnp.cos(ang), jnp.sin(ang)
    t = pos.shape[0]
    pad = jnp.zeros((t, HEAD_DIM - ROPE_DIM), F32)
    rc = jnp.concatenate([cos, cos, pad + 1.0], axis=1)
    rs1 = jnp.concatenate([-sin, jnp.zeros_like(sin), pad], axis=1)
    rs2 = jnp.concatenate([jnp.zeros_like(sin), sin, pad], axis=1)
    rep = LANES // HEAD_DIM
    return tuple(jnp.tile(a, (1, rep)) for a in (rc, rs1, rs2))


def _pack_w_in(w):
    wt = w.T
    d = wt.shape[1]
    o_gl = ATT_DIM + 6 * KV_DIM
    gl = wt[o_gl:o_gl + 3 * N_HEADS].reshape(N_KV, 3 * GROUP, d)
    gl = jnp.pad(gl, ((0, 0), (0, LANES - 3 * GROUP), (0, 0))).reshape(N_KV * LANES, d)
    return jnp.concatenate([wt[:o_gl], gl, wt[o_gl + 3 * N_HEADS:]], axis=0).astype(BF16)


def _pack_cmp(pe, w1, w2):
    eye = jnp.eye(N_KV, dtype=F32)
    pe2 = jnp.broadcast_to(pe.reshape(2, CMP_STRIDE, 1, HEAD_DIM), (2, CMP_STRIDE, N_KV, HEAD_DIM)).reshape(2, CHUNK)
    w1r = w1.reshape(2, CMP_STRIDE, HEAD_DIM, CMP_HIDDEN)
    w1p = jnp.einsum('pjdh,kc->pjkdch', w1r, eye).reshape(2, CHUNK, HID2).astype(BF16)
    w2p = jnp.einsum('hd,kc->khcd', w2, eye).reshape(HID2, KV_DIM).astype(BF16)
    return pe2, w1p, w2p


def _overlap_t(nch):
    ci = jnp.arange(nch)[None, :] * CMP_STRIDE
    sj = jnp.arange(NS_PAD)[:, None] * SEL_BLOCK
    return ((ci <= sj + SEL_BLOCK - 1) & (ci + CMP_BLOCK - 1 >= sj)).astype(BF16)


def _dim_major(a):
    lead = a.shape[:-3]
    n = len(lead)
    return a.transpose(*range(n), n + 1, n + 2, n).reshape(*lead, KV_DIM, a.shape[-3])


def _pos_major(a):
    lead = a.shape[:-2]
    n = len(lead)
    return a.reshape(*lead, N_KV, HEAD_DIM, a.shape[-1]).transpose(*range(n), n + 2, n, n + 1)


def _prompt_attention(q, gates, kcvc, kskw, vt, cmpk, cmpv, nb, seq):
    nch = seq // CMP_STRIDE
    abk = _cmp_ab_rows(kcvc, nb * seq, 0, cmpk[0], cmpk[1], seq)
    abv = _cmp_ab_rows(kcvc, nb * seq, 1, cmpv[0], cmpv[1], seq)
    kcmp, vcmp = _cmp_fin(abk, abv, cmpk[2], cmpv[2], nch, nch)
    kcmp = kcmp.reshape(nb, nch, KV_DIM).astype(BF16)
    vcmp_t = vcmp.reshape(nb, nch, KV_DIM).transpose(0, 2, 1).astype(BF16)
    return _pattn(q, gates, kcmp, vcmp_t, kskw, vt, _overlap_t(nch), nb, seq)


def _sample_cache(cmpk, cmpv, pool_k, pool_v, slc_k, slc_v, page_table, layer):
    ndb, n_pages = page_table.shape
    n_pool, _, page = slc_k.shape[1:]
    cpp = page // CMP_STRIDE
    nchp = n_pages * cpp
    pp = math.gcd(n_pool, POOL_PAGES)
    abk = _cmp_ab_pages(pool_k, layer, cmpk[0], cmpk[1], pp).reshape(n_pool, cpp, 2 * HID2)[page_table]
    abv = _cmp_ab_pages(pool_v, layer, cmpv[0], cmpv[1], pp).reshape(n_pool, cpp, 2 * HID2)[page_table]
    kcmp, vcmp = _cmp_fin(abk.reshape(ndb * nchp, 2 * HID2), abv.reshape(ndb * nchp, 2 * HID2), cmpk[2], cmpv[2],
                          nchp, math.gcd(ndb, SUBLANES) * nchp)
    pages = layer * n_pool + page_table
    gather = lambda pool: pool.reshape(-1, KV_DIM, page)[pages]
    return kcmp.reshape(ndb, nchp, KV_DIM), vcmp.reshape(ndb, nchp, KV_DIM), gather(slc_k), gather(slc_v)


def _sample_attention(q, gates, new4, cache, win_k, win_v, layer, past, dec):
    kcmp, vcmp, ks_pages, vs_pages = cache
    ndb, nchp = kcmp.shape[:2]
    nrow = N_KV * GROUP * dec
    eye_kv = jnp.eye(N_KV, dtype=F32)
    q_s = q.reshape(ndb, dec, N_KV, GROUP, HEAD_DIM).transpose(0, 2, 3, 1, 4)
    qp = (q_s[:, :, :, :, None] * eye_kv[None, :, None, None, :, None]).reshape(ndb, nrow, KV_DIM).astype(BF16)
    g_s = gates.reshape(ndb, dec, N_KV, LANES)[..., :3 * GROUP].reshape(ndb, dec, N_KV, GROUP, 3)
    g_s = jnp.pad(g_s.transpose(0, 2, 3, 1, 4).reshape(ndb, nrow, 3), ((0, 0), (0, 0), (0, SUBLANES - 3)))
    e_t = jax.nn.one_hot(jnp.arange(past) // SEL_BLOCK, NS_PAD, dtype=BF16).T
    e_new = jax.nn.one_hot((past + jnp.arange(LANES)) // SEL_BLOCK, NS_PAD, dtype=BF16)
    rid = jnp.arange(nrow)
    gsum = ((rid[:, None] // (GROUP * dec) == rid[None, :] // (GROUP * dec))
            & (rid[:, None] % dec == rid[None, :] % dec)).astype(BF16)
    ksn, vsn, kwn, vwn = (a.reshape(ndb, dec, KV_DIM) for a in new4)
    o_t = _sattn(qp, g_s, kcmp, vcmp, ks_pages, vs_pages, win_k, win_v, layer,
                 ksn, vsn, kwn, vwn, _overlap_t(nchp), gsum, e_t, e_new, past, dec)
    o_t = o_t.reshape(ndb, N_KV, GROUP, dec, N_KV, HEAD_DIM)
    return jnp.einsum('bkgtcd,kc->btkgd', o_t, eye_kv).reshape(ndb * dec, ATT_DIM)


def kernel(x_prompt, x_sample, cache_cmp_k, cache_cmp_v, cache_slc_k, cache_slc_v, state_win_k, state_win_v,
           state_conv, page_table, w_in, cmp_pe, cmp_w1, cmp_w2, w_att_out, conv_w, w_conv_out, w_o,
           norm_mix, norm_ffn, w_ffn_in, w_ffn_out, norm_final):
    nb, seq, d = x_prompt.shape
    ndb, dec, _ = x_sample.shape
    depth, n_pool, page = cache_cmp_k.shape[:3]
    n_pages = page_table.shape[1]
    past = n_pages * page
    lw = state_win_k.shape[2]
    tp, ts = nb * seq, ndb * dec
    tm_p, tm_s = math.gcd(ROW_TILE, seq), math.gcd(ROW_TILE, ts)
    nchp = past // CMP_STRIDE
    keep = min(WINDOW, seq)
    assert seq % SEL_GROUP == 0 and seq >= WIN_SPAN and seq // SEL_BLOCK <= NS_PAD
    assert dec <= SUBLANES and N_KV * GROUP * dec <= LANES and (N_KV * GROUP * dec) % (2 * SUBLANES) == 0
    assert tm_s % SUBLANES == 0 and tm_s % dec == 0
    assert (past + dec - CMP_BLOCK) // CMP_STRIDE + 1 == nchp - 1
    assert -(-(past + dec) // SEL_BLOCK) <= NS_PAD and past >= WINDOW and lw == WINDOW

    rope_p = _rope_tables(jnp.arange(seq, dtype=jnp.int32))
    rope_s = _rope_tables(jnp.tile(past + jnp.arange(dec, dtype=jnp.int32), ndb))
    pool_k, pool_v, slc_k, slc_v, win_k, win_v = (
        _dim_major(a) for a in (cache_cmp_k, cache_cmp_v, cache_slc_k, cache_slc_v, state_win_k, state_win_v))
    no_side = jnp.zeros((tm_p, CONV_DIM), F32)
    zero = jnp.zeros((ndb, dec - 1, CONV_DIM), F32)

    cmp_w = [[_pack_cmp(cmp_pe[l, a], cmp_w1[l, a], cmp_w2[l, a]) for a in range(2)] for l in range(depth)]
    caches = [_sample_cache(*cmp_w[l], pool_k, pool_v, slc_k, slc_v, page_table, l) for l in range(depth)]

    xp, xs = x_prompt.reshape(tp, d), x_sample.reshape(ts, d)
    st_p = [[] for _ in range(7)]
    st_s = [[] for _ in range(7)]
    for l in range(depth):
        w_l = _pack_w_in(w_in[l])
        g_l = norm_mix[l][None, :]
        cmpk, cmpv = cmp_w[l]
        cw = jnp.pad(conv_w[l], ((0, SUBLANES - CONV_WIDTH), (0, 0)))
        w_mix = (w_att_out[l].astype(BF16), w_conv_out[l].astype(BF16), w_o[l].astype(BF16))
        w_ffn = (norm_ffn[l][None, :], w_ffn_in[l].astype(BF16), w_ffn_out[l].astype(BF16), norm_final[None, :])
        final = l == depth - 1

        q, kcvc, kskw, vt, stp, gates, u, cb, sga, sgb = _inproj(xp, g_l, w_l, rope_p, nb, tm_p)
        o_att = _prompt_attention(q, gates, kcvc, kskw, vt, cmpk, cmpv, nb, seq)
        xp = _mix(xp, o_att, u, cb, sga, sgb, no_side, no_side, cw, *w_mix, tm_p, seq)
        xp = _ffn(xp, *w_ffn, tm_p, final)
        for j in range(4):
            st_p[j].append(stp[j])
        st_p[4].append(stp[4][:, :, seq - keep:])
        st_p[5].append(stp[5][:, :, seq - keep:])
        st_p[6].append(jnp.stack([u[(b + 1) * seq - (CONV_WIDTH - 1):(b + 1) * seq] for b in range(nb)]))

        q, kcvc, kskw, vt, sts, gates, u, cb, sga, sgb = _inproj(xs, g_l, w_l, rope_s, 1, tm_s)
        new_t = sts[:, 0].reshape(6, KV_DIM, ndb, dec).transpose(0, 2, 1, 3)
        new_rows = lambda j: sts[j, 0].T
        o_att = _sample_attention(q, gates, [kskw[:, :KV_DIM], new_rows(3), kskw[:, KV_DIM:], new_rows(5)],
                                  caches[l], win_k, win_v, l, past, dec)
        cbuf = state_conv[l]
        side1 = jnp.concatenate([cbuf[:, 1:2], zero], axis=1).reshape(ts, CONV_DIM)
        side2 = jnp.concatenate([cbuf[:, 0:2], zero[:, 1:]], axis=1).reshape(ts, CONV_DIM)
        xs = _mix(xs, o_att, u, cb, sga, sgb, side1, side2, cw, *w_mix, tm_s, dec)
        xs = _ffn(xs, *w_ffn, tm_s, final)
        for j in range(6):
            st_s[j].append(new_t[j])
        st_s[6].append(jnp.concatenate([cbuf, u.reshape(ndb, dec, CONV_DIM)], axis=1)[:, -(CONV_WIDTH - 1):])

    kv_p = [_pos_major(jnp.stack(a)) for a in st_p[:6]]
    kv_s = [jnp.stack(a) for a in st_s[:6]]
    kv_s[4] = jnp.concatenate([win_k[:, :, :, dec:], kv_s[4]], axis=-1)
    kv_s[5] = jnp.concatenate([win_v[:, :, :, dec:], kv_s[5]], axis=-1)
    kv_s = [_pos_major(a) for a in kv_s]
    return (xp.reshape(nb, seq, d), xs.reshape(ndb, dec, d), *kv_p, jnp.stack(st_p[6]),
            *kv_s, jnp.stack(st_s[6]))
```

```python
import functools
import math

import jax
import jax.numpy as jnp
from jax import lax
from jax.experimental import pallas as pl
from jax.experimental.pallas import tpu as pltpu

N_HEADS = 8
HEAD_DIM = 64
N_KV = 2
GROUP = N_HEADS // N_KV
ATT_DIM = N_HEADS * HEAD_DIM
KV_DIM = N_KV * HEAD_DIM
ROPE_DIM = HEAD_DIM // 4
ROPE_HALF = ROPE_DIM // 2
ROPE_THETA = 500000.0
CMP_BLOCK = 32
CMP_STRIDE = 16
CMP_HIDDEN = 2 * HEAD_DIM
SEL_BLOCK = 64
SEL_TOPK = 16
N_LOCAL_BLOCKS = 2
WINDOW = 512
CONV_DIM = 512
CONV_WIDTH = 3
EPS = 1e-6
NEG = -1e30
FORCE_SCORE = 1e4

LANES = 128
SUBLANES = 8
Q_TILE = 128
SEL_CHUNK = 256
SEL_GROUP = 4 * SEL_CHUNK
LOG2E = 1.4426950408889634
VT_ROWS = HEAD_DIM + 16
WIN_SPAN = WINDOW + Q_TILE
ROW_TILE = 512
POOL_PAGES = 32
NS_PAD = LANES
VMEM_LIMIT = 56 * 1024 * 1024

C_Q = 0
C_KV = C_Q + ATT_DIM
C_GL = C_KV + 6 * KV_DIM
C_CU = C_GL + N_KV * LANES
C_CB = C_CU + CONV_DIM
C_CC = C_CB + CONV_DIM
F32 = jnp.float32
BF16 = jnp.bfloat16


def _nt(a, b):
    return lax.dot_general(a, b, (((1,), (1,)), ((), ())), preferred_element_type=F32)


def _nn(a, b):
    return jnp.dot(a, b, preferred_element_type=F32)


def _split3(x):
    hi = x.astype(BF16)
    r1 = x - hi.astype(F32)
    mid = r1.astype(BF16)
    lo = (r1 - mid.astype(F32)).astype(BF16)
    return hi, mid, lo


def _params(sem):
    return pltpu.CompilerParams(dimension_semantics=sem, vmem_limit_bytes=VMEM_LIMIT)


def _inproj_body(x_ref, g_ref, w_ref, rc_ref, rs1_ref, rs2_ref,
                 q_ref, kcvc_ref, kskw_ref, vt_ref, st_ref, gate_ref, u_ref, cb_ref, sga_ref, sgb_ref, *, d_model):
    x = x_ref[...]
    h = x * lax.rsqrt(jnp.mean(x * x, axis=-1, keepdims=True) + EPS)
    hb = (h * g_ref[...]).astype(BF16)
    rc, rs1, rs2 = rc_ref[...], rs1_ref[...], rs2_ref[...]

    def mm(lo, width):
        return _nt(hb, w_ref[lo:lo + width, :])

    def rope(z):
        return z * rc + pltpu.roll(z, LANES - ROPE_HALF, 1) * rs1 + pltpu.roll(z, ROPE_HALF, 1) * rs2

    for j in range(ATT_DIM // 256):
        z = mm(C_Q + 256 * j, 256)
        for t in range(2):
            q_ref[:, 256 * j + LANES * t:256 * j + LANES * (t + 1)] = (
                rope(z[:, LANES * t:LANES * (t + 1)]) * (HEAD_DIM ** -0.5))
    for j in range(3):
        z = mm(C_KV + 256 * j, 256)
        k, v = rope(z[:, :LANES]), z[:, LANES:]
        kt, vt = k.T, v.T
        st_ref[2 * j, 0] = kt
        st_ref[2 * j + 1, 0] = vt
        if j == 0:
            kcvc_ref[:, :LANES] = k
            kcvc_ref[:, LANES:] = v
        else:
            kskw_ref[:, LANES * (j - 1):LANES * j] = k.astype(BF16)
            ones_row = (lax.broadcasted_iota(jnp.int32, (VT_ROWS - HEAD_DIM, vt.shape[1]), 0) == 0).astype(BF16)
            for kvh in range(N_KV):
                vt_ref[j - 1, 0, VT_ROWS * kvh:VT_ROWS * kvh + HEAD_DIM] = vt[HEAD_DIM * kvh:HEAD_DIM * (kvh + 1)].astype(BF16)
                vt_ref[j - 1, 0, VT_ROWS * kvh + HEAD_DIM:VT_ROWS * (kvh + 1)] = ones_row
    gate_ref[...] = jax.nn.sigmoid(mm(C_GL, N_KV * LANES))
    for j in range(CONV_DIM // 256):
        u_ref[:, 256 * j:256 * (j + 1)] = mm(C_CC + 256 * j, 256) * mm(C_CU + 256 * j, 256)
        cb_ref[:, 256 * j:256 * (j + 1)] = mm(C_CB + 256 * j, 256)
    c_ga = C_CC + CONV_DIM
    c_gb = c_ga + d_model
    for j in range(d_model // 256):
        sga_ref[:, 256 * j:256 * (j + 1)] = jax.nn.sigmoid(mm(c_ga + 256 * j, 256))
        sgb_ref[:, 256 * j:256 * (j + 1)] = jax.nn.sigmoid(mm(c_gb + 256 * j, 256))


def _inproj(x, g, w, rope, nseq, tm):
    t, d = x.shape
    slen = t // nseq
    tps = slen // tm
    nrows = w.shape[0]
    row = lambda width: pl.BlockSpec((tm, width), lambda i: (i, 0))
    fixed = lambda shape: pl.BlockSpec(shape, lambda i: (0, 0))
    tab = pl.BlockSpec((tm, LANES), lambda i: (i % tps, 0))
    tr = lambda n, rows: pl.BlockSpec((n, 1, rows, tm), lambda i: (0, i // tps, 0, i % tps))
    sds = jax.ShapeDtypeStruct
    return pl.pallas_call(
        functools.partial(_inproj_body, d_model=d),
        grid=(t // tm,),
        in_specs=[row(d), fixed((1, d)), fixed((nrows, d)), tab, tab, tab],
        out_specs=[row(ATT_DIM), row(2 * KV_DIM), row(2 * KV_DIM), tr(2, N_KV * VT_ROWS), tr(6, KV_DIM), row(N_KV * LANES),
                   row(CONV_DIM), row(CONV_DIM), row(d), row(d)],
        out_shape=[sds((t, ATT_DIM), F32), sds((t, 2 * KV_DIM), F32), sds((t, 2 * KV_DIM), BF16),
                   sds((2, nseq, N_KV * VT_ROWS, slen), BF16), sds((6, nseq, KV_DIM, slen), F32), sds((t, N_KV * LANES), F32),
                   sds((t, CONV_DIM), F32), sds((t, CONV_DIM), F32), sds((t, d), F32), sds((t, d), F32)],
        compiler_params=_params(("arbitrary",)),
    )(x, g, w, *rope)


CHUNK = CMP_STRIDE * KV_DIM
HID2 = N_KV * CMP_HIDDEN


def _cmp_partials(src, pe_ref, w_ref, ab_ref):
    n = src.shape[0] // CMP_STRIDE
    acc_a = jnp.zeros((n, HID2), F32)
    acc_b = jnp.zeros((n, HID2), F32)
    for jp in range(CMP_STRIDE // 2):
        x = jnp.concatenate([src[pl.ds(2 * jp, n, stride=CMP_STRIDE), :],
                             src[pl.ds(2 * jp + 1, n, stride=CMP_STRIDE), :]], axis=1)
        cols = slice(2 * KV_DIM * jp, 2 * KV_DIM * (jp + 1))
        acc_a = acc_a + _nn((x + pe_ref[0:1, cols]).astype(BF16), w_ref[0, cols, :])
        acc_b = acc_b + _nn((x + pe_ref[1:2, cols]).astype(BF16), w_ref[1, cols, :])
    ab_ref[:, :HID2] = acc_a
    ab_ref[:, HID2:] = acc_b


def _cmp_ab_rows_body(x_ref, pe_ref, w_ref, ab_ref):
    _cmp_partials(x_ref, pe_ref, w_ref, ab_ref)


def _cmp_ab_rows(x, rows, col, pe, w, tr):
    return pl.pallas_call(
        _cmp_ab_rows_body,
        grid=(rows // tr,),
        in_specs=[pl.BlockSpec((tr, KV_DIM), lambda i: (i, col)),
                  pl.BlockSpec((2, CHUNK), lambda i: (0, 0)),
                  pl.BlockSpec((2, CHUNK, HID2), lambda i: (0, 0, 0))],
        out_specs=pl.BlockSpec((tr // CMP_STRIDE, 2 * HID2), lambda i: (i, 0)),
        out_shape=jax.ShapeDtypeStruct((rows // CMP_STRIDE, 2 * HID2), F32),
        compiler_params=_params(("arbitrary",)),
    )(x, pe, w)


def _cmp_ab_pages_body(x_ref, pe_ref, w_ref, ab_ref, rows_scr):
    page = x_ref.shape[3]
    for p in range(x_ref.shape[1]):
        rows_scr[page * p:page * (p + 1), :] = x_ref[0, p].T
    _cmp_partials(rows_scr, pe_ref, w_ref, ab_ref)


def _cmp_ab_pages(pool, layer, pe, w, pp):
    _, n_pool, _, page = pool.shape
    cpp = page // CMP_STRIDE
    return pl.pallas_call(
        _cmp_ab_pages_body,
        grid=(n_pool // pp,),
        in_specs=[pl.BlockSpec((1, pp, KV_DIM, page), lambda i: (layer, i, 0, 0)),
                  pl.BlockSpec((2, CHUNK), lambda i: (0, 0)),
                  pl.BlockSpec((2, CHUNK, HID2), lambda i: (0, 0, 0))],
        out_specs=pl.BlockSpec((pp * cpp, 2 * HID2), lambda i: (i, 0)),
        out_shape=jax.ShapeDtypeStruct((n_pool * cpp, 2 * HID2), F32),
        scratch_shapes=[pltpu.VMEM((pp * page, KV_DIM), F32)],
        compiler_params=_params(("arbitrary",)),
    )(pool, pe, w)


def _cmp_fin_body(abk_ref, abv_ref, w2k_ref, w2v_ref, ok_ref, ov_ref, *, nch):
    rows = abk_ref.shape[0]
    valid = lax.broadcasted_iota(jnp.int32, (rows, 1), 0) % nch < nch - 1

    def fin(ab_ref, w2_ref, o_ref):
        ab = ab_ref[...]
        hid = ab[:, :HID2] + pltpu.roll(ab[:, HID2:], rows - 1, 0)
        act = hid * jax.nn.sigmoid(hid)
        o_ref[...] = jnp.where(valid, _nn(act.astype(BF16), w2_ref[...]), 0.0)

    fin(abk_ref, w2k_ref, ok_ref)
    fin(abv_ref, w2v_ref, ov_ref)


def _cmp_fin(abk, abv, w2k, w2v, nch, tr):
    rows = abk.shape[0]
    ab_spec = pl.BlockSpec((tr, 2 * HID2), lambda i: (i, 0))
    w_spec = pl.BlockSpec((HID2, KV_DIM), lambda i: (0, 0))
    o_spec = pl.BlockSpec((tr, KV_DIM), lambda i: (i, 0))
    return pl.pallas_call(
        functools.partial(_cmp_fin_body, nch=nch),
        grid=(rows // tr,),
        in_specs=[ab_spec, ab_spec, w_spec, w_spec],
        out_specs=[o_spec, o_spec],
        out_shape=[jax.ShapeDtypeStruct((rows, KV_DIM), F32)] * 2,
        compiler_params=_params(("arbitrary",)),
    )(abk, abv, w2k, w2v)


def _select_bias(imp_t, qpos, ntop):
    shape = imp_t.shape
    s_io = lax.broadcasted_iota(jnp.int32, shape, 0)
    cur = qpos // SEL_BLOCK
    avail = s_io <= cur
    forced = (s_io == 0) | (avail & (s_io > cur - N_LOCAL_BLOCKS))
    val = jnp.where(avail, jnp.where(forced, FORCE_SCORE, imp_t), NEG)

    def pick_one(_, carry):
        val, sel = carry
        m = jnp.max(val, axis=0, keepdims=True)
        idx = jnp.min(jnp.where(val == m, s_io, NS_PAD), axis=0, keepdims=True)
        pick = s_io == idx
        return jnp.where(pick, -jnp.inf, val), jnp.where(pick, 1.0, sel)

    _, sel = lax.fori_loop(0, ntop, pick_one, (val, jnp.zeros(shape, F32)), unroll=True)
    return jnp.where((sel > 0.0) & avail, 0.0, NEG)


def _pattn_body(q_ref, g_ref, kc_ref, vct_ref, ks_ref, kw_ref, vst_ref, vwt_ref, ovt_ref, o_ref, bias_scr, *, seq):
    kv = pl.program_id(1)
    i = pl.program_id(2)
    nch = seq // CMP_STRIDE
    cols = GROUP * Q_TILE
    q = q_ref[...]
    qt = jnp.concatenate([q[:, :LANES].T, q[:, LANES:].T], axis=0)
    qt4 = jnp.concatenate([qt[HEAD_DIM * h:HEAD_DIM * (h + 1)] for h in range(GROUP)], axis=1)
    row_kv = lax.broadcasted_iota(jnp.int32, (KV_DIM, 1), 0) // HEAD_DIM
    qt_pad = jnp.where(row_kv == kv, jnp.concatenate([qt4] * N_KV, axis=0) * LOG2E, 0.0).astype(BF16)
    qpos1 = i * Q_TILE + lax.broadcasted_iota(jnp.int32, (1, Q_TILE), 1)
    qpos = jnp.concatenate([qpos1] * GROUP, axis=1)

    n_below = (i * Q_TILE) // SEL_GROUP
    w0 = pl.multiple_of(jnp.maximum(i * Q_TILE - WINDOW, 0), Q_TILE)
    s = _nn(kc_ref[0], qt_pad)
    s_win = _nn(kw_ref[pl.ds(w0, WIN_SPAN), :], qt_pad)
    s_diag = _nn(ks_ref[pl.ds(pl.multiple_of(n_below * SEL_GROUP, SEL_GROUP), SEL_GROUP), :], qt_pad)

    cmask = lax.broadcasted_iota(jnp.int32, (nch, 1), 0) * CMP_STRIDE + (CMP_BLOCK - 1) <= qpos
    s = jnp.where(cmask, s, NEG)
    p = jnp.where(cmask, jnp.exp2(s - jnp.max(s, axis=0, keepdims=True)), 0.0)
    l = jnp.sum(p, axis=0, keepdims=True)
    pc = p * jnp.where(l > 0.0, 1.0 / l, 0.0)
    o_c = _nn(vct_ref[0], pc.astype(BF16))

    pcs = pc[:, :Q_TILE]
    for h in range(1, GROUP):
        pcs = pcs + pc[:, h * Q_TILE:(h + 1) * Q_TILE]
    ovt = ovt_ref[...]
    imp_t = sum(_nn(ovt, part) for part in _split3(pcs))
    bias_scr[...] = _select_bias(imp_t, qpos1, min(SEL_TOPK, seq // SEL_BLOCK))

    kpos = w0 + lax.broadcasted_iota(jnp.int32, (WIN_SPAN, 1), 0)
    s = jnp.where((kpos <= qpos) & (kpos > qpos - WINDOW), s_win, NEG)
    p = jnp.exp2(s - jnp.max(s, axis=0, keepdims=True))
    acc_w = _nn(vwt_ref[0, 0, :, pl.ds(w0, WIN_SPAN)], p.astype(BF16))
    o_w = acc_w[:HEAD_DIM] / acc_w[HEAD_DIM:HEAD_DIM + 1]

    def group(gj, carry, s_all, causal):
        m, acc = carry
        g0 = pl.multiple_of(gj * SEL_GROUP, SEL_GROUP)
        if s_all is None:
            s_all = _nn(ks_ref[pl.ds(g0, SEL_GROUP), :], qt_pad)
        for c in range(SEL_GROUP // SEL_CHUNK):
            parts = []
            for r in range(SEL_CHUNK // SEL_BLOCK):
                row = SEL_CHUNK * c + SEL_BLOCK * r
                b_row = bias_scr[pl.ds(gj * (SEL_GROUP // SEL_BLOCK) + row // SEL_BLOCK, 1), :]
                parts.append(s_all[row:row + SEL_BLOCK] + jnp.concatenate([b_row] * GROUP, axis=1))
            s = jnp.concatenate(parts, axis=0)
            k0 = pl.multiple_of(g0 + SEL_CHUNK * c, SEL_CHUNK)
            if causal:
                kpos = k0 + lax.broadcasted_iota(jnp.int32, (SEL_CHUNK, 1), 0)
                s = jnp.where(kpos <= qpos, s, NEG)
            m_new = jnp.maximum(m, jnp.max(s, axis=0, keepdims=True))
            alpha = jnp.exp2(m - m_new)
            p = jnp.exp2(s - m_new)
            acc = alpha * acc + _nn(vst_ref[0, 0, :, pl.ds(k0, SEL_CHUNK)], p.astype(BF16))
            m = m_new
        return m, acc

    init = (jnp.full((1, cols), -3e38, F32), jnp.zeros((VT_ROWS, cols), F32))
    carry = group(n_below, init, s_diag, True)
    _, acc_s = lax.fori_loop(0, n_below, lambda gj, c: group(gj, c, None, False), carry)
    o_s = acc_s[:HEAD_DIM] / acc_s[HEAD_DIM:HEAD_DIM + 1]

    gt = g_ref[...].T
    outs = []
    for h in range(GROUP):
        c = slice(h * Q_TILE, (h + 1) * Q_TILE)
        outs.append(gt[3 * h:3 * h + 1] * o_c[:, c] + gt[3 * h + 1:3 * h + 2] * o_s[:, c]
                    + gt[3 * h + 2:3 * h + 3] * o_w[:, c])
    for t in range(GROUP // 2):
        o_ref[:, LANES * t:LANES * (t + 1)] = jnp.concatenate(outs[2 * t:2 * t + 2], axis=0).T


def _pattn(q, gates, kcmp, vcmp_t, kskw, vt, ovt, nb, seq):
    nq = seq // Q_TILE
    nch = seq // CMP_STRIDE
    qspec = pl.BlockSpec((Q_TILE, GROUP * HEAD_DIM), lambda b, k, i: (b * nq + i, k))
    gspec = pl.BlockSpec((Q_TILE, LANES), lambda b, k, i: (b * nq + i, k))
    kspec = lambda which: pl.BlockSpec((seq, KV_DIM), lambda b, k, i: (b, which))
    vspec = lambda which: pl.BlockSpec((1, 1, VT_ROWS, seq), lambda b, k, i: (which, b, k, 0))
    return pl.pallas_call(
        functools.partial(_pattn_body, seq=seq),
        grid=(nb, N_KV, nq),
        in_specs=[qspec, gspec,
                  pl.BlockSpec((1, nch, KV_DIM), lambda b, k, i: (b, 0, 0)),
                  pl.BlockSpec((1, HEAD_DIM, nch), lambda b, k, i: (b, k, 0)),
                  kspec(0), kspec(1), vspec(0), vspec(1),
                  pl.BlockSpec((NS_PAD, nch), lambda b, k, i: (0, 0))],
        out_specs=qspec,
        out_shape=jax.ShapeDtypeStruct((nb * seq, ATT_DIM), F32),
        scratch_shapes=[pltpu.VMEM((NS_PAD, Q_TILE), F32)],
        compiler_params=_params(("arbitrary", "arbitrary", "arbitrary")),
    )(q, gates, kcmp, vcmp_t, kskw, kskw, vt, vt, ovt)


def _sattn_body(pages_ref, qp_ref, g_ref, kc_ref, vc_ref, ks_hbm, vs_hbm, kw_ref, vw_ref,
                ksn_ref, vsn_ref, kwn_ref, vwn_ref, ovt_ref, gsum_ref, et_ref, en_ref, o_ref,
                kbuf, vbuf, sems, *, past, dec, ntop):
    b = pl.program_id(0)
    nchp = kc_ref.shape[1]
    n_pages = kbuf.shape[1]
    lw = kw_ref.shape[3]

    def page_copies(seq_idx, slot):
        return [pltpu.make_async_copy(src.at[pages_ref[seq_idx, pg]], buf.at[slot, pg], sems.at[a, slot])
                for a, (src, buf) in enumerate(((ks_hbm, kbuf), (vs_hbm, vbuf))) for pg in range(n_pages)]

    slot = b % 2

    @pl.when(b == 0)
    def _():
        for cp in page_copies(0, 0):
            cp.start()

    @pl.when(b + 1 < pl.num_programs(0))
    def _():
        for cp in page_copies(b + 1, 1 - slot):
            cp.start()

    qp = qp_ref[0]
    nrow = qp.shape[0]
    tok_r = lax.broadcasted_iota(jnp.int32, (nrow, 1), 0) % dec
    tok_l = lax.broadcasted_iota(jnp.int32, (1, LANES), 1) % dec
    new_j = lax.broadcasted_iota(jnp.int32, (1, LANES), 1)
    new_mask = (new_j < dec) & (new_j <= tok_r)

    def pad_rows(x):
        return jnp.concatenate([x.astype(F32), jnp.zeros((LANES - dec, KV_DIM), F32)], axis=0).astype(BF16)

    def softmax2(s_p, s_n):
        m = jnp.maximum(jnp.max(s_p, axis=-1, keepdims=True), jnp.max(s_n, axis=-1, keepdims=True))
        p_p, p_n = jnp.exp(s_p - m), jnp.exp(s_n - m)
        l = jnp.sum(p_p, axis=-1, keepdims=True) + jnp.sum(p_n, axis=-1, keepdims=True)
        return p_p.astype(BF16), p_n.astype(BF16), l

    s = _nn(qp, kc_ref[0].T.astype(BF16))
    sw_p = _nn(qp, kw_ref[0, 0].astype(BF16))
    sw_n = _nt(qp, pad_rows(kwn_ref[0]))
    for cp in page_copies(b, slot):
        cp.wait()
    kt = jnp.concatenate([kbuf[slot, pg] for pg in range(n_pages)], axis=1).astype(BF16)
    ss_p = _nn(qp, kt)
    ss_n = _nt(qp, pad_rows(ksn_ref[0]))

    cmask = lax.broadcasted_iota(jnp.int32, (1, nchp), 1) < nchp - 1
    s = jnp.where(cmask, s, NEG)
    p = jnp.where(cmask, jnp.exp(s - jnp.max(s, axis=-1, keepdims=True)), 0.0)
    pc = p / jnp.sum(p, axis=-1, keepdims=True)
    o_c = _nn(pc.astype(BF16), vc_ref[0].astype(BF16))

    gsum = gsum_ref[...]
    pcs = sum(_nn(gsum, part) for part in _split3(pc))
    pcs = jnp.concatenate([pcs, jnp.zeros((LANES - nrow, nchp), F32)], axis=0)
    ovt = ovt_ref[...]
    imp_t = sum(_nt(ovt, part) for part in _split3(pcs))
    bias = _select_bias(imp_t, past + tok_l, ntop).T[:nrow].astype(BF16)

    kpos = (past - lw) + lax.broadcasted_iota(jnp.int32, (1, lw), 1)
    p_p, p_n, l = softmax2(jnp.where(kpos > past + tok_r - WINDOW, sw_p, NEG), jnp.where(new_mask, sw_n, NEG))
    o_w = (_nt(p_p, vw_ref[0, 0].astype(BF16)) + _nn(p_n, pad_rows(vwn_ref[0]))) / l

    vt = jnp.concatenate([vbuf[slot, pg] for pg in range(n_pages)], axis=1).astype(BF16)
    s_p = ss_p + _nn(bias, et_ref[...])
    s_n = jnp.where(new_mask, ss_n + _nt(bias, en_ref[...]), NEG)
    p_p, p_n, l = softmax2(s_p, s_n)
    o_s = (_nt(p_p, vt) + _nn(p_n, pad_rows(vsn_ref[0]))) / l

    g = g_ref[0]
    o_ref[0] = g[:, 0:1] * o_c + g[:, 1:2] * o_s + g[:, 2:3] * o_w


def _sattn(pages, qp, g, kc, vc, ks, vs, win_k, win_v, layer, ksn, vsn, kwn, vwn, ovt, gsum, e_t, e_new, past, dec):
    nseq, nrow = qp.shape[:2]
    nchp = kc.shape[1]
    n_pages = pages.shape[1]
    page = ks.shape[2]
    lw = win_k.shape[3]
    ntop = min(SEL_TOPK, -(-(past + dec) // SEL_BLOCK))
    seq3 = lambda n, w: pl.BlockSpec((1, n, w), lambda b, pt: (b, 0, 0))
    fixed = lambda r, c: pl.BlockSpec((r, c), lambda b, pt: (0, 0))
    hbm = pl.BlockSpec(memory_space=pl.ANY)
    win = pl.BlockSpec((1, 1, KV_DIM, lw), lambda b, pt: (layer, b, 0, 0))
    return pl.pallas_call(
        functools.partial(_sattn_body, past=past, dec=dec, ntop=ntop),
        grid_spec=pltpu.PrefetchScalarGridSpec(
            num_scalar_prefetch=1,
            grid=(nseq,),
            in_specs=[seq3(nrow, KV_DIM), seq3(nrow, SUBLANES), seq3(nchp, KV_DIM), seq3(nchp, KV_DIM),
                      hbm, hbm, win, win,
                      seq3(dec, KV_DIM), seq3(dec, KV_DIM), seq3(dec, KV_DIM), seq3(dec, KV_DIM),
                      fixed(NS_PAD, nchp), fixed(nrow, nrow), fixed(NS_PAD, past), fixed(LANES, NS_PAD)],
            out_specs=seq3(nrow, KV_DIM),
            scratch_shapes=[pltpu.VMEM((2, n_pages, KV_DIM, page), F32), pltpu.VMEM((2, n_pages, KV_DIM, page), F32),
                            pltpu.SemaphoreType.DMA((2, 2))]),
        out_shape=jax.ShapeDtypeStruct((nseq, nrow, KV_DIM), F32),
        compiler_params=_params(("arbitrary",)),
    )(pages, qp, g, kc, vc, ks, vs, win_k, win_v, ksn, vsn, kwn, vwn, ovt, gsum, e_t, e_new)


def _mix_body(x_ref, o_ref, u_ref, uprev_ref, cb_ref, sga_ref, sgb_ref, s1_ref, s2_ref,
              cw_ref, wa_ref, wc_ref, wo_ref, y_ref, ubuf, *, tm, slen):
    i = pl.program_id(0)
    u = u_ref[...]
    ubuf[0:SUBLANES, :] = uprev_ref[...]
    ubuf[SUBLANES:, :] = u
    pos = (i * tm + lax.broadcasted_iota(jnp.int32, (tm, 1), 0)) % slen
    prev1 = jnp.where(pos >= 1, ubuf[SUBLANES - 1:SUBLANES - 1 + tm, :], s1_ref[...])
    prev2 = jnp.where(pos >= 2, ubuf[SUBLANES - 2:SUBLANES - 2 + tm, :], s2_ref[...])
    cw = cw_ref[...]
    y_conv = cw[0:1] * prev2 + cw[1:2] * prev1 + cw[2:3] * u
    a_br = _nn(o_ref[...].astype(BF16), wa_ref[...])
    c_br = _nn((cb_ref[...] * y_conv).astype(BF16), wc_ref[...])
    mix = sga_ref[...] * a_br + sgb_ref[...] * c_br
    y_ref[...] = x_ref[...] + _nn(mix.astype(BF16), wo_ref[...])


def _mix(x, o_att, u, cb, sga, sgb, side1, side2, conv_w, wa, wc, wo, tm, slen):
    t, d = x.shape
    row = lambda width: pl.BlockSpec((tm, width), lambda i: (i, 0))
    fixed = lambda shape: pl.BlockSpec(shape, lambda i: (0, 0))
    n_side = side1.shape[0] // tm
    side = pl.BlockSpec((tm, CONV_DIM), lambda i: (i % n_side, 0))
    prev = pl.BlockSpec((SUBLANES, CONV_DIM), lambda i: (jnp.maximum(i * (tm // SUBLANES) - 1, 0), 0))
    return pl.pallas_call(
        functools.partial(_mix_body, tm=tm, slen=slen),
        grid=(t // tm,),
        in_specs=[row(d), row(ATT_DIM), row(CONV_DIM), prev, row(CONV_DIM), row(d), row(d), side, side,
                  fixed((SUBLANES, CONV_DIM)), fixed((ATT_DIM, d)), fixed((CONV_DIM, d)), fixed((d, d))],
        out_specs=row(d),
        out_shape=jax.ShapeDtypeStruct((t, d), F32),
        scratch_shapes=[pltpu.VMEM((tm + SUBLANES, CONV_DIM), F32)],
        compiler_params=_params(("arbitrary",)),
    )(x, o_att, u, u, cb, sga, sgb, side1, side2, conv_w, wa, wc, wo)


def _ffn_body(x_ref, g_ref, win_ref, wout_ref, gf_ref, y_ref, *, d_ff, fc, final):
    x = x_ref[...]
    h = x * lax.rsqrt(jnp.mean(x * x, axis=-1, keepdims=True) + EPS)
    hb = (h * g_ref[...]).astype(BF16)
    acc = x
    for c in range(d_ff // fc):
        gate = _nn(hb, win_ref[:, c * fc:(c + 1) * fc])
        up = _nn(hb, win_ref[:, d_ff + c * fc:d_ff + (c + 1) * fc])
        act = (gate * jax.nn.sigmoid(gate) * up).astype(BF16)
        acc = acc + _nn(act, wout_ref[c * fc:(c + 1) * fc, :])
    if final:
        acc = acc * lax.rsqrt(jnp.mean(acc * acc, axis=-1, keepdims=True) + EPS) * gf_ref[...]
    y_ref[...] = acc


def _ffn(x, g, w_in, w_out, g_final, tm, final):
    t, d = x.shape
    d_ff = w_out.shape[0]
    fc = 256 if d_ff % 256 == 0 else LANES
    row = pl.BlockSpec((tm, d), lambda i: (i, 0))
    fixed = lambda shape: pl.BlockSpec(shape, lambda i: (0, 0))
    return pl.pallas_call(
        functools.partial(_ffn_body, d_ff=d_ff, fc=fc, final=final),
        grid=(t // tm,),
        in_specs=[row, fixed((1, d)), fixed((d, 2 * d_ff)), fixed((d_ff, d)), fixed((1, d))],
        out_specs=row,
        out_shape=jax.ShapeDtypeStruct((t, d), F32),
        compiler_params=_params(("arbitrary",)),
    )(x, g, w_in, w_out, g_final)


def _rope_tables(pos):
    inv = 1.0 / (ROPE_THETA ** (jnp.arange(ROPE_HALF, dtype=F32) * (2.0 / ROPE_DIM)))
    ang = pos.astype(F32)[:, None] * inv[None, :]
    cos, sin = jnp.cos(ang), jnp.sin(ang)
    t = pos.shape[0]
    pad = jnp.zeros((t, HEAD_DIM - ROPE_DIM), F32)
    rc = jnp.concatenate([cos, cos, pad + 1.0], axis=1)
    rs1 = jnp.concatenate([-sin, jnp.zeros_like(sin), pad], axis=1)
    rs2 = jnp.concatenate([jnp.zeros_like(sin), sin, pad], axis=1)
    rep = LANES // HEAD_DIM
    return tuple(jnp.tile(a, (1, rep)) for a in (rc, rs1, rs2))


def _pack_w_in(w):
    wt = w.T
    d = wt.shape[1]
    o_gl = ATT_DIM + 6 * KV_DIM
    gl = wt[o_gl:o_gl + 3 * N_HEADS].reshape(N_KV, 3 * GROUP, d)
    gl = jnp.pad(gl, ((0, 0), (0, LANES - 3 * GROUP), (0, 0))).reshape(N_KV * LANES, d)
    return jnp.concatenate([wt[:o_gl], gl, wt[o_gl + 3 * N_HEADS:]], axis=0).astype(BF16)


def _pack_cmp(pe, w1, w2):
    eye = jnp.eye(N_KV, dtype=F32)
    pe2 = jnp.broadcast_to(pe.reshape(2, CMP_STRIDE, 1, HEAD_DIM), (2, CMP_STRIDE, N_KV, HEAD_DIM)).reshape(2, CHUNK)
    w1r = w1.reshape(2, CMP_STRIDE, HEAD_DIM, CMP_HIDDEN)
    w1p = jnp.einsum('pjdh,kc->pjkdch', w1r, eye).reshape(2, CHUNK, HID2).astype(BF16)
    w2p = jnp.einsum('hd,kc->khcd', w2, eye).reshape(HID2, KV_DIM).astype(BF16)
    return pe2, w1p, w2p


def _overlap_t(nch):
    ci = jnp.arange(nch)[None, :] * CMP_STRIDE
    sj = jnp.arange(NS_PAD)[:, None] * SEL_BLOCK
    return ((ci <= sj + SEL_BLOCK - 1) & (ci + CMP_BLOCK - 1 >= sj)).astype(BF16)


def _dim_major(a):
    lead = a.shape[:-3]
    n = len(lead)
    return a.transpose(*range(n), n + 1, n + 2, n).reshape(*lead, KV_DIM, a.shape[-3])


def _pos_major(a):
    lead = a.shape[:-2]
    n = len(lead)
    return a.reshape(*lead, N_KV, HEAD_DIM, a.shape[-1]).transpose(*range(n), n + 2, n, n + 1)


def _prompt_attention(q, gates, kcvc, kskw, vt, cmpk, cmpv, nb, seq):
    nch = seq // CMP_STRIDE
    abk = _cmp_ab_rows(kcvc, nb * seq, 0, cmpk[0], cmpk[1], seq)
    abv = _cmp_ab_rows(kcvc, nb * seq, 1, cmpv[0], cmpv[1], seq)
    kcmp, vcmp = _cmp_fin(abk, abv, cmpk[2], cmpv[2], nch, nch)
    kcmp = kcmp.reshape(nb, nch, KV_DIM).astype(BF16)
    vcmp_t = vcmp.reshape(nb, nch, KV_DIM).transpose(0, 2, 1).astype(BF16)
    return _pattn(q, gates, kcmp, vcmp_t, kskw, vt, _overlap_t(nch), nb, seq)


def _sample_cache(cmpk, cmpv, pool_k, pool_v, page_table, layer):
    ndb, n_pages = page_table.shape
    n_pool, _, page = pool_k.shape[1:]
    cpp = page // CMP_STRIDE
    nchp = n_pages * cpp
    pp = math.gcd(n_pool, POOL_PAGES)
    abk = _cmp_ab_pages(pool_k, layer, cmpk[0], cmpk[1], pp).reshape(n_pool, cpp, 2 * HID2)[page_table]
    abv = _cmp_ab_pages(pool_v, layer, cmpv[0], cmpv[1], pp).reshape(n_pool, cpp, 2 * HID2)[page_table]
    kcmp, vcmp = _cmp_fin(abk.reshape(ndb * nchp, 2 * HID2), abv.reshape(ndb * nchp, 2 * HID2), cmpk[2], cmpv[2],
                          nchp, math.gcd(ndb, SUBLANES) * nchp)
    return kcmp.reshape(ndb, nchp, KV_DIM), vcmp.reshape(ndb, nchp, KV_DIM)


def _sample_attention(q, gates, new4, cache, slc_k, slc_v, page_table, win_k, win_v, layer, dec):
    kcmp, vcmp = cache
    ndb, nchp = kcmp.shape[:2]
    n_pool, _, page = slc_k.shape[1:]
    past = page_table.shape[1] * page
    pages = (layer * n_pool + page_table).astype(jnp.int32)
    nrow = N_KV * GROUP * dec
    eye_kv = jnp.eye(N_KV, dtype=F32)
    q_s = q.reshape(ndb, dec, N_KV, GROUP, HEAD_DIM).transpose(0, 2, 3, 1, 4)
    qp = (q_s[:, :, :, :, None] * eye_kv[None, :, None, None, :, None]).reshape(ndb, nrow, KV_DIM).astype(BF16)
    g_s = gates.reshape(ndb, dec, N_KV, LANES)[..., :3 * GROUP].reshape(ndb, dec, N_KV, GROUP, 3)
    g_s = jnp.pad(g_s.transpose(0, 2, 3, 1, 4).reshape(ndb, nrow, 3), ((0, 0), (0, 0), (0, SUBLANES - 3)))
    e_t = jax.nn.one_hot(jnp.arange(past) // SEL_BLOCK, NS_PAD, dtype=BF16).T
    e_new = jax.nn.one_hot((past + jnp.arange(LANES)) // SEL_BLOCK, NS_PAD, dtype=BF16)
    rid = jnp.arange(nrow)
    gsum = ((rid[:, None] // (GROUP * dec) == rid[None, :] // (GROUP * dec))
            & (rid[:, None] % dec == rid[None, :] % dec)).astype(BF16)
    ksn, vsn, kwn, vwn = (a.reshape(ndb, dec, KV_DIM) for a in new4)
    o_t = _sattn(pages, qp, g_s, kcmp, vcmp, slc_k.reshape(-1, KV_DIM, page), slc_v.reshape(-1, KV_DIM, page),
                 win_k, win_v, layer,
                 ksn, vsn, kwn, vwn, _overlap_t(nchp), gsum, e_t, e_new, past, dec)
    o_t = o_t.reshape(ndb, N_KV, GROUP, dec, N_KV, HEAD_DIM)
    return jnp.einsum('bkgtcd,kc->btkgd', o_t, eye_kv).reshape(ndb * dec, ATT_DIM)


def kernel(x_prompt, x_sample, cache_cmp_k, cache_cmp_v, cache_slc_k, cache_slc_v, state_win_k, state_win_v,
           state_conv, page_table, w_in, cmp_pe, cmp_w1, cmp_w2, w_att_out, conv_w, w_conv_out, w_o,
           norm_mix, norm_ffn, w_ffn_in, w_ffn_out, norm_final):
    nb, seq, d = x_prompt.shape
    ndb, dec, _ = x_sample.shape
    depth, n_pool, page = cache_cmp_k.shape[:3]
    n_pages = page_table.shape[1]
    past = n_pages * page
    lw = state_win_k.shape[2]
    tp, ts = nb * seq, ndb * dec
    tm_p, tm_s = math.gcd(ROW_TILE, seq), math.gcd(ROW_TILE, ts)
    nchp = past // CMP_STRIDE
    keep = min(WINDOW, seq)
    assert seq % SEL_GROUP == 0 and seq >= WIN_SPAN and seq // SEL_BLOCK <= NS_PAD
    assert dec <= SUBLANES and N_KV * GROUP * dec <= LANES and (N_KV * GROUP * dec) % (2 * SUBLANES) == 0
    assert tm_s % SUBLANES == 0 and tm_s % dec == 0
    assert (past + dec - CMP_BLOCK) // CMP_STRIDE + 1 == nchp - 1
    assert -(-(past + dec) // SEL_BLOCK) <= NS_PAD and past >= WINDOW and lw == WINDOW

    rope_p = _rope_tables(jnp.arange(seq, dtype=jnp.int32))
    rope_s = _rope_tables(jnp.tile(past + jnp.arange(dec, dtype=jnp.int32), ndb))
    pool_k, pool_v, slc_k, slc_v, win_k, win_v = (
        _dim_major(a) for a in (cache_cmp_k, cache_cmp_v, cache_slc_k, cache_slc_v, state_win_k, state_win_v))
    no_side = jnp.zeros((tm_p, CONV_DIM), F32)
    zero = jnp.zeros((ndb, dec - 1, CONV_DIM), F32)

    cmp_w = [[_pack_cmp(cmp_pe[l, a], cmp_w1[l, a], cmp_w2[l, a]) for a in range(2)] for l in range(depth)]
    caches = [_sample_cache(*cmp_w[l], pool_k, pool_v, page_table, l) for l in range(depth)]

    xp, xs = x_prompt.reshape(tp, d), x_sample.reshape(ts, d)
    st_p = [[] for _ in range(7)]
    st_s = [[] for _ in range(7)]
    for l in range(depth):
        w_l = _pack_w_in(w_in[l])
        g_l = norm_mix[l][None, :]
        cmpk, cmpv = cmp_w[l]
        cw = jnp.pad(conv_w[l], ((0, SUBLANES - CONV_WIDTH), (0, 0)))
        w_mix = (w_att_out[l].astype(BF16), w_conv_out[l].astype(BF16), w_o[l].astype(BF16))
        w_ffn = (norm_ffn[l][None, :], w_ffn_in[l].astype(BF16), w_ffn_out[l].astype(BF16), norm_final[None, :])
        final = l == depth - 1

        q, kcvc, kskw, vt, stp, gates, u, cb, sga, sgb = _inproj(xp, g_l, w_l, rope_p, nb, tm_p)
        o_att = _prompt_attention(q, gates, kcvc, kskw, vt, cmpk, cmpv, nb, seq)
        xp = _mix(xp, o_att, u, cb, sga, sgb, no_side, no_side, cw, *w_mix, tm_p, seq)
        xp = _ffn(xp, *w_ffn, tm_p, final)
        for j in range(4):
            st_p[j].append(stp[j])
        st_p[4].append(stp[4][:, :, seq - keep:])
        st_p[5].append(stp[5][:, :, seq - keep:])
        st_p[6].append(jnp.stack([u[(b + 1) * seq - (CONV_WIDTH - 1):(b + 1) * seq] for b in range(nb)]))

        q, kcvc, kskw, vt, sts, gates, u, cb, sga, sgb = _inproj(xs, g_l, w_l, rope_s, 1, tm_s)
        new_t = sts[:, 0].reshape(6, KV_DIM, ndb, dec).transpose(0, 2, 1, 3)
        new_rows = lambda j: sts[j, 0].T
        o_att = _sample_attention(q, gates, [kskw[:, :KV_DIM], new_rows(3), kskw[:, KV_DIM:], new_rows(5)],
                                  caches[l], slc_k, slc_v, page_table, win_k, win_v, l, dec)
        cbuf = state_conv[l]
        side1 = jnp.concatenate([cbuf[:, 1:2], zero], axis=1).reshape(ts, CONV_DIM)
        side2 = jnp.concatenate([cbuf[:, 0:2], zero[:, 1:]], axis=1).reshape(ts, CONV_DIM)
        xs = _mix(xs, o_att, u, cb, sga, sgb, side1, side2, cw, *w_mix, tm_s, dec)
        xs = _ffn(xs, *w_ffn, tm_s, final)
        for j in range(6):
            st_s[j].append(new_t[j])
        st_s[6].append(jnp.concatenate([cbuf, u.reshape(ndb, dec, CONV_DIM)], axis=1)[:, -(CONV_WIDTH - 1):])

    kv_p = [_pos_major(jnp.stack(a)) for a in st_p[:6]]
    kv_s = [jnp.stack(a) for a in st_s[:6]]
    kv_s[4] = jnp.concatenate([win_k[:, :, :, dec:], kv_s[4]], axis=-1)
    kv_s[5] = jnp.concatenate([win_v[:, :, :, dec:], kv_s[5]], axis=-1)
    kv_s = [_pos_major(a) for a in kv_s]
    return (xp.reshape(nb, seq, d), xs.reshape(ndb, dec, d), *kv_p, jnp.stack(st_p[6]),
            *kv_s, jnp.stack(st_s[6]))
```

```python
import functools
import math

import jax
import jax.numpy as jnp
from jax import lax
from jax.experimental import pallas as pl
from jax.experimental.pallas import tpu as pltpu

N_HEADS = 8
HEAD_DIM = 64
N_KV = 2
GROUP = N_HEADS // N_KV
ATT_DIM = N_HEADS * HEAD_DIM
KV_DIM = N_KV * HEAD_DIM
ROPE_DIM = HEAD_DIM // 4
ROPE_HALF = ROPE_DIM // 2
ROPE_THETA = 500000.0
CMP_BLOCK = 32
CMP_STRIDE = 16
CMP_HIDDEN = 2 * HEAD_DIM
SEL_BLOCK = 64
SEL_TOPK = 16
N_LOCAL_BLOCKS = 2
WINDOW = 512
CONV_DIM = 512
CONV_WIDTH = 3
EPS = 1e-6
NEG = -1e30
FORCE_SCORE = 1e4

LANES = 128
SUBLANES = 8
Q_TILE = 128
SEL_CHUNK = 256
SEL_GROUP = 4 * SEL_CHUNK
LOG2E = 1.4426950408889634
VT_ROWS = HEAD_DIM + 16
WIN_SPAN = WINDOW + Q_TILE
ROW_TILE = 512
POOL_PAGES = 64
NS_PAD = LANES
VMEM_LIMIT = 56 * 1024 * 1024

C_Q = 0
C_KV = C_Q + ATT_DIM
C_GL = C_KV + 6 * KV_DIM
C_CU = C_GL + N_KV * LANES
C_CB = C_CU + CONV_DIM
C_CC = C_CB + CONV_DIM
F32 = jnp.float32
BF16 = jnp.bfloat16


def _nt(a, b):
    return lax.dot_general(a, b, (((1,), (1,)), ((), ())), preferred_element_type=F32)


def _nn(a, b):
    return jnp.dot(a, b, preferred_element_type=F32)


def _split3(x):
    hi = x.astype(BF16)
    r1 = x - hi.astype(F32)
    mid = r1.astype(BF16)
    lo = (r1 - mid.astype(F32)).astype(BF16)
    return hi, mid, lo


def _params(sem):
    return pltpu.CompilerParams(dimension_semantics=sem, vmem_limit_bytes=VMEM_LIMIT)


def _inproj_body(x_ref, g_ref, w_ref, rc_ref, rs1_ref, rs2_ref,
                 q_ref, kcvc_ref, kskw_ref, vt_ref, st_ref, gate_ref, u_ref, cb_ref, sga_ref, sgb_ref, *, d_model):
    x = x_ref[...]
    h = x * lax.rsqrt(jnp.mean(x * x, axis=-1, keepdims=True) + EPS)
    hb = (h * g_ref[...]).astype(BF16)
    rc, rs1, rs2 = rc_ref[...], rs1_ref[...], rs2_ref[...]

    def mm(lo, width):
        return _nt(hb, w_ref[lo:lo + width, :])

    def rope(z):
        return z * rc + pltpu.roll(z, LANES - ROPE_HALF, 1) * rs1 + pltpu.roll(z, ROPE_HALF, 1) * rs2

    for j in range(ATT_DIM // 256):
        z = mm(C_Q + 256 * j, 256)
        for t in range(2):
            q_ref[:, 256 * j + LANES * t:256 * j + LANES * (t + 1)] = (
                rope(z[:, LANES * t:LANES * (t + 1)]) * (HEAD_DIM ** -0.5))
    for j in range(3):
        z = mm(C_KV + 256 * j, 256)
        k, v = rope(z[:, :LANES]), z[:, LANES:]
        kt, vt = k.T, v.T
        st_ref[2 * j, 0] = kt
        st_ref[2 * j + 1, 0] = vt
        if j == 0:
            kcvc_ref[:, :LANES] = k
            kcvc_ref[:, LANES:] = v
        else:
            kskw_ref[:, LANES * (j - 1):LANES * j] = k.astype(BF16)
            ones_row = (lax.broadcasted_iota(jnp.int32, (VT_ROWS - HEAD_DIM, vt.shape[1]), 0) == 0).astype(BF16)
            for kvh in range(N_KV):
                vt_ref[j - 1, 0, VT_ROWS * kvh:VT_ROWS * kvh + HEAD_DIM] = vt[HEAD_DIM * kvh:HEAD_DIM * (kvh + 1)].astype(BF16)
                vt_ref[j - 1, 0, VT_ROWS * kvh + HEAD_DIM:VT_ROWS * (kvh + 1)] = ones_row
    gate_ref[...] = jax.nn.sigmoid(mm(C_GL, N_KV * LANES))
    for j in range(CONV_DIM // 256):
        u_ref[:, 256 * j:256 * (j + 1)] = mm(C_CC + 256 * j, 256) * mm(C_CU + 256 * j, 256)
        cb_ref[:, 256 * j:256 * (j + 1)] = mm(C_CB + 256 * j, 256)
    c_ga = C_CC + CONV_DIM
    c_gb = c_ga + d_model
    for j in range(d_model // 256):
        sga_ref[:, 256 * j:256 * (j + 1)] = jax.nn.sigmoid(mm(c_ga + 256 * j, 256))
        sgb_ref[:, 256 * j:256 * (j + 1)] = jax.nn.sigmoid(mm(c_gb + 256 * j, 256))


def _inproj(x, g, w, rope, nseq, tm):
    t, d = x.shape
    slen = t // nseq
    tps = slen // tm
    nrows = w.shape[0]
    row = lambda width: pl.BlockSpec((tm, width), lambda i: (i, 0))
    fixed = lambda shape: pl.BlockSpec(shape, lambda i: (0, 0))
    tab = pl.BlockSpec((tm, LANES), lambda i: (i % tps, 0))
    tr = lambda n, rows: pl.BlockSpec((n, 1, rows, tm), lambda i: (0, i // tps, 0, i % tps))
    sds = jax.ShapeDtypeStruct
    return pl.pallas_call(
        functools.partial(_inproj_body, d_model=d),
        grid=(t // tm,),
        in_specs=[row(d), fixed((1, d)), fixed((nrows, d)), tab, tab, tab],
        out_specs=[row(ATT_DIM), row(2 * KV_DIM), row(2 * KV_DIM), tr(2, N_KV * VT_ROWS), tr(6, KV_DIM), row(N_KV * LANES),
                   row(CONV_DIM), row(CONV_DIM), row(d), row(d)],
        out_shape=[sds((t, ATT_DIM), F32), sds((t, 2 * KV_DIM), F32), sds((t, 2 * KV_DIM), BF16),
                   sds((2, nseq, N_KV * VT_ROWS, slen), BF16), sds((6, nseq, KV_DIM, slen), F32), sds((t, N_KV * LANES), F32),
                   sds((t, CONV_DIM), F32), sds((t, CONV_DIM), F32), sds((t, d), F32), sds((t, d), F32)],
        compiler_params=_params(("arbitrary",)),
    )(x, g, w, *rope)


CHUNK = CMP_STRIDE * KV_DIM
HID2 = N_KV * CMP_HIDDEN


def _cmp_partials(src, pe_ref, w_ref, ab_ref):
    n = src.shape[0] // CMP_STRIDE
    acc_a = jnp.zeros((n, HID2), F32)
    acc_b = jnp.zeros((n, HID2), F32)
    for jp in range(CMP_STRIDE // 2):
        x = jnp.concatenate([src[pl.ds(2 * jp, n, stride=CMP_STRIDE), :],
                             src[pl.ds(2 * jp + 1, n, stride=CMP_STRIDE), :]], axis=1)
        cols = slice(2 * KV_DIM * jp, 2 * KV_DIM * (jp + 1))
        acc_a = acc_a + _nn((x + pe_ref[0:1, cols]).astype(BF16), w_ref[0, cols, :])
        acc_b = acc_b + _nn((x + pe_ref[1:2, cols]).astype(BF16), w_ref[1, cols, :])
    ab_ref[:, :HID2] = acc_a
    ab_ref[:, HID2:] = acc_b


def _cmp_ab_rows_body(x_ref, pe_ref, w_ref, ab_ref):
    _cmp_partials(x_ref, pe_ref, w_ref, ab_ref)


def _cmp_ab_rows(x, rows, col, pe, w, tr):
    return pl.pallas_call(
        _cmp_ab_rows_body,
        grid=(rows // tr,),
        in_specs=[pl.BlockSpec((tr, KV_DIM), lambda i: (i, col)),
                  pl.BlockSpec((2, CHUNK), lambda i: (0, 0)),
                  pl.BlockSpec((2, CHUNK, HID2), lambda i: (0, 0, 0))],
        out_specs=pl.BlockSpec((tr // CMP_STRIDE, 2 * HID2), lambda i: (i, 0)),
        out_shape=jax.ShapeDtypeStruct((rows // CMP_STRIDE, 2 * HID2), F32),
        compiler_params=_params(("arbitrary",)),
    )(x, pe, w)


def _cmp_ab_pages_body(x_ref, pe_ref, w_ref, ab_ref, rows_scr):
    page = x_ref.shape[3]
    for p in range(x_ref.shape[1]):
        rows_scr[page * p:page * (p + 1), :] = x_ref[0, p].T
    _cmp_partials(rows_scr, pe_ref, w_ref, ab_ref)


def _cmp_ab_pages(pool, layer, pe, w, pp):
    _, n_pool, _, page = pool.shape
    cpp = page // CMP_STRIDE
    return pl.pallas_call(
        _cmp_ab_pages_body,
        grid=(n_pool // pp,),
        in_specs=[pl.BlockSpec((1, pp, KV_DIM, page), lambda i: (layer, i, 0, 0)),
                  pl.BlockSpec((2, CHUNK), lambda i: (0, 0)),
                  pl.BlockSpec((2, CHUNK, HID2), lambda i: (0, 0, 0))],
        out_specs=pl.BlockSpec((pp * cpp, 2 * HID2), lambda i: (i, 0)),
        out_shape=jax.ShapeDtypeStruct((n_pool * cpp, 2 * HID2), F32),
        scratch_shapes=[pltpu.VMEM((pp * page, KV_DIM), F32)],
        compiler_params=_params(("arbitrary",)),
    )(pool, pe, w)


def _cmp_fin_body(abk_ref, abv_ref, w2k_ref, w2v_ref, ok_ref, ov_ref, *, nch):
    rows = abk_ref.shape[0]
    valid = lax.broadcasted_iota(jnp.int32, (rows, 1), 0) % nch < nch - 1

    def fin(ab_ref, w2_ref, o_ref):
        ab = ab_ref[...]
        hid = ab[:, :HID2] + pltpu.roll(ab[:, HID2:], rows - 1, 0)
        act = hid * jax.nn.sigmoid(hid)
        o_ref[...] = jnp.where(valid, _nn(act.astype(BF16), w2_ref[...]), 0.0)

    fin(abk_ref, w2k_ref, ok_ref)
    fin(abv_ref, w2v_ref, ov_ref)


def _cmp_fin(abk, abv, w2k, w2v, nch, tr):
    rows = abk.shape[0]
    ab_spec = pl.BlockSpec((tr, 2 * HID2), lambda i: (i, 0))
    w_spec = pl.BlockSpec((HID2, KV_DIM), lambda i: (0, 0))
    o_spec = pl.BlockSpec((tr, KV_DIM), lambda i: (i, 0))
    return pl.pallas_call(
        functools.partial(_cmp_fin_body, nch=nch),
        grid=(rows // tr,),
        in_specs=[ab_spec, ab_spec, w_spec, w_spec],
        out_specs=[o_spec, o_spec],
        out_shape=[jax.ShapeDtypeStruct((rows, KV_DIM), F32)] * 2,
        compiler_params=_params(("arbitrary",)),
    )(abk, abv, w2k, w2v)


def _select_bias(imp_t, qpos, ntop):
    shape = imp_t.shape
    s_io = lax.broadcasted_iota(jnp.int32, shape, 0)
    cur = qpos // SEL_BLOCK
    avail = s_io <= cur
    forced = (s_io == 0) | (avail & (s_io > cur - N_LOCAL_BLOCKS))
    val = jnp.where(avail, jnp.where(forced, FORCE_SCORE, imp_t), NEG)

    def pick_one(_, carry):
        val, sel = carry
        m = jnp.max(val, axis=0, keepdims=True)
        idx = jnp.min(jnp.where(val == m, s_io, NS_PAD), axis=0, keepdims=True)
        pick = s_io == idx
        return jnp.where(pick, -jnp.inf, val), jnp.where(pick, 1.0, sel)

    _, sel = lax.fori_loop(0, ntop, pick_one, (val, jnp.zeros(shape, F32)), unroll=True)
    return jnp.where((sel > 0.0) & avail, 0.0, NEG)


def _pattn_body(q_ref, g_ref, kc_ref, vct_ref, ks_ref, kw_ref, vst_ref, vwt_ref, ovt_ref, o_ref, bias_scr, *, seq):
    kv = pl.program_id(1)
    i = pl.program_id(2)
    nch = seq // CMP_STRIDE
    cols = GROUP * Q_TILE
    q = q_ref[...]
    qt = jnp.concatenate([q[:, :LANES].T, q[:, LANES:].T], axis=0)
    qt4 = jnp.concatenate([qt[HEAD_DIM * h:HEAD_DIM * (h + 1)] for h in range(GROUP)], axis=1)
    row_kv = lax.broadcasted_iota(jnp.int32, (KV_DIM, 1), 0) // HEAD_DIM
    qt_pad = jnp.where(row_kv == kv, jnp.concatenate([qt4] * N_KV, axis=0) * LOG2E, 0.0).astype(BF16)
    qpos1 = i * Q_TILE + lax.broadcasted_iota(jnp.int32, (1, Q_TILE), 1)
    qpos = jnp.concatenate([qpos1] * GROUP, axis=1)

    n_below = (i * Q_TILE) // SEL_GROUP
    w0 = pl.multiple_of(jnp.maximum(i * Q_TILE - WINDOW, 0), Q_TILE)
    s = _nn(kc_ref[0], qt_pad)
    s_win = _nn(kw_ref[pl.ds(w0, WIN_SPAN), :], qt_pad)
    s_diag = _nn(ks_ref[pl.ds(pl.multiple_of(n_below * SEL_GROUP, SEL_GROUP), SEL_GROUP), :], qt_pad)

    cmask = lax.broadcasted_iota(jnp.int32, (nch, 1), 0) * CMP_STRIDE + (CMP_BLOCK - 1) <= qpos
    s = jnp.where(cmask, s, NEG)
    p = jnp.where(cmask, jnp.exp2(s - jnp.max(s, axis=0, keepdims=True)), 0.0)
    l = jnp.sum(p, axis=0, keepdims=True)
    pc = p * jnp.where(l > 0.0, 1.0 / l, 0.0)
    o_c = _nn(vct_ref[0], pc.astype(BF16))

    pcs = pc[:, :Q_TILE]
    for h in range(1, GROUP):
        pcs = pcs + pc[:, h * Q_TILE:(h + 1) * Q_TILE]
    ovt = ovt_ref[...]
    imp_t = sum(_nn(ovt, part) for part in _split3(pcs))
    bias_scr[...] = _select_bias(imp_t, qpos1, min(SEL_TOPK, seq // SEL_BLOCK))

    kpos = w0 + lax.broadcasted_iota(jnp.int32, (WIN_SPAN, 1), 0)
    s = jnp.where((kpos <= qpos) & (kpos > qpos - WINDOW), s_win, NEG)
    p = jnp.exp2(s - jnp.max(s, axis=0, keepdims=True))
    acc_w = _nn(vwt_ref[0, 0, :, pl.ds(w0, WIN_SPAN)], p.astype(BF16))
    o_w = acc_w[:HEAD_DIM] / acc_w[HEAD_DIM:HEAD_DIM + 1]

    def group(gj, nkeys, carry, s_all, causal):
        m, acc = carry
        g0 = pl.multiple_of(gj * nkeys, nkeys)
        if s_all is None:
            s_all = _nn(ks_ref[pl.ds(g0, nkeys), :], qt_pad)
        for c in range(nkeys // SEL_CHUNK):
            parts = []
            for r in range(SEL_CHUNK // SEL_BLOCK):
                row = SEL_CHUNK * c + SEL_BLOCK * r
                b_row = bias_scr[pl.ds(gj * (nkeys // SEL_BLOCK) + row // SEL_BLOCK, 1), :]
                parts.append(s_all[row:row + SEL_BLOCK] + jnp.concatenate([b_row] * GROUP, axis=1))
            s = jnp.concatenate(parts, axis=0)
            k0 = pl.multiple_of(g0 + SEL_CHUNK * c, SEL_CHUNK)
            if causal:
                kpos = k0 + lax.broadcasted_iota(jnp.int32, (SEL_CHUNK, 1), 0)
                s = jnp.where(kpos <= qpos, s, NEG)
            m_new = jnp.maximum(m, jnp.max(s, axis=0, keepdims=True))
            alpha = jnp.exp2(m - m_new)
            p = jnp.exp2(s - m_new)
            acc = alpha * acc + _nn(vst_ref[0, 0, :, pl.ds(k0, SEL_CHUNK)], p.astype(BF16))
            m = m_new
        return m, acc

    init = (jnp.full((1, cols), -3e38, F32), jnp.zeros((VT_ROWS, cols), F32))
    carry = group(n_below, SEL_GROUP, init, s_diag, True)
    carry = lax.fori_loop(0, n_below // 2, lambda gj, c: group(gj, 2 * SEL_GROUP, c, None, False), carry)
    _, acc_s = lax.cond(n_below % 2 == 1, lambda c: group(n_below - 1, SEL_GROUP, c, None, False), lambda c: c, carry)
    o_s = acc_s[:HEAD_DIM] / acc_s[HEAD_DIM:HEAD_DIM + 1]

    gt = g_ref[...].T
    outs = []
    for h in range(GROUP):
        c = slice(h * Q_TILE, (h + 1) * Q_TILE)
        outs.append(gt[3 * h:3 * h + 1] * o_c[:, c] + gt[3 * h + 1:3 * h + 2] * o_s[:, c]
                    + gt[3 * h + 2:3 * h + 3] * o_w[:, c])
    for t in range(GROUP // 2):
        o_ref[:, LANES * t:LANES * (t + 1)] = jnp.concatenate(outs[2 * t:2 * t + 2], axis=0).T


def _pattn(q, gates, kcmp, vcmp_t, kskw, vt, ovt, nb, seq):
    nq = seq // Q_TILE
    nch = seq // CMP_STRIDE
    qspec = pl.BlockSpec((Q_TILE, GROUP * HEAD_DIM), lambda b, k, i: (b * nq + i, k))
    gspec = pl.BlockSpec((Q_TILE, LANES), lambda b, k, i: (b * nq + i, k))
    kspec = lambda which: pl.BlockSpec((seq, KV_DIM), lambda b, k, i: (b, which))
    vspec = lambda which: pl.BlockSpec((1, 1, VT_ROWS, seq), lambda b, k, i: (which, b, k, 0))
    return pl.pallas_call(
        functools.partial(_pattn_body, seq=seq),
        grid=(nb, N_KV, nq),
        in_specs=[qspec, gspec,
                  pl.BlockSpec((1, nch, KV_DIM), lambda b, k, i: (b, 0, 0)),
                  pl.BlockSpec((1, HEAD_DIM, nch), lambda b, k, i: (b, k, 0)),
                  kspec(0), kspec(1), vspec(0), vspec(1),
                  pl.BlockSpec((NS_PAD, nch), lambda b, k, i: (0, 0))],
        out_specs=qspec,
        out_shape=jax.ShapeDtypeStruct((nb * seq, ATT_DIM), F32),
        scratch_shapes=[pltpu.VMEM((NS_PAD, Q_TILE), F32)],
        compiler_params=_params(("arbitrary", "arbitrary", "arbitrary")),
    )(q, gates, kcmp, vcmp_t, kskw, kskw, vt, vt, ovt)


def _sattn_body(pages_ref, qp_ref, g_ref, kc_ref, vc_ref, ks_hbm, vs_hbm, kw_ref, vw_ref,
                ksn_ref, vsn_ref, kwn_ref, vwn_ref, ovt_ref, gsum_ref, et_ref, en_ref, o_ref,
                kbuf, vbuf, sems, *, past, dec, ntop):
    b = pl.program_id(0)
    nchp = kc_ref.shape[1]
    n_pages = kbuf.shape[1]
    lw = kw_ref.shape[3]

    def page_copies(seq_idx, slot):
        return [pltpu.make_async_copy(src.at[pages_ref[seq_idx, pg]], buf.at[slot, pg], sems.at[a, slot])
                for a, (src, buf) in enumerate(((ks_hbm, kbuf), (vs_hbm, vbuf))) for pg in range(n_pages)]

    slot = b % 2

    @pl.when(b == 0)
    def _():
        for cp in page_copies(0, 0):
            cp.start()

    @pl.when(b + 1 < pl.num_programs(0))
    def _():
        for cp in page_copies(b + 1, 1 - slot):
            cp.start()

    qp = qp_ref[0]
    nrow = qp.shape[0]
    tok_r = lax.broadcasted_iota(jnp.int32, (nrow, 1), 0) % dec
    tok_l = lax.broadcasted_iota(jnp.int32, (1, LANES), 1) % dec
    new_j = lax.broadcasted_iota(jnp.int32, (1, LANES), 1)
    new_mask = (new_j < dec) & (new_j <= tok_r)

    def pad_rows(x):
        return jnp.concatenate([x.astype(F32), jnp.zeros((LANES - dec, KV_DIM), F32)], axis=0).astype(BF16)

    def softmax2(s_p, s_n):
        m = jnp.maximum(jnp.max(s_p, axis=-1, keepdims=True), jnp.max(s_n, axis=-1, keepdims=True))
        p_p, p_n = jnp.exp(s_p - m), jnp.exp(s_n - m)
        l = jnp.sum(p_p, axis=-1, keepdims=True) + jnp.sum(p_n, axis=-1, keepdims=True)
        return p_p.astype(BF16), p_n.astype(BF16), l

    s = _nn(qp, kc_ref[0].T.astype(BF16))
    sw_p = _nn(qp, kw_ref[0, 0].astype(BF16))
    sw_n = _nt(qp, pad_rows(kwn_ref[0]))
    for cp in page_copies(b, slot):
        cp.wait()
    kt = jnp.concatenate([kbuf[slot, pg] for pg in range(n_pages)], axis=1).astype(BF16)
    ss_p = _nn(qp, kt)
    ss_n = _nt(qp, pad_rows(ksn_ref[0]))

    cmask = lax.broadcasted_iota(jnp.int32, (1, nchp), 1) < nchp - 1
    s = jnp.where(cmask, s, NEG)
    p = jnp.where(cmask, jnp.exp(s - jnp.max(s, axis=-1, keepdims=True)), 0.0)
    pc = p / jnp.sum(p, axis=-1, keepdims=True)
    o_c = _nn(pc.astype(BF16), vc_ref[0].astype(BF16))

    gsum = gsum_ref[...]
    pcs = sum(_nn(gsum, part) for part in _split3(pc))
    pcs = jnp.concatenate([pcs, jnp.zeros((LANES - nrow, nchp), F32)], axis=0)
    ovt = ovt_ref[...]
    imp_t = sum(_nt(ovt, part) for part in _split3(pcs))
    bias = _select_bias(imp_t, past + tok_l, ntop).T[:nrow].astype(BF16)

    kpos = (past - lw) + lax.broadcasted_iota(jnp.int32, (1, lw), 1)
    p_p, p_n, l = softmax2(jnp.where(kpos > past + tok_r - WINDOW, sw_p, NEG), jnp.where(new_mask, sw_n, NEG))
    o_w = (_nt(p_p, vw_ref[0, 0].astype(BF16)) + _nn(p_n, pad_rows(vwn_ref[0]))) / l

    vt = jnp.concatenate([vbuf[slot, pg] for pg in range(n_pages)], axis=1).astype(BF16)
    s_p = ss_p + _nn(bias, et_ref[...])
    s_n = jnp.where(new_mask, ss_n + _nt(bias, en_ref[...]), NEG)
    p_p, p_n, l = softmax2(s_p, s_n)
    o_s = (_nt(p_p, vt) + _nn(p_n, pad_rows(vsn_ref[0]))) / l

    g = g_ref[0]
    o_ref[0] = g[:, 0:1] * o_c + g[:, 1:2] * o_s + g[:, 2:3] * o_w


def _sattn(pages, qp, g, kc, vc, ks, vs, win_k, win_v, layer, ksn, vsn, kwn, vwn, ovt, gsum, e_t, e_new, past, dec):
    nseq, nrow = qp.shape[:2]
    nchp = kc.shape[1]
    n_pages = pages.shape[1]
    page = ks.shape[2]
    lw = win_k.shape[3]
    ntop = min(SEL_TOPK, -(-(past + dec) // SEL_BLOCK))
    seq3 = lambda n, w: pl.BlockSpec((1, n, w), lambda b, pt: (b, 0, 0))
    fixed = lambda r, c: pl.BlockSpec((r, c), lambda b, pt: (0, 0))
    hbm = pl.BlockSpec(memory_space=pl.ANY)
    win = pl.BlockSpec((1, 1, KV_DIM, lw), lambda b, pt: (layer, b, 0, 0))
    return pl.pallas_call(
        functools.partial(_sattn_body, past=past, dec=dec, ntop=ntop),
        grid_spec=pltpu.PrefetchScalarGridSpec(
            num_scalar_prefetch=1,
            grid=(nseq,),
            in_specs=[seq3(nrow, KV_DIM), seq3(nrow, SUBLANES), seq3(nchp, KV_DIM), seq3(nchp, KV_DIM),
                      hbm, hbm, win, win,
                      seq3(dec, KV_DIM), seq3(dec, KV_DIM), seq3(dec, KV_DIM), seq3(dec, KV_DIM),
                      fixed(NS_PAD, nchp), fixed(nrow, nrow), fixed(NS_PAD, past), fixed(LANES, NS_PAD)],
            out_specs=seq3(nrow, KV_DIM),
            scratch_shapes=[pltpu.VMEM((2, n_pages, KV_DIM, page), F32), pltpu.VMEM((2, n_pages, KV_DIM, page), F32),
                            pltpu.SemaphoreType.DMA((2, 2))]),
        out_shape=jax.ShapeDtypeStruct((nseq, nrow, KV_DIM), F32),
        compiler_params=_params(("arbitrary",)),
    )(pages, qp, g, kc, vc, ks, vs, win_k, win_v, ksn, vsn, kwn, vwn, ovt, gsum, e_t, e_new)


def _mix_body(x_ref, o_ref, u_ref, uprev_ref, cb_ref, sga_ref, sgb_ref, s1_ref, s2_ref,
              cw_ref, wa_ref, wc_ref, wo_ref, y_ref, ubuf, *, tm, slen):
    i = pl.program_id(0)
    u = u_ref[...]
    ubuf[0:SUBLANES, :] = uprev_ref[...]
    ubuf[SUBLANES:, :] = u
    pos = (i * tm + lax.broadcasted_iota(jnp.int32, (tm, 1), 0)) % slen
    prev1 = jnp.where(pos >= 1, ubuf[SUBLANES - 1:SUBLANES - 1 + tm, :], s1_ref[...])
    prev2 = jnp.where(pos >= 2, ubuf[SUBLANES - 2:SUBLANES - 2 + tm, :], s2_ref[...])
    cw = cw_ref[...]
    y_conv = cw[0:1] * prev2 + cw[1:2] * prev1 + cw[2:3] * u
    a_br = _nn(o_ref[...].astype(BF16), wa_ref[...])
    c_br = _nn((cb_ref[...] * y_conv).astype(BF16), wc_ref[...])
    mix = sga_ref[...] * a_br + sgb_ref[...] * c_br
    y_ref[...] = x_ref[...] + _nn(mix.astype(BF16), wo_ref[...])


def _mix(x, o_att, u, cb, sga, sgb, side1, side2, conv_w, wa, wc, wo, tm, slen):
    t, d = x.shape
    row = lambda width: pl.BlockSpec((tm, width), lambda i: (i, 0))
    fixed = lambda shape: pl.BlockSpec(shape, lambda i: (0, 0))
    n_side = side1.shape[0] // tm
    side = pl.BlockSpec((tm, CONV_DIM), lambda i: (i % n_side, 0))
    prev = pl.BlockSpec((SUBLANES, CONV_DIM), lambda i: (jnp.maximum(i * (tm // SUBLANES) - 1, 0), 0))
    return pl.pallas_call(
        functools.partial(_mix_body, tm=tm, slen=slen),
        grid=(t // tm,),
        in_specs=[row(d), row(ATT_DIM), row(CONV_DIM), prev, row(CONV_DIM), row(d), row(d), side, side,
                  fixed((SUBLANES, CONV_DIM)), fixed((ATT_DIM, d)), fixed((CONV_DIM, d)), fixed((d, d))],
        out_specs=row(d),
        out_shape=jax.ShapeDtypeStruct((t, d), F32),
        scratch_shapes=[pltpu.VMEM((tm + SUBLANES, CONV_DIM), F32)],
        compiler_params=_params(("arbitrary",)),
    )(x, o_att, u, u, cb, sga, sgb, side1, side2, conv_w, wa, wc, wo)


def _ffn_body(x_ref, g_ref, win_ref, wout_ref, gf_ref, y_ref, *, d_ff, fc, final):
    x = x_ref[...]
    h = x * lax.rsqrt(jnp.mean(x * x, axis=-1, keepdims=True) + EPS)
    hb = (h * g_ref[...]).astype(BF16)
    acc = x
    for c in range(d_ff // fc):
        gate = _nn(hb, win_ref[:, c * fc:(c + 1) * fc])
        up = _nn(hb, win_ref[:, d_ff + c * fc:d_ff + (c + 1) * fc])
        act = (gate * jax.nn.sigmoid(gate) * up).astype(BF16)
        acc = acc + _nn(act, wout_ref[c * fc:(c + 1) * fc, :])
    if final:
        acc = acc * lax.rsqrt(jnp.mean(acc * acc, axis=-1, keepdims=True) + EPS) * gf_ref[...]
    y_ref[...] = acc


def _ffn(x, g, w_in, w_out, g_final, tm, final):
    t, d = x.shape
    d_ff = w_out.shape[0]
    fc = 256 if d_ff % 256 == 0 else LANES
    row = pl.BlockSpec((tm, d), lambda i: (i, 0))
    fixed = lambda shape: pl.BlockSpec(shape, lambda i: (0, 0))
    return pl.pallas_call(
        functools.partial(_ffn_body, d_ff=d_ff, fc=fc, final=final),
        grid=(t // tm,),
        in_specs=[row, fixed((1, d)), fixed((d, 2 * d_ff)), fixed((d_ff, d)), fixed((1, d))],
        out_specs=row,
        out_shape=jax.ShapeDtypeStruct((t, d), F32),
        compiler_params=_params(("arbitrary",)),
    )(x, g, w_in, w_out, g_final)


def _rope_tables(pos):
    inv = 1.0 / (ROPE_THETA ** (jnp.arange(ROPE_HALF, dtype=F32) * (2.0 / ROPE_DIM)))
    ang = pos.astype(F32)[:, None] * inv[None, :]
    cos, sin = jnp.cos(ang), jnp.sin(ang)
    t = pos.shape[0]
    pad = jnp.zeros((t, HEAD_DIM - ROPE_DIM), F32)
    rc = jnp.concatenate([cos, cos, pad + 1.0], axis=1)
    rs1 = jnp.concatenate([-sin, jnp.zeros_like(sin), pad], axis=1)
    rs2 = jnp.concatenate([jnp.zeros_like(sin), sin, pad], axis=1)
    rep = LANES // HEAD_DIM
    return tuple(jnp.tile(a, (1, rep)) for a in (rc, rs1, rs2))


def _pack_w_in(w):
    wt = w.T
    d = wt.shape[1]
    o_gl = ATT_DIM + 6 * KV_DIM
    gl = wt[o_gl:o_gl + 3 * N_HEADS].reshape(N_KV, 3 * GROUP, d)
    gl = jnp.pad(gl, ((0, 0), (0, LANES - 3 * GROUP), (0, 0))).reshape(N_KV * LANES, d)
    return jnp.concatenate([wt[:o_gl], gl, wt[o_gl + 3 * N_HEADS:]], axis=0).astype(BF16)


def _pack_cmp(pe, w1, w2):
    eye = jnp.eye(N_KV, dtype=F32)
    pe2 = jnp.broadcast_to(pe.reshape(2, CMP_STRIDE, 1, HEAD_DIM), (2, CMP_STRIDE, N_KV, HEAD_DIM)).reshape(2, CHUNK)
    w1r = w1.reshape(2, CMP_STRIDE, HEAD_DIM, CMP_HIDDEN)
    w1p = jnp.einsum('pjdh,kc->pjkdch', w1r, eye).reshape(2, CHUNK, HID2).astype(BF16)
    w2p = jnp.einsum('hd,kc->khcd', w2, eye).reshape(HID2, KV_DIM).astype(BF16)
    return pe2, w1p, w2p


def _overlap_t(nch):
    ci = jnp.arange(nch)[None, :] * CMP_STRIDE
    sj = jnp.arange(NS_PAD)[:, None] * SEL_BLOCK
    return ((ci <= sj + SEL_BLOCK - 1) & (ci + CMP_BLOCK - 1 >= sj)).astype(BF16)


def _dim_major(a):
    lead = a.shape[:-3]
    n = len(lead)
    return a.transpose(*range(n), n + 1, n + 2, n).reshape(*lead, KV_DIM, a.shape[-3])


def _pos_major(a):
    lead = a.shape[:-2]
    n = len(lead)
    return a.reshape(*lead, N_KV, HEAD_DIM, a.shape[-1]).transpose(*range(n), n + 2, n, n + 1)


def _prompt_attention(q, gates, kcvc, kskw, vt, cmpk, cmpv, nb, seq):
    nch = seq // CMP_STRIDE
    abk = _cmp_ab_rows(kcvc, nb * seq, 0, cmpk[0], cmpk[1], seq)
    abv = _cmp_ab_rows(kcvc, nb * seq, 1, cmpv[0], cmpv[1], seq)
    kcmp, vcmp = _cmp_fin(abk, abv, cmpk[2], cmpv[2], nch, nch)
    kcmp = kcmp.reshape(nb, nch, KV_DIM).astype(BF16)
    vcmp_t = vcmp.reshape(nb, nch, KV_DIM).transpose(0, 2, 1).astype(BF16)
    return _pattn(q, gates, kcmp, vcmp_t, kskw, vt, _overlap_t(nch), nb, seq)


def _sample_cache(cmpk, cmpv, pool_k, pool_v, page_table, layer):
    ndb, n_pages = page_table.shape
    n_pool, _, page = pool_k.shape[1:]
    cpp = page // CMP_STRIDE
    nchp = n_pages * cpp
    pp = math.gcd(n_pool, POOL_PAGES)
    abk = _cmp_ab_pages(pool_k, layer, cmpk[0], cmpk[1], pp).reshape(n_pool, cpp, 2 * HID2)[page_table]
    abv = _cmp_ab_pages(pool_v, layer, cmpv[0], cmpv[1], pp).reshape(n_pool, cpp, 2 * HID2)[page_table]
    kcmp, vcmp = _cmp_fin(abk.reshape(ndb * nchp, 2 * HID2), abv.reshape(ndb * nchp, 2 * HID2), cmpk[2], cmpv[2],
                          nchp, math.gcd(ndb, SUBLANES) * nchp)
    return kcmp.reshape(ndb, nchp, KV_DIM), vcmp.reshape(ndb, nchp, KV_DIM)


def _sample_attention(q, gates, new4, cache, slc_k, slc_v, page_table, win_k, win_v, layer, dec):
    kcmp, vcmp = cache
    ndb, nchp = kcmp.shape[:2]
    n_pool, _, page = slc_k.shape[1:]
    past = page_table.shape[1] * page
    pages = (layer * n_pool + page_table).astype(jnp.int32)
    nrow = N_KV * GROUP * dec
    eye_kv = jnp.eye(N_KV, dtype=F32)
    q_s = q.reshape(ndb, dec, N_KV, GROUP, HEAD_DIM).transpose(0, 2, 3, 1, 4)
    qp = (q_s[:, :, :, :, None] * eye_kv[None, :, None, None, :, None]).reshape(ndb, nrow, KV_DIM).astype(BF16)
    g_s = gates.reshape(ndb, dec, N_KV, LANES)[..., :3 * GROUP].reshape(ndb, dec, N_KV, GROUP, 3)
    g_s = jnp.pad(g_s.transpose(0, 2, 3, 1, 4).reshape(ndb, nrow, 3), ((0, 0), (0, 0), (0, SUBLANES - 3)))
    e_t = jax.nn.one_hot(jnp.arange(past) // SEL_BLOCK, NS_PAD, dtype=BF16).T
    e_new = jax.nn.one_hot((past + jnp.arange(LANES)) // SEL_BLOCK, NS_PAD, dtype=BF16)
    rid = jnp.arange(nrow)
    gsum = ((rid[:, None] // (GROUP * dec) == rid[None, :] // (GROUP * dec))
            & (rid[:, None] % dec == rid[None, :] % dec)).astype(BF16)
    ksn, vsn, kwn, vwn = (a.reshape(ndb, dec, KV_DIM) for a in new4)
    o_t = _sattn(pages, qp, g_s, kcmp, vcmp, slc_k.reshape(-1, KV_DIM, page), slc_v.reshape(-1, KV_DIM, page),
                 win_k, win_v, layer,
                 ksn, vsn, kwn, vwn, _overlap_t(nchp), gsum, e_t, e_new, past, dec)
    o_t = o_t.reshape(ndb, N_KV, GROUP, dec, N_KV, HEAD_DIM)
    return jnp.einsum('bkgtcd,kc->btkgd', o_t, eye_kv).reshape(ndb * dec, ATT_DIM)


def kernel(x_prompt, x_sample, cache_cmp_k, cache_cmp_v, cache_slc_k, cache_slc_v, state_win_k, state_win_v,
           state_conv, page_table, w_in, cmp_pe, cmp_w1, cmp_w2, w_att_out, conv_w, w_conv_out, w_o,
           norm_mix, norm_ffn, w_ffn_in, w_ffn_out, norm_final):
    nb, seq, d = x_prompt.shape
    ndb, dec, _ = x_sample.shape
    depth, n_pool, page = cache_cmp_k.shape[:3]
    n_pages = page_table.shape[1]
    past = n_pages * page
    lw = state_win_k.shape[2]
    tp, ts = nb * seq, ndb * dec
    tm_p, tm_s = math.gcd(ROW_TILE, seq), math.gcd(ROW_TILE, ts)
    nchp = past // CMP_STRIDE
    keep = min(WINDOW, seq)
    assert seq % SEL_GROUP == 0 and seq >= WIN_SPAN and seq // SEL_BLOCK <= NS_PAD
    assert dec <= SUBLANES and N_KV * GROUP * dec <= LANES and (N_KV * GROUP * dec) % (2 * SUBLANES) == 0
    assert tm_s % SUBLANES == 0 and tm_s % dec == 0
    assert (past + dec - CMP_BLOCK) // CMP_STRIDE + 1 == nchp - 1
    assert -(-(past + dec) // SEL_BLOCK) <= NS_PAD and past >= WINDOW and lw == WINDOW

    rope_p = _rope_tables(jnp.arange(seq, dtype=jnp.int32))
    rope_s = _rope_tables(jnp.tile(past + jnp.arange(dec, dtype=jnp.int32), ndb))
    pool_k, pool_v, slc_k, slc_v, win_k, win_v = (
        _dim_major(a) for a in (cache_cmp_k, cache_cmp_v, cache_slc_k, cache_slc_v, state_win_k, state_win_v))
    no_side = jnp.zeros((tm_p, CONV_DIM), F32)
    zero = jnp.zeros((ndb, dec - 1, CONV_DIM), F32)

    cmp_w = [[_pack_cmp(cmp_pe[l, a], cmp_w1[l, a], cmp_w2[l, a]) for a in range(2)] for l in range(depth)]
    caches = [_sample_cache(*cmp_w[l], pool_k, pool_v, page_table, l) for l in range(depth)]

    xp, xs = x_prompt.reshape(tp, d), x_sample.reshape(ts, d)
    st_p = [[] for _ in range(7)]
    st_s = [[] for _ in range(7)]
    for l in range(depth):
        w_l = _pack_w_in(w_in[l])
        g_l = norm_mix[l][None, :]
        cmpk, cmpv = cmp_w[l]
        cw = jnp.pad(conv_w[l], ((0, SUBLANES - CONV_WIDTH), (0, 0)))
        w_mix = (w_att_out[l].astype(BF16), w_conv_out[l].astype(BF16), w_o[l].astype(BF16))
        w_ffn = (norm_ffn[l][None, :], w_ffn_in[l].astype(BF16), w_ffn_out[l].astype(BF16), norm_final[None, :])
        final = l == depth - 1

        q, kcvc, kskw, vt, stp, gates, u, cb, sga, sgb = _inproj(xp, g_l, w_l, rope_p, nb, tm_p)
        o_att = _prompt_attention(q, gates, kcvc, kskw, vt, cmpk, cmpv, nb, seq)
        xp = _mix(xp, o_att, u, cb, sga, sgb, no_side, no_side, cw, *w_mix, tm_p, seq)
        xp = _ffn(xp, *w_ffn, tm_p, final)
        for j in range(4):
            st_p[j].append(stp[j])
        st_p[4].append(stp[4][:, :, seq - keep:])
        st_p[5].append(stp[5][:, :, seq - keep:])
        st_p[6].append(jnp.stack([u[(b + 1) * seq - (CONV_WIDTH - 1):(b + 1) * seq] for b in range(nb)]))

        q, kcvc, kskw, vt, sts, gates, u, cb, sga, sgb = _inproj(xs, g_l, w_l, rope_s, 1, tm_s)
        new_t = sts[:, 0].reshape(6, KV_DIM, ndb, dec).transpose(0, 2, 1, 3)
        new_rows = lambda j: sts[j, 0].T
        o_att = _sample_attention(q, gates, [kskw[:, :KV_DIM], new_rows(3), kskw[:, KV_DIM:], new_rows(5)],
                                  caches[l], slc_k, slc_v, page_table, win_k, win_v, l, dec)
        cbuf = state_conv[l]
        side1 = jnp.concatenate([cbuf[:, 1:2], zero], axis=1).reshape(ts, CONV_DIM)
        side2 = jnp.concatenate([cbuf[:, 0:2], zero[:, 1:]], axis=1).reshape(ts, CONV_DIM)
        xs = _mix(xs, o_att, u, cb, sga, sgb, side1, side2, cw, *w_mix, tm_s, dec)
        xs = _ffn(xs, *w_ffn, tm_s, final)
        for j in range(6):
            st_s[j].append(new_t[j])
        st_s[6].append(jnp.concatenate([cbuf, u.reshape(ndb, dec, CONV_DIM)], axis=1)[:, -(CONV_WIDTH - 1):])

    kv_p = [_pos_major(jnp.stack(a)) for a in st_p[:6]]
    kv_s = [jnp.stack(a) for a in st_s[:6]]
    kv_s[4] = jnp.concatenate([win_k[:, :, :, dec:], kv_s[4]], axis=-1)
    kv_s[5] = jnp.concatenate([win_v[:, :, :, dec:], kv_s[5]], axis=-1)
    kv_s = [_pos_major(a) for a in kv_s]
    return (xp.reshape(nb, seq, d), xs.reshape(ndb, dec, d), *kv_p, jnp.stack(st_p[6]),
            *kv_s, jnp.stack(st_s[6]))
```

```python
import functools
import math

import jax
import jax.numpy as jnp
from jax import lax
from jax.experimental import pallas as pl
from jax.experimental.pallas import tpu as pltpu

N_HEADS = 8
HEAD_DIM = 64
N_KV = 2
GROUP = N_HEADS // N_KV
ATT_DIM = N_HEADS * HEAD_DIM
KV_DIM = N_KV * HEAD_DIM
ROPE_DIM = HEAD_DIM // 4
ROPE_HALF = ROPE_DIM // 2
ROPE_THETA = 500000.0
CMP_BLOCK = 32
CMP_STRIDE = 16
CMP_HIDDEN = 2 * HEAD_DIM
SEL_BLOCK = 64
SEL_TOPK = 16
N_LOCAL_BLOCKS = 2
WINDOW = 512
CONV_DIM = 512
CONV_WIDTH = 3
EPS = 1e-6
NEG = -1e30
FORCE_SCORE = 1e4

LANES = 128
SUBLANES = 8
Q_TILE = 256
SEL_CHUNK = 256
SEL_GROUP = 4 * SEL_CHUNK
LOG2E = 1.4426950408889634
VT_ROWS = HEAD_DIM + 16
WIN_SPAN = WINDOW + Q_TILE
ROW_TILE = 512
POOL_PAGES = 64
NS_PAD = LANES
VMEM_LIMIT = 56 * 1024 * 1024

C_Q = 0
C_KV = C_Q + ATT_DIM
C_GL = C_KV + 6 * KV_DIM
C_CU = C_GL + N_KV * LANES
C_CB = C_CU + CONV_DIM
C_CC = C_CB + CONV_DIM
F32 = jnp.float32
BF16 = jnp.bfloat16


def _nt(a, b):
    return lax.dot_general(a, b, (((1,), (1,)), ((), ())), preferred_element_type=F32)


def _nn(a, b):
    return jnp.dot(a, b, preferred_element_type=F32)


def _split3(x):
    hi = x.astype(BF16)
    r1 = x - hi.astype(F32)
    mid = r1.astype(BF16)
    lo = (r1 - mid.astype(F32)).astype(BF16)
    return hi, mid, lo


def _params(sem):
    return pltpu.CompilerParams(dimension_semantics=sem, vmem_limit_bytes=VMEM_LIMIT)


def _inproj_body(x_ref, g_ref, w_ref, rc_ref, rs1_ref, rs2_ref,
                 q_ref, kcvc_ref, kskw_ref, vt_ref, st_ref, gate_ref, u_ref, cb_ref, sga_ref, sgb_ref, *, d_model):
    x = x_ref[...]
    h = x * lax.rsqrt(jnp.mean(x * x, axis=-1, keepdims=True) + EPS)
    hb = (h * g_ref[...]).astype(BF16)
    rc, rs1, rs2 = rc_ref[...], rs1_ref[...], rs2_ref[...]

    def mm(lo, width):
        return _nt(hb, w_ref[lo:lo + width, :])

    def rope(z):
        return z * rc + pltpu.roll(z, LANES - ROPE_HALF, 1) * rs1 + pltpu.roll(z, ROPE_HALF, 1) * rs2

    for j in range(ATT_DIM // 256):
        z = mm(C_Q + 256 * j, 256)
        for t in range(2):
            q_ref[:, 256 * j + LANES * t:256 * j + LANES * (t + 1)] = (
                rope(z[:, LANES * t:LANES * (t + 1)]) * (HEAD_DIM ** -0.5))
    for j in range(3):
        z = mm(C_KV + 256 * j, 256)
        k, v = rope(z[:, :LANES]), z[:, LANES:]
        kt, vt = k.T, v.T
        st_ref[2 * j, 0] = kt
        st_ref[2 * j + 1, 0] = vt
        if j == 0:
            kcvc_ref[:, :LANES] = k
            kcvc_ref[:, LANES:] = v
        else:
            kskw_ref[:, LANES * (j - 1):LANES * j] = k.astype(BF16)
            ones_row = (lax.broadcasted_iota(jnp.int32, (VT_ROWS - HEAD_DIM, vt.shape[1]), 0) == 0).astype(BF16)
            for kvh in range(N_KV):
                vt_ref[j - 1, 0, VT_ROWS * kvh:VT_ROWS * kvh + HEAD_DIM] = vt[HEAD_DIM * kvh:HEAD_DIM * (kvh + 1)].astype(BF16)
                vt_ref[j - 1, 0, VT_ROWS * kvh + HEAD_DIM:VT_ROWS * (kvh + 1)] = ones_row
    gate_ref[...] = jax.nn.sigmoid(mm(C_GL, N_KV * LANES))
    for j in range(CONV_DIM // 256):
        u_ref[:, 256 * j:256 * (j + 1)] = mm(C_CC + 256 * j, 256) * mm(C_CU + 256 * j, 256)
        cb_ref[:, 256 * j:256 * (j + 1)] = mm(C_CB + 256 * j, 256)
    c_ga = C_CC + CONV_DIM
    c_gb = c_ga + d_model
    for j in range(d_model // 256):
        sga_ref[:, 256 * j:256 * (j + 1)] = jax.nn.sigmoid(mm(c_ga + 256 * j, 256))
        sgb_ref[:, 256 * j:256 * (j + 1)] = jax.nn.sigmoid(mm(c_gb + 256 * j, 256))


def _layer_block(layer, shape):
    return pl.BlockSpec((None,) + tuple(shape), lambda i: (layer,) + (0,) * len(shape))


def _inproj(x, g, w, layer, rope, nseq, tm):
    t, d = x.shape
    slen = t // nseq
    tps = slen // tm
    nrows = w.shape[1]
    row = lambda width: pl.BlockSpec((tm, width), lambda i: (i, 0))
    fixed = lambda shape: pl.BlockSpec(shape, lambda i: (0, 0))
    tab = pl.BlockSpec((tm, LANES), lambda i: (i % tps, 0))
    tr = lambda n, rows: pl.BlockSpec((n, 1, rows, tm), lambda i: (0, i // tps, 0, i % tps))
    sds = jax.ShapeDtypeStruct
    return pl.pallas_call(
        functools.partial(_inproj_body, d_model=d),
        grid=(t // tm,),
        in_specs=[row(d), fixed((1, d)), _layer_block(layer, (nrows, d)), tab, tab, tab],
        out_specs=[row(ATT_DIM), row(2 * KV_DIM), row(2 * KV_DIM), tr(2, N_KV * VT_ROWS), tr(6, KV_DIM), row(N_KV * LANES),
                   row(CONV_DIM), row(CONV_DIM), row(d), row(d)],
        out_shape=[sds((t, ATT_DIM), F32), sds((t, 2 * KV_DIM), F32), sds((t, 2 * KV_DIM), BF16),
                   sds((2, nseq, N_KV * VT_ROWS, slen), BF16), sds((6, nseq, KV_DIM, slen), F32), sds((t, N_KV * LANES), F32),
                   sds((t, CONV_DIM), F32), sds((t, CONV_DIM), F32), sds((t, d), F32), sds((t, d), F32)],
        compiler_params=_params(("arbitrary",)),
    )(x, g, w, *rope)


CHUNK = CMP_STRIDE * KV_DIM
HID2 = N_KV * CMP_HIDDEN


def _cmp_partials(src, pe_ref, w_ref, ab_ref):
    n = src.shape[0] // CMP_STRIDE
    acc_a = jnp.zeros((n, HID2), F32)
    acc_b = jnp.zeros((n, HID2), F32)
    for jp in range(CMP_STRIDE // 2):
        x = jnp.concatenate([src[pl.ds(2 * jp, n, stride=CMP_STRIDE), :],
                             src[pl.ds(2 * jp + 1, n, stride=CMP_STRIDE), :]], axis=1)
        cols = slice(2 * KV_DIM * jp, 2 * KV_DIM * (jp + 1))
        acc_a = acc_a + _nn((x + pe_ref[0:1, cols]).astype(BF16), w_ref[0, cols, :])
        acc_b = acc_b + _nn((x + pe_ref[1:2, cols]).astype(BF16), w_ref[1, cols, :])
    ab_ref[:, :HID2] = acc_a
    ab_ref[:, HID2:] = acc_b


def _cmp_ab_rows_body(x_ref, pe_ref, w_ref, ab_ref):
    _cmp_partials(x_ref, pe_ref, w_ref, ab_ref)


def _cmp_ab_rows(x, rows, col, pe, w, tr):
    return pl.pallas_call(
        _cmp_ab_rows_body,
        grid=(rows // tr,),
        in_specs=[pl.BlockSpec((tr, KV_DIM), lambda i: (i, col)),
                  pl.BlockSpec((2, CHUNK), lambda i: (0, 0)),
                  pl.BlockSpec((2, CHUNK, HID2), lambda i: (0, 0, 0))],
        out_specs=pl.BlockSpec((tr // CMP_STRIDE, 2 * HID2), lambda i: (i, 0)),
        out_shape=jax.ShapeDtypeStruct((rows // CMP_STRIDE, 2 * HID2), F32),
        compiler_params=_params(("arbitrary",)),
    )(x, pe, w)


def _cmp_ab_pages_body(x_ref, pe_ref, w_ref, ab_ref, rows_scr):
    page = x_ref.shape[3]
    for p in range(x_ref.shape[1]):
        rows_scr[page * p:page * (p + 1), :] = x_ref[0, p].T
    _cmp_partials(rows_scr, pe_ref, w_ref, ab_ref)


def _cmp_ab_pages(pool, layer, pe, w, pp):
    _, n_pool, _, page = pool.shape
    cpp = page // CMP_STRIDE
    return pl.pallas_call(
        _cmp_ab_pages_body,
        grid=(n_pool // pp,),
        in_specs=[pl.BlockSpec((1, pp, KV_DIM, page), lambda i: (layer, i, 0, 0)),
                  pl.BlockSpec((2, CHUNK), lambda i: (0, 0)),
                  pl.BlockSpec((2, CHUNK, HID2), lambda i: (0, 0, 0))],
        out_specs=pl.BlockSpec((pp * cpp, 2 * HID2), lambda i: (i, 0)),
        out_shape=jax.ShapeDtypeStruct((n_pool * cpp, 2 * HID2), F32),
        scratch_shapes=[pltpu.VMEM((pp * page, KV_DIM), F32)],
        compiler_params=_params(("arbitrary",)),
    )(pool, pe, w)


def _cmp_fin_body(abk_ref, abv_ref, w2k_ref, w2v_ref, ok_ref, ov_ref, *, nch):
    rows = abk_ref.shape[0]
    valid = lax.broadcasted_iota(jnp.int32, (rows, 1), 0) % nch < nch - 1

    def fin(ab_ref, w2_ref, o_ref):
        ab = ab_ref[...]
        hid = ab[:, :HID2] + pltpu.roll(ab[:, HID2:], rows - 1, 0)
        act = hid * jax.nn.sigmoid(hid)
        o_ref[...] = jnp.where(valid, _nn(act.astype(BF16), w2_ref[...]), 0.0)

    fin(abk_ref, w2k_ref, ok_ref)
    fin(abv_ref, w2v_ref, ov_ref)


def _cmp_fin(abk, abv, w2k, w2v, nch, tr):
    rows = abk.shape[0]
    ab_spec = pl.BlockSpec((tr, 2 * HID2), lambda i: (i, 0))
    w_spec = pl.BlockSpec((HID2, KV_DIM), lambda i: (0, 0))
    o_spec = pl.BlockSpec((tr, KV_DIM), lambda i: (i, 0))
    return pl.pallas_call(
        functools.partial(_cmp_fin_body, nch=nch),
        grid=(rows // tr,),
        in_specs=[ab_spec, ab_spec, w_spec, w_spec],
        out_specs=[o_spec, o_spec],
        out_shape=[jax.ShapeDtypeStruct((rows, KV_DIM), F32)] * 2,
        compiler_params=_params(("arbitrary",)),
    )(abk, abv, w2k, w2v)


def _select_bias(imp_t, qpos, ntop):
    shape = imp_t.shape
    s_io = lax.broadcasted_iota(jnp.int32, shape, 0)
    cur = qpos // SEL_BLOCK
    avail = s_io <= cur
    forced = (s_io == 0) | (avail & (s_io > cur - N_LOCAL_BLOCKS))
    val = jnp.where(avail, jnp.where(forced, FORCE_SCORE, imp_t), NEG)

    def pick_one(_, carry):
        val, sel = carry
        m = jnp.max(val, axis=0, keepdims=True)
        idx = jnp.min(jnp.where(val == m, s_io, NS_PAD), axis=0, keepdims=True)
        pick = s_io == idx
        return jnp.where(pick, -jnp.inf, val), jnp.where(pick, 1.0, sel)

    _, sel = lax.fori_loop(0, ntop, pick_one, (val, jnp.zeros(shape, F32)), unroll=True)
    return jnp.where((sel > 0.0) & avail, 0.0, NEG)


def _pattn_body(q_ref, g_ref, kc_ref, vct_ref, ks_ref, kw_ref, vst_ref, vwt_ref, ovt_ref, o_ref, bias_scr, *, seq):
    kv = pl.program_id(1)
    i = pl.program_id(2)
    nch = seq // CMP_STRIDE
    cols = GROUP * Q_TILE
    q = q_ref[...]
    qt = jnp.concatenate([q[:, :LANES].T, q[:, LANES:].T], axis=0)
    qt4 = jnp.concatenate([qt[HEAD_DIM * h:HEAD_DIM * (h + 1)] for h in range(GROUP)], axis=1)
    row_kv = lax.broadcasted_iota(jnp.int32, (KV_DIM, 1), 0) // HEAD_DIM
    qt_pad = jnp.where(row_kv == kv, jnp.concatenate([qt4] * N_KV, axis=0) * LOG2E, 0.0).astype(BF16)
    qpos1 = i * Q_TILE + lax.broadcasted_iota(jnp.int32, (1, Q_TILE), 1)
    qpos = jnp.concatenate([qpos1] * GROUP, axis=1)

    n_below = (i * Q_TILE) // SEL_GROUP
    w0 = pl.multiple_of(jnp.maximum(i * Q_TILE - WINDOW, 0), Q_TILE)
    s = _nn(kc_ref[0], qt_pad)
    s_win = _nn(kw_ref[pl.ds(w0, WIN_SPAN), :], qt_pad)
    s_diag = _nn(ks_ref[pl.ds(pl.multiple_of(n_below * SEL_GROUP, SEL_GROUP), SEL_GROUP), :], qt_pad)

    cmask = lax.broadcasted_iota(jnp.int32, (nch, 1), 0) * CMP_STRIDE + (CMP_BLOCK - 1) <= qpos
    s = jnp.where(cmask, s, NEG)
    p = jnp.where(cmask, jnp.exp2(s - jnp.max(s, axis=0, keepdims=True)), 0.0)
    l = jnp.sum(p, axis=0, keepdims=True)
    pc = p * jnp.where(l > 0.0, 1.0 / l, 0.0)
    o_c = _nn(vct_ref[0], pc.astype(BF16))

    pcs = pc[:, :Q_TILE]
    for h in range(1, GROUP):
        pcs = pcs + pc[:, h * Q_TILE:(h + 1) * Q_TILE]
    ovt = ovt_ref[...]
    imp_t = sum(_nn(ovt, part) for part in _split3(pcs))
    bias_scr[...] = _select_bias(imp_t, qpos1, min(SEL_TOPK, seq // SEL_BLOCK))

    kpos = w0 + lax.broadcasted_iota(jnp.int32, (WIN_SPAN, 1), 0)
    s = jnp.where((kpos <= qpos) & (kpos > qpos - WINDOW), s_win, NEG)
    p = jnp.exp2(s - jnp.max(s, axis=0, keepdims=True))
    acc_w = _nn(vwt_ref[0, 0, :, pl.ds(w0, WIN_SPAN)], p.astype(BF16))
    o_w = acc_w[:HEAD_DIM] / acc_w[HEAD_DIM:HEAD_DIM + 1]

    def group(gj, nkeys, carry, s_all, causal):
        m, acc = carry
        g0 = pl.multiple_of(gj * nkeys, nkeys)
        if s_all is None:
            s_all = _nn(ks_ref[pl.ds(g0, nkeys), :], qt_pad)
        for c in range(nkeys // SEL_CHUNK):
            parts = []
            for r in range(SEL_CHUNK // SEL_BLOCK):
                row = SEL_CHUNK * c + SEL_BLOCK * r
                b_row = bias_scr[pl.ds(gj * (nkeys // SEL_BLOCK) + row // SEL_BLOCK, 1), :]
                parts.append(s_all[row:row + SEL_BLOCK] + jnp.concatenate([b_row] * GROUP, axis=1))
            s = jnp.concatenate(parts, axis=0)
            k0 = pl.multiple_of(g0 + SEL_CHUNK * c, SEL_CHUNK)
            if causal:
                kpos = k0 + lax.broadcasted_iota(jnp.int32, (SEL_CHUNK, 1), 0)
                s = jnp.where(kpos <= qpos, s, NEG)
            m_new = jnp.maximum(m, jnp.max(s, axis=0, keepdims=True))
            alpha = jnp.exp2(m - m_new)
            p = jnp.exp2(s - m_new)
            acc = alpha * acc + _nn(vst_ref[0, 0, :, pl.ds(k0, SEL_CHUNK)], p.astype(BF16))
            m = m_new
        return m, acc

    init = (jnp.full((1, cols), -3e38, F32), jnp.zeros((VT_ROWS, cols), F32))
    carry = group(n_below, SEL_GROUP, init, s_diag, True)
    carry = lax.fori_loop(0, n_below // 2, lambda gj, c: group(gj, 2 * SEL_GROUP, c, None, False), carry)
    _, acc_s = lax.cond(n_below % 2 == 1, lambda c: group(n_below - 1, SEL_GROUP, c, None, False), lambda c: c, carry)
    o_s = acc_s[:HEAD_DIM] / acc_s[HEAD_DIM:HEAD_DIM + 1]

    gt = g_ref[...].T
    outs = []
    for h in range(GROUP):
        c = slice(h * Q_TILE, (h + 1) * Q_TILE)
        outs.append(gt[3 * h:3 * h + 1] * o_c[:, c] + gt[3 * h + 1:3 * h + 2] * o_s[:, c]
                    + gt[3 * h + 2:3 * h + 3] * o_w[:, c])
    for t in range(GROUP // 2):
        o_ref[:, LANES * t:LANES * (t + 1)] = jnp.concatenate(outs[2 * t:2 * t + 2], axis=0).T


def _pattn(q, gates, kcmp, vcmp_t, kskw, vt, ovt, nb, seq):
    nq = seq // Q_TILE
    nch = seq // CMP_STRIDE
    qspec = pl.BlockSpec((Q_TILE, GROUP * HEAD_DIM), lambda b, k, i: (b * nq + i, k))
    gspec = pl.BlockSpec((Q_TILE, LANES), lambda b, k, i: (b * nq + i, k))
    kspec = lambda which: pl.BlockSpec((seq, KV_DIM), lambda b, k, i: (b, which))
    vspec = lambda which: pl.BlockSpec((1, 1, VT_ROWS, seq), lambda b, k, i: (which, b, k, 0))
    return pl.pallas_call(
        functools.partial(_pattn_body, seq=seq),
        grid=(nb, N_KV, nq),
        in_specs=[qspec, gspec,
                  pl.BlockSpec((1, nch, KV_DIM), lambda b, k, i: (b, 0, 0)),
                  pl.BlockSpec((1, HEAD_DIM, nch), lambda b, k, i: (b, k, 0)),
                  kspec(0), kspec(1), vspec(0), vspec(1),
                  pl.BlockSpec((NS_PAD, nch), lambda b, k, i: (0, 0))],
        out_specs=qspec,
        out_shape=jax.ShapeDtypeStruct((nb * seq, ATT_DIM), F32),
        scratch_shapes=[pltpu.VMEM((NS_PAD, Q_TILE), F32)],
        compiler_params=_params(("arbitrary", "arbitrary", "arbitrary")),
    )(q, gates, kcmp, vcmp_t, kskw, kskw, vt, vt, ovt)


def _sattn_body(pages_ref, qp_ref, g_ref, kc_ref, vc_ref, ks_hbm, vs_hbm, kw_ref, vw_ref,
                ksn_ref, vsn_ref, kwn_ref, vwn_ref, ovt_ref, gsum_ref, et_ref, en_ref, o_ref,
                kbuf, vbuf, sems, *, past, dec, ntop):
    b = pl.program_id(0)
    nchp = kc_ref.shape[1]
    n_pages = kbuf.shape[1]
    lw = kw_ref.shape[3]

    def page_copies(seq_idx, slot):
        return [pltpu.make_async_copy(src.at[pages_ref[seq_idx, pg]], buf.at[slot, pg], sems.at[a, slot])
                for a, (src, buf) in enumerate(((ks_hbm, kbuf), (vs_hbm, vbuf))) for pg in range(n_pages)]

    slot = b % 2

    @pl.when(b == 0)
    def _():
        for cp in page_copies(0, 0):
            cp.start()

    @pl.when(b + 1 < pl.num_programs(0))
    def _():
        for cp in page_copies(b + 1, 1 - slot):
            cp.start()

    qp = qp_ref[0]
    nrow = qp.shape[0]
    tok_r = lax.broadcasted_iota(jnp.int32, (nrow, 1), 0) % dec
    tok_l = lax.broadcasted_iota(jnp.int32, (1, LANES), 1) % dec
    new_j = lax.broadcasted_iota(jnp.int32, (1, LANES), 1)
    new_mask = (new_j < dec) & (new_j <= tok_r)

    def pad_rows(x):
        return jnp.concatenate([x.astype(F32), jnp.zeros((LANES - dec, KV_DIM), F32)], axis=0).astype(BF16)

    def softmax2(s_p, s_n):
        m = jnp.maximum(jnp.max(s_p, axis=-1, keepdims=True), jnp.max(s_n, axis=-1, keepdims=True))
        p_p, p_n = jnp.exp(s_p - m), jnp.exp(s_n - m)
        l = jnp.sum(p_p, axis=-1, keepdims=True) + jnp.sum(p_n, axis=-1, keepdims=True)
        return p_p.astype(BF16), p_n.astype(BF16), l

    s = _nn(qp, kc_ref[0].T.astype(BF16))
    sw_p = _nn(qp, kw_ref[0, 0].astype(BF16))
    sw_n = _nt(qp, pad_rows(kwn_ref[0]))
    for cp in page_copies(b, slot):
        cp.wait()
    kt = jnp.concatenate([kbuf[slot, pg] for pg in range(n_pages)], axis=1).astype(BF16)
    ss_p = _nn(qp, kt)
    ss_n = _nt(qp, pad_rows(ksn_ref[0]))

    cmask = lax.broadcasted_iota(jnp.int32, (1, nchp), 1) < nchp - 1
    s = jnp.where(cmask, s, NEG)
    p = jnp.where(cmask, jnp.exp(s - jnp.max(s, axis=-1, keepdims=True)), 0.0)
    pc = p / jnp.sum(p, axis=-1, keepdims=True)
    o_c = _nn(pc.astype(BF16), vc_ref[0].astype(BF16))

    gsum = gsum_ref[...]
    pcs = sum(_nn(gsum, part) for part in _split3(pc))
    pcs = jnp.concatenate([pcs, jnp.zeros((LANES - nrow, nchp), F32)], axis=0)
    ovt = ovt_ref[...]
    imp_t = sum(_nt(ovt, part) for part in _split3(pcs))
    bias = _select_bias(imp_t, past + tok_l, ntop).T[:nrow].astype(BF16)

    kpos = (past - lw) + lax.broadcasted_iota(jnp.int32, (1, lw), 1)
    p_p, p_n, l = softmax2(jnp.where(kpos > past + tok_r - WINDOW, sw_p, NEG), jnp.where(new_mask, sw_n, NEG))
    o_w = (_nt(p_p, vw_ref[0, 0].astype(BF16)) + _nn(p_n, pad_rows(vwn_ref[0]))) / l

    vt = jnp.concatenate([vbuf[slot, pg] for pg in range(n_pages)], axis=1).astype(BF16)
    s_p = ss_p + _nn(bias, et_ref[...])
    s_n = jnp.where(new_mask, ss_n + _nt(bias, en_ref[...]), NEG)
    p_p, p_n, l = softmax2(s_p, s_n)
    o_s = (_nt(p_p, vt) + _nn(p_n, pad_rows(vsn_ref[0]))) / l

    g = g_ref[0]
    o_ref[0] = g[:, 0:1] * o_c + g[:, 1:2] * o_s + g[:, 2:3] * o_w


def _sattn(pages, qp, g, kc, vc, ks, vs, win_k, win_v, layer, ksn, vsn, kwn, vwn, ovt, gsum, e_t, e_new, past, dec):
    nseq, nrow = qp.shape[:2]
    nchp = kc.shape[1]
    n_pages = pages.shape[1]
    page = ks.shape[2]
    lw = win_k.shape[3]
    ntop = min(SEL_TOPK, -(-(past + dec) // SEL_BLOCK))
    seq3 = lambda n, w: pl.BlockSpec((1, n, w), lambda b, pt: (b, 0, 0))
    fixed = lambda r, c: pl.BlockSpec((r, c), lambda b, pt: (0, 0))
    hbm = pl.BlockSpec(memory_space=pl.ANY)
    win = pl.BlockSpec((1, 1, KV_DIM, lw), lambda b, pt: (layer, b, 0, 0))
    return pl.pallas_call(
        functools.partial(_sattn_body, past=past, dec=dec, ntop=ntop),
        grid_spec=pltpu.PrefetchScalarGridSpec(
            num_scalar_prefetch=1,
            grid=(nseq,),
            in_specs=[seq3(nrow, KV_DIM), seq3(nrow, SUBLANES), seq3(nchp, KV_DIM), seq3(nchp, KV_DIM),
                      hbm, hbm, win, win,
                      seq3(dec, KV_DIM), seq3(dec, KV_DIM), seq3(dec, KV_DIM), seq3(dec, KV_DIM),
                      fixed(NS_PAD, nchp), fixed(nrow, nrow), fixed(NS_PAD, past), fixed(LANES, NS_PAD)],
            out_specs=seq3(nrow, KV_DIM),
            scratch_shapes=[pltpu.VMEM((2, n_pages, KV_DIM, page), F32), pltpu.VMEM((2, n_pages, KV_DIM, page), F32),
                            pltpu.SemaphoreType.DMA((2, 2))]),
        out_shape=jax.ShapeDtypeStruct((nseq, nrow, KV_DIM), F32),
        compiler_params=_params(("arbitrary",)),
    )(pages, qp, g, kc, vc, ks, vs, win_k, win_v, ksn, vsn, kwn, vwn, ovt, gsum, e_t, e_new)


def _mix_body(x_ref, o_ref, u_ref, uprev_ref, cb_ref, sga_ref, sgb_ref, s1_ref, s2_ref,
              cw_ref, wa_ref, wc_ref, wo_ref, y_ref, ubuf, *, tm, slen):
    i = pl.program_id(0)
    u = u_ref[...]
    ubuf[0:SUBLANES, :] = uprev_ref[...]
    ubuf[SUBLANES:, :] = u
    pos = (i * tm + lax.broadcasted_iota(jnp.int32, (tm, 1), 0)) % slen
    prev1 = jnp.where(pos >= 1, ubuf[SUBLANES - 1:SUBLANES - 1 + tm, :], s1_ref[...])
    prev2 = jnp.where(pos >= 2, ubuf[SUBLANES - 2:SUBLANES - 2 + tm, :], s2_ref[...])
    cw = cw_ref[...]
    y_conv = cw[0:1] * prev2 + cw[1:2] * prev1 + cw[2:3] * u
    a_br = _nn(o_ref[...].astype(BF16), wa_ref[...])
    c_br = _nn((cb_ref[...] * y_conv).astype(BF16), wc_ref[...])
    mix = sga_ref[...] * a_br + sgb_ref[...] * c_br
    y_ref[...] = x_ref[...] + _nn(mix.astype(BF16), wo_ref[...])


def _mix(x, o_att, u, cb, sga, sgb, side1, side2, conv_w, wa, wc, wo, layer, tm, slen):
    t, d = x.shape
    row = lambda width: pl.BlockSpec((tm, width), lambda i: (i, 0))
    fixed = lambda shape: pl.BlockSpec(shape, lambda i: (0, 0))
    n_side = side1.shape[0] // tm
    side = pl.BlockSpec((tm, CONV_DIM), lambda i: (i % n_side, 0))
    prev = pl.BlockSpec((SUBLANES, CONV_DIM), lambda i: (jnp.maximum(i * (tm // SUBLANES) - 1, 0), 0))
    return pl.pallas_call(
        functools.partial(_mix_body, tm=tm, slen=slen),
        grid=(t // tm,),
        in_specs=[row(d), row(ATT_DIM), row(CONV_DIM), prev, row(CONV_DIM), row(d), row(d), side, side,
                  fixed((SUBLANES, CONV_DIM)), _layer_block(layer, (ATT_DIM, d)), _layer_block(layer, (CONV_DIM, d)),
                  _layer_block(layer, (d, d))],
        out_specs=row(d),
        out_shape=jax.ShapeDtypeStruct((t, d), F32),
        scratch_shapes=[pltpu.VMEM((tm + SUBLANES, CONV_DIM), F32)],
        compiler_params=_params(("arbitrary",)),
    )(x, o_att, u, u, cb, sga, sgb, side1, side2, conv_w, wa, wc, wo)


def _ffn_body(x_ref, g_ref, win_ref, wout_ref, gf_ref, y_ref, *, d_ff, fc, final):
    x = x_ref[...]
    h = x * lax.rsqrt(jnp.mean(x * x, axis=-1, keepdims=True) + EPS)
    hb = (h * g_ref[...]).astype(BF16)
    acc = x
    for c in range(d_ff // fc):
        gate = _nn(hb, win_ref[:, c * fc:(c + 1) * fc])
        up = _nn(hb, win_ref[:, d_ff + c * fc:d_ff + (c + 1) * fc])
        act = (gate * jax.nn.sigmoid(gate) * up).astype(BF16)
        acc = acc + _nn(act, wout_ref[c * fc:(c + 1) * fc, :])
    if final:
        acc = acc * lax.rsqrt(jnp.mean(acc * acc, axis=-1, keepdims=True) + EPS) * gf_ref[...]
    y_ref[...] = acc


def _ffn(x, g, w_in, w_out, g_final, layer, tm, final):
    t, d = x.shape
    d_ff = w_out.shape[1]
    fc = 256 if d_ff % 256 == 0 else LANES
    row = pl.BlockSpec((tm, d), lambda i: (i, 0))
    fixed = lambda shape: pl.BlockSpec(shape, lambda i: (0, 0))
    return pl.pallas_call(
        functools.partial(_ffn_body, d_ff=d_ff, fc=fc, final=final),
        grid=(t // tm,),
        in_specs=[row, fixed((1, d)), _layer_block(layer, (d, 2 * d_ff)), _layer_block(layer, (d_ff, d)), fixed((1, d))],
        out_specs=row,
        out_shape=jax.ShapeDtypeStruct((t, d), F32),
        compiler_params=_params(("arbitrary",)),
    )(x, g, w_in, w_out, g_final)


def _rope_tables(pos):
    inv = 1.0 / (ROPE_THETA ** (jnp.arange(ROPE_HALF, dtype=F32) * (2.0 / ROPE_DIM)))
    ang = pos.astype(F32)[:, None] * inv[None, :]
    cos, sin = jnp.cos(ang), jnp.sin(ang)
    t = pos.shape[0]
    pad = jnp.zeros((t, HEAD_DIM - ROPE_DIM), F32)
    rc = jnp.concatenate([cos, cos, pad + 1.0], axis=1)
    rs1 = jnp.concatenate([-sin, jnp.zeros_like(sin), pad], axis=1)
    rs2 = jnp.concatenate([jnp.zeros_like(sin), sin, pad], axis=1)
    rep = LANES // HEAD_DIM
    return tuple(jnp.tile(a, (1, rep)) for a in (rc, rs1, rs2))


def _pack_w_in(w):
    wt = w.transpose(0, 2, 1)
    depth, _, d = wt.shape
    o_gl = ATT_DIM + 6 * KV_DIM
    gl = wt[:, o_gl:o_gl + 3 * N_HEADS].reshape(depth, N_KV, 3 * GROUP, d)
    gl = jnp.pad(gl, ((0, 0), (0, 0), (0, LANES - 3 * GROUP), (0, 0))).reshape(depth, N_KV * LANES, d)
    return jnp.concatenate([wt[:, :o_gl], gl, wt[:, o_gl + 3 * N_HEADS:]], axis=1).astype(BF16)


def _pack_cmp(pe, w1, w2):
    eye = jnp.eye(N_KV, dtype=F32)
    pe2 = jnp.broadcast_to(pe.reshape(2, CMP_STRIDE, 1, HEAD_DIM), (2, CMP_STRIDE, N_KV, HEAD_DIM)).reshape(2, CHUNK)
    w1r = w1.reshape(2, CMP_STRIDE, HEAD_DIM, CMP_HIDDEN)
    w1p = jnp.einsum('pjdh,kc->pjkdch', w1r, eye).reshape(2, CHUNK, HID2).astype(BF16)
    w2p = jnp.einsum('hd,kc->khcd', w2, eye).reshape(HID2, KV_DIM).astype(BF16)
    return pe2, w1p, w2p


def _overlap_t(nch):
    ci = jnp.arange(nch)[None, :] * CMP_STRIDE
    sj = jnp.arange(NS_PAD)[:, None] * SEL_BLOCK
    return ((ci <= sj + SEL_BLOCK - 1) & (ci + CMP_BLOCK - 1 >= sj)).astype(BF16)


def _dim_major(a):
    lead = a.shape[:-3]
    n = len(lead)
    return a.transpose(*range(n), n + 1, n + 2, n).reshape(*lead, KV_DIM, a.shape[-3])


def _pos_major(a):
    lead = a.shape[:-2]
    n = len(lead)
    return a.reshape(*lead, N_KV, HEAD_DIM, a.shape[-1]).transpose(*range(n), n + 2, n, n + 1)


def _prompt_attention(q, gates, kcvc, kskw, vt, cmpk, cmpv, nb, seq):
    nch = seq // CMP_STRIDE
    abk = _cmp_ab_rows(kcvc, nb * seq, 0, cmpk[0], cmpk[1], seq)
    abv = _cmp_ab_rows(kcvc, nb * seq, 1, cmpv[0], cmpv[1], seq)
    kcmp, vcmp = _cmp_fin(abk, abv, cmpk[2], cmpv[2], nch, nch)
    kcmp = kcmp.reshape(nb, nch, KV_DIM).astype(BF16)
    vcmp_t = vcmp.reshape(nb, nch, KV_DIM).transpose(0, 2, 1).astype(BF16)
    return _pattn(q, gates, kcmp, vcmp_t, kskw, vt, _overlap_t(nch), nb, seq)


def _sample_cache(cmpk, cmpv, pool_k, pool_v, page_table, layer):
    ndb, n_pages = page_table.shape
    n_pool, _, page = pool_k.shape[1:]
    cpp = page // CMP_STRIDE
    nchp = n_pages * cpp
    pp = math.gcd(n_pool, POOL_PAGES)
    abk = _cmp_ab_pages(pool_k, layer, cmpk[0], cmpk[1], pp).reshape(n_pool, cpp, 2 * HID2)[page_table]
    abv = _cmp_ab_pages(pool_v, layer, cmpv[0], cmpv[1], pp).reshape(n_pool, cpp, 2 * HID2)[page_table]
    kcmp, vcmp = _cmp_fin(abk.reshape(ndb * nchp, 2 * HID2), abv.reshape(ndb * nchp, 2 * HID2), cmpk[2], cmpv[2],
                          nchp, math.gcd(ndb, SUBLANES) * nchp)
    return kcmp.reshape(ndb, nchp, KV_DIM), vcmp.reshape(ndb, nchp, KV_DIM)


def _sample_attention(q, gates, new4, cache, slc_k, slc_v, page_table, win_k, win_v, layer, dec):
    kcmp, vcmp = cache
    ndb, nchp = kcmp.shape[:2]
    n_pool, _, page = slc_k.shape[1:]
    past = page_table.shape[1] * page
    pages = (layer * n_pool + page_table).astype(jnp.int32)
    nrow = N_KV * GROUP * dec
    eye_kv = jnp.eye(N_KV, dtype=F32)
    q_s = q.reshape(ndb, dec, N_KV, GROUP, HEAD_DIM).transpose(0, 2, 3, 1, 4)
    qp = (q_s[:, :, :, :, None] * eye_kv[None, :, None, None, :, None]).reshape(ndb, nrow, KV_DIM).astype(BF16)
    g_s = gates.reshape(ndb, dec, N_KV, LANES)[..., :3 * GROUP].reshape(ndb, dec, N_KV, GROUP, 3)
    g_s = jnp.pad(g_s.transpose(0, 2, 3, 1, 4).reshape(ndb, nrow, 3), ((0, 0), (0, 0), (0, SUBLANES - 3)))
    e_t = jax.nn.one_hot(jnp.arange(past) // SEL_BLOCK, NS_PAD, dtype=BF16).T
    e_new = jax.nn.one_hot((past + jnp.arange(LANES)) // SEL_BLOCK, NS_PAD, dtype=BF16)
    rid = jnp.arange(nrow)
    gsum = ((rid[:, None] // (GROUP * dec) == rid[None, :] // (GROUP * dec))
            & (rid[:, None] % dec == rid[None, :] % dec)).astype(BF16)
    ksn, vsn, kwn, vwn = (a.reshape(ndb, dec, KV_DIM) for a in new4)
    o_t = _sattn(pages, qp, g_s, kcmp, vcmp, slc_k.reshape(-1, KV_DIM, page), slc_v.reshape(-1, KV_DIM, page),
                 win_k, win_v, layer,
                 ksn, vsn, kwn, vwn, _overlap_t(nchp), gsum, e_t, e_new, past, dec)
    o_t = o_t.reshape(ndb, N_KV, GROUP, dec, N_KV, HEAD_DIM)
    return jnp.einsum('bkgtcd,kc->btkgd', o_t, eye_kv).reshape(ndb * dec, ATT_DIM)


def kernel(x_prompt, x_sample, cache_cmp_k, cache_cmp_v, cache_slc_k, cache_slc_v, state_win_k, state_win_v,
           state_conv, page_table, w_in, cmp_pe, cmp_w1, cmp_w2, w_att_out, conv_w, w_conv_out, w_o,
           norm_mix, norm_ffn, w_ffn_in, w_ffn_out, norm_final):
    nb, seq, d = x_prompt.shape
    ndb, dec, _ = x_sample.shape
    depth, n_pool, page = cache_cmp_k.shape[:3]
    n_pages = page_table.shape[1]
    past = n_pages * page
    lw = state_win_k.shape[2]
    tp, ts = nb * seq, ndb * dec
    tm_p, tm_s = math.gcd(ROW_TILE, seq), math.gcd(ROW_TILE, ts)
    nchp = past // CMP_STRIDE
    keep = min(WINDOW, seq)
    assert seq % SEL_GROUP == 0 and seq >= WIN_SPAN and seq // SEL_BLOCK <= NS_PAD
    assert dec <= SUBLANES and N_KV * GROUP * dec <= LANES and (N_KV * GROUP * dec) % (2 * SUBLANES) == 0
    assert tm_s % SUBLANES == 0 and tm_s % dec == 0
    assert (past + dec - CMP_BLOCK) // CMP_STRIDE + 1 == nchp - 1
    assert -(-(past + dec) // SEL_BLOCK) <= NS_PAD and past >= WINDOW and lw == WINDOW

    rope_p = _rope_tables(jnp.arange(seq, dtype=jnp.int32))
    rope_s = _rope_tables(jnp.tile(past + jnp.arange(dec, dtype=jnp.int32), ndb))
    pool_k, pool_v, slc_k, slc_v, win_k, win_v = (
        _dim_major(a) for a in (cache_cmp_k, cache_cmp_v, cache_slc_k, cache_slc_v, state_win_k, state_win_v))
    no_side = jnp.zeros((tm_p, CONV_DIM), F32)
    zero = jnp.zeros((ndb, dec - 1, CONV_DIM), F32)

    cmp_w = [[_pack_cmp(cmp_pe[l, a], cmp_w1[l, a], cmp_w2[l, a]) for a in range(2)] for l in range(depth)]
    caches = [_sample_cache(*cmp_w[l], pool_k, pool_v, page_table, l) for l in range(depth)]

    w_inp = _pack_w_in(w_in)
    w_mix = (w_att_out.astype(BF16), w_conv_out.astype(BF16), w_o.astype(BF16))
    w_ffn = (w_ffn_in.astype(BF16), w_ffn_out.astype(BF16), norm_final[None, :])

    xp, xs = x_prompt.reshape(tp, d), x_sample.reshape(ts, d)
    st_p = [[] for _ in range(7)]
    st_s = [[] for _ in range(7)]
    for l in range(depth):
        g_l = norm_mix[l][None, :]
        g_f = norm_ffn[l][None, :]
        cmpk, cmpv = cmp_w[l]
        cw = jnp.pad(conv_w[l], ((0, SUBLANES - CONV_WIDTH), (0, 0)))
        final = l == depth - 1

        q, kcvc, kskw, vt, stp, gates, u, cb, sga, sgb = _inproj(xp, g_l, w_inp, l, rope_p, nb, tm_p)
        o_att = _prompt_attention(q, gates, kcvc, kskw, vt, cmpk, cmpv, nb, seq)
        xp = _mix(xp, o_att, u, cb, sga, sgb, no_side, no_side, cw, *w_mix, l, tm_p, seq)
        xp = _ffn(xp, g_f, *w_ffn, l, tm_p, final)
        for j in range(4):
            st_p[j].append(stp[j])
        st_p[4].append(stp[4][:, :, seq - keep:])
        st_p[5].append(stp[5][:, :, seq - keep:])
        st_p[6].append(jnp.stack([u[(b + 1) * seq - (CONV_WIDTH - 1):(b + 1) * seq] for b in range(nb)]))

        q, kcvc, kskw, vt, sts, gates, u, cb, sga, sgb = _inproj(xs, g_l, w_inp, l, rope_s, 1, tm_s)
        new_t = sts[:, 0].reshape(6, KV_DIM, ndb, dec).transpose(0, 2, 1, 3)
        new_rows = lambda j: sts[j, 0].T
        o_att = _sample_attention(q, gates, [kskw[:, :KV_DIM], new_rows(3), kskw[:, KV_DIM:], new_rows(5)],
                                  caches[l], slc_k, slc_v, page_table, win_k, win_v, l, dec)
        cbuf = state_conv[l]
        side1 = jnp.concatenate([cbuf[:, 1:2], zero], axis=1).reshape(ts, CONV_DIM)
        side2 = jnp.concatenate([cbuf[:, 0:2], zero[:, 1:]], axis=1).reshape(ts, CONV_DIM)
        xs = _mix(xs, o_att, u, cb, sga, sgb, side1, side2, cw, *w_mix, l, tm_s, dec)
        xs = _ffn(xs, g_f, *w_ffn, l, tm_s, final)
        for j in range(6):
            st_s[j].append(new_t[j])
        st_s[6].append(jnp.concatenate([cbuf, u.reshape(ndb, dec, CONV_DIM)], axis=1)[:, -(CONV_WIDTH - 1):])

    kv_p = [_pos_major(jnp.stack(a)) for a in st_p[:6]]
    kv_s = [jnp.stack(a) for a in st_s[:6]]
    kv_s[4] = jnp.concatenate([win_k[:, :, :, dec:], kv_s[4]], axis=-1)
    kv_s[5] = jnp.concatenate([win_v[:, :, :, dec:], kv_s[5]], axis=-1)
    kv_s = [_pos_major(a) for a in kv_s]
    return (xp.reshape(nb, seq, d), xs.reshape(ndb, dec, d), *kv_p, jnp.stack(st_p[6]),
            *kv_s, jnp.stack(st_s[6]))
```

```python
import functools
import math

import jax
import jax.numpy as jnp
from jax import lax
from jax.experimental import pallas as pl
from jax.experimental.pallas import tpu as pltpu

N_HEADS = 8
HEAD_DIM = 64
N_KV = 2
GROUP = N_HEADS // N_KV
ATT_DIM = N_HEADS * HEAD_DIM
KV_DIM = N_KV * HEAD_DIM
ROPE_DIM = HEAD_DIM // 4
ROPE_HALF = ROPE_DIM // 2
ROPE_THETA = 500000.0
CMP_BLOCK = 32
CMP_STRIDE = 16
CMP_HIDDEN = 2 * HEAD_DIM
SEL_BLOCK = 64
SEL_TOPK = 16
N_LOCAL_BLOCKS = 2
WINDOW = 512
CONV_DIM = 512
CONV_WIDTH = 3
EPS = 1e-6
NEG = -1e30
FORCE_SCORE = 1e4

LANES = 128
SUBLANES = 8
Q_TILE = 256
SEL_CHUNK = 256
SEL_GROUP = 4 * SEL_CHUNK
LOG2E = 1.4426950408889634
VT_ROWS = HEAD_DIM + 16
WIN_SPAN = WINDOW + Q_TILE
ROW_TILE = 512
POOL_PAGES = 64
NS_PAD = LANES
VMEM_LIMIT = 56 * 1024 * 1024

C_Q = 0
C_KV = C_Q + ATT_DIM
C_GL = C_KV + 6 * KV_DIM
C_CU = C_GL + N_KV * LANES
C_CB = C_CU + CONV_DIM
C_CC = C_CB + CONV_DIM
F32 = jnp.float32
BF16 = jnp.bfloat16


def _nt(a, b):
    return lax.dot_general(a, b, (((1,), (1,)), ((), ())), preferred_element_type=F32)


def _nn(a, b):
    return jnp.dot(a, b, preferred_element_type=F32)


def _split3(x):
    hi = x.astype(BF16)
    r1 = x - hi.astype(F32)
    mid = r1.astype(BF16)
    lo = (r1 - mid.astype(F32)).astype(BF16)
    return hi, mid, lo


def _params(sem):
    return pltpu.CompilerParams(dimension_semantics=sem, vmem_limit_bytes=VMEM_LIMIT)


def _inproj_body(x_ref, g_ref, w_ref, rc_ref, rs1_ref, rs2_ref,
                 q_ref, kcvc_ref, kskw_ref, vt_ref, st_ref, gate_ref, u_ref, cb_ref, sga_ref, sgb_ref, *, d_model):
    x = x_ref[...]
    h = x * lax.rsqrt(jnp.mean(x * x, axis=-1, keepdims=True) + EPS)
    hb = (h * g_ref[...]).astype(BF16)
    rc, rs1, rs2 = rc_ref[...], rs1_ref[...], rs2_ref[...]

    def mm(lo, width):
        return _nt(hb, w_ref[lo:lo + width, :])

    def rope(z):
        return z * rc + pltpu.roll(z, LANES - ROPE_HALF, 1) * rs1 + pltpu.roll(z, ROPE_HALF, 1) * rs2

    for j in range(ATT_DIM // 256):
        z = mm(C_Q + 256 * j, 256)
        for t in range(2):
            q_ref[:, 256 * j + LANES * t:256 * j + LANES * (t + 1)] = (
                rope(z[:, LANES * t:LANES * (t + 1)]) * (HEAD_DIM ** -0.5))
    for j in range(3):
        z = mm(C_KV + 256 * j, 256)
        k, v = rope(z[:, :LANES]), z[:, LANES:]
        kt, vt = k.T, v.T
        st_ref[2 * j, 0] = kt
        st_ref[2 * j + 1, 0] = vt
        if j == 0:
            kcvc_ref[:, :LANES] = k
            kcvc_ref[:, LANES:] = v
        else:
            kskw_ref[:, LANES * (j - 1):LANES * j] = k.astype(BF16)
            ones_row = (lax.broadcasted_iota(jnp.int32, (VT_ROWS - HEAD_DIM, vt.shape[1]), 0) == 0).astype(BF16)
            for kvh in range(N_KV):
                vt_ref[j - 1, 0, VT_ROWS * kvh:VT_ROWS * kvh + HEAD_DIM] = vt[HEAD_DIM * kvh:HEAD_DIM * (kvh + 1)].astype(BF16)
                vt_ref[j - 1, 0, VT_ROWS * kvh + HEAD_DIM:VT_ROWS * (kvh + 1)] = ones_row
    gate_ref[...] = jax.nn.sigmoid(mm(C_GL, N_KV * LANES))
    for j in range(CONV_DIM // 256):
        u_ref[:, 256 * j:256 * (j + 1)] = mm(C_CC + 256 * j, 256) * mm(C_CU + 256 * j, 256)
        cb_ref[:, 256 * j:256 * (j + 1)] = mm(C_CB + 256 * j, 256)
    c_ga = C_CC + CONV_DIM
    c_gb = c_ga + d_model
    for j in range(d_model // 256):
        sga_ref[:, 256 * j:256 * (j + 1)] = jax.nn.sigmoid(mm(c_ga + 256 * j, 256))
        sgb_ref[:, 256 * j:256 * (j + 1)] = jax.nn.sigmoid(mm(c_gb + 256 * j, 256))


def _layer_block(layer, shape):
    return pl.BlockSpec((None,) + tuple(shape), lambda i: (layer,) + (0,) * len(shape))


def _inproj(x, g, w, layer, rope, nseq, tm):
    t, d = x.shape
    slen = t // nseq
    tps = slen // tm
    nrows = w.shape[1]
    row = lambda width: pl.BlockSpec((tm, width), lambda i: (i, 0))
    fixed = lambda shape: pl.BlockSpec(shape, lambda i: (0, 0))
    tab = pl.BlockSpec((tm, LANES), lambda i: (i % tps, 0))
    tr = lambda n, rows: pl.BlockSpec((n, 1, rows, tm), lambda i: (0, i // tps, 0, i % tps))
    sds = jax.ShapeDtypeStruct
    return pl.pallas_call(
        functools.partial(_inproj_body, d_model=d),
        grid=(t // tm,),
        in_specs=[row(d), fixed((1, d)), _layer_block(layer, (nrows, d)), tab, tab, tab],
        out_specs=[row(ATT_DIM), row(2 * KV_DIM), row(2 * KV_DIM), tr(2, N_KV * VT_ROWS), tr(6, KV_DIM), row(N_KV * LANES),
                   row(CONV_DIM), row(CONV_DIM), row(d), row(d)],
        out_shape=[sds((t, ATT_DIM), F32), sds((t, 2 * KV_DIM), F32), sds((t, 2 * KV_DIM), BF16),
                   sds((2, nseq, N_KV * VT_ROWS, slen), BF16), sds((6, nseq, KV_DIM, slen), F32), sds((t, N_KV * LANES), F32),
                   sds((t, CONV_DIM), F32), sds((t, CONV_DIM), F32), sds((t, d), F32), sds((t, d), F32)],
        compiler_params=_params(("arbitrary",)),
    )(x, g, w, *rope)


CHUNK = CMP_STRIDE * KV_DIM
HID2 = N_KV * CMP_HIDDEN


def _cmp_partials(src, pe_ref, w_ref, ab_ref):
    n = src.shape[0] // CMP_STRIDE
    acc_a = jnp.zeros((n, HID2), F32)
    acc_b = jnp.zeros((n, HID2), F32)
    for jp in range(CMP_STRIDE // 2):
        x = jnp.concatenate([src[pl.ds(2 * jp, n, stride=CMP_STRIDE), :],
                             src[pl.ds(2 * jp + 1, n, stride=CMP_STRIDE), :]], axis=1)
        cols = slice(2 * KV_DIM * jp, 2 * KV_DIM * (jp + 1))
        acc_a = acc_a + _nn((x + pe_ref[0:1, cols]).astype(BF16), w_ref[0, cols, :])
        acc_b = acc_b + _nn((x + pe_ref[1:2, cols]).astype(BF16), w_ref[1, cols, :])
    ab_ref[:, :HID2] = acc_a
    ab_ref[:, HID2:] = acc_b


def _cmp_ab_rows_body(x_ref, pe_ref, w_ref, ab_ref):
    _cmp_partials(x_ref, pe_ref, w_ref, ab_ref)


def _cmp_ab_rows(x, rows, col, pe, w, tr):
    return pl.pallas_call(
        _cmp_ab_rows_body,
        grid=(rows // tr,),
        in_specs=[pl.BlockSpec((tr, KV_DIM), lambda i: (i, col)),
                  pl.BlockSpec((2, CHUNK), lambda i: (0, 0)),
                  pl.BlockSpec((2, CHUNK, HID2), lambda i: (0, 0, 0))],
        out_specs=pl.BlockSpec((tr // CMP_STRIDE, 2 * HID2), lambda i: (i, 0)),
        out_shape=jax.ShapeDtypeStruct((rows // CMP_STRIDE, 2 * HID2), F32),
        compiler_params=_params(("arbitrary",)),
    )(x, pe, w)


def _cmp_ab_pages_body(x_ref, pe_ref, w_ref, ab_ref, rows_scr):
    page = x_ref.shape[3]
    for p in range(x_ref.shape[1]):
        rows_scr[page * p:page * (p + 1), :] = x_ref[0, p].T
    _cmp_partials(rows_scr, pe_ref, w_ref, ab_ref)


def _cmp_ab_pages(pool, layer, pe, w, pp):
    _, n_pool, _, page = pool.shape
    cpp = page // CMP_STRIDE
    return pl.pallas_call(
        _cmp_ab_pages_body,
        grid=(n_pool // pp,),
        in_specs=[pl.BlockSpec((1, pp, KV_DIM, page), lambda i: (layer, i, 0, 0)),
                  pl.BlockSpec((2, CHUNK), lambda i: (0, 0)),
                  pl.BlockSpec((2, CHUNK, HID2), lambda i: (0, 0, 0))],
        out_specs=pl.BlockSpec((pp * cpp, 2 * HID2), lambda i: (i, 0)),
        out_shape=jax.ShapeDtypeStruct((n_pool * cpp, 2 * HID2), F32),
        scratch_shapes=[pltpu.VMEM((pp * page, KV_DIM), F32)],
        compiler_params=_params(("arbitrary",)),
    )(pool, pe, w)


def _cmp_fin_body(abk_ref, abv_ref, w2k_ref, w2v_ref, ok_ref, ov_ref, *, nch):
    rows = abk_ref.shape[0]
    valid = lax.broadcasted_iota(jnp.int32, (rows, 1), 0) % nch < nch - 1

    def fin(ab_ref, w2_ref, o_ref):
        ab = ab_ref[...]
        hid = ab[:, :HID2] + pltpu.roll(ab[:, HID2:], rows - 1, 0)
        act = hid * jax.nn.sigmoid(hid)
        o_ref[...] = jnp.where(valid, _nn(act.astype(BF16), w2_ref[...]), 0.0)

    fin(abk_ref, w2k_ref, ok_ref)
    fin(abv_ref, w2v_ref, ov_ref)


def _cmp_fin(abk, abv, w2k, w2v, nch, tr):
    rows = abk.shape[0]
    ab_spec = pl.BlockSpec((tr, 2 * HID2), lambda i: (i, 0))
    w_spec = pl.BlockSpec((HID2, KV_DIM), lambda i: (0, 0))
    o_spec = pl.BlockSpec((tr, KV_DIM), lambda i: (i, 0))
    return pl.pallas_call(
        functools.partial(_cmp_fin_body, nch=nch),
        grid=(rows // tr,),
        in_specs=[ab_spec, ab_spec, w_spec, w_spec],
        out_specs=[o_spec, o_spec],
        out_shape=[jax.ShapeDtypeStruct((rows, KV_DIM), F32)] * 2,
        compiler_params=_params(("arbitrary",)),
    )(abk, abv, w2k, w2v)


def _select_bias(imp_t, qpos, ntop):
    shape = imp_t.shape
    s_io = lax.broadcasted_iota(jnp.int32, shape, 0)
    cur = qpos // SEL_BLOCK
    avail = s_io <= cur
    forced = (s_io == 0) | (avail & (s_io > cur - N_LOCAL_BLOCKS))
    val = jnp.where(avail, jnp.where(forced, FORCE_SCORE, imp_t), NEG)

    def pick_one(_, carry):
        val, sel = carry
        m = jnp.max(val, axis=0, keepdims=True)
        idx = jnp.min(jnp.where(val == m, s_io, NS_PAD), axis=0, keepdims=True)
        pick = s_io == idx
        return jnp.where(pick, -jnp.inf, val), jnp.where(pick, 1.0, sel)

    _, sel = lax.fori_loop(0, ntop, pick_one, (val, jnp.zeros(shape, F32)), unroll=True)
    return jnp.where((sel > 0.0) & avail, 0.0, NEG)


def _pattn_body(q_ref, g_ref, kc_ref, vct_ref, ks_ref, kw_ref, vst_ref, vwt_ref, ovt_ref, o_ref, bias_scr, *, seq):
    kv = pl.program_id(1)
    i = pl.program_id(2)
    nch = seq // CMP_STRIDE
    cols = GROUP * Q_TILE
    q = q_ref[...]
    qt = jnp.concatenate([q[:, :LANES].T, q[:, LANES:].T], axis=0)
    qt4 = jnp.concatenate([qt[HEAD_DIM * h:HEAD_DIM * (h + 1)] for h in range(GROUP)], axis=1)
    row_kv = lax.broadcasted_iota(jnp.int32, (KV_DIM, 1), 0) // HEAD_DIM
    qt_pad = jnp.where(row_kv == kv, jnp.concatenate([qt4] * N_KV, axis=0) * LOG2E, 0.0).astype(BF16)
    qpos1 = i * Q_TILE + lax.broadcasted_iota(jnp.int32, (1, Q_TILE), 1)
    qpos = jnp.concatenate([qpos1] * GROUP, axis=1)

    n_below = (i * Q_TILE) // SEL_GROUP
    w0 = pl.multiple_of(jnp.maximum(i * Q_TILE - WINDOW, 0), Q_TILE)
    s = _nn(kc_ref[0], qt_pad)
    s_win = _nn(kw_ref[pl.ds(w0, WIN_SPAN), :], qt_pad)
    s_diag = _nn(ks_ref[pl.ds(pl.multiple_of(n_below * SEL_GROUP, SEL_GROUP), SEL_GROUP), :], qt_pad)

    cmask = lax.broadcasted_iota(jnp.int32, (nch, 1), 0) * CMP_STRIDE + (CMP_BLOCK - 1) <= qpos
    s = jnp.where(cmask, s, NEG)
    p = jnp.where(cmask, jnp.exp2(s - jnp.max(s, axis=0, keepdims=True)), 0.0)
    l = jnp.sum(p, axis=0, keepdims=True)
    pc = p * jnp.where(l > 0.0, 1.0 / l, 0.0)
    o_c = _nn(vct_ref[0], pc.astype(BF16))

    pcs = pc[:, :Q_TILE]
    for h in range(1, GROUP):
        pcs = pcs + pc[:, h * Q_TILE:(h + 1) * Q_TILE]
    ovt = ovt_ref[...]
    imp_t = sum(_nn(ovt, part) for part in _split3(pcs))
    bias_scr[...] = _select_bias(imp_t, qpos1, min(SEL_TOPK, seq // SEL_BLOCK))

    kpos = w0 + lax.broadcasted_iota(jnp.int32, (WIN_SPAN, 1), 0)
    s = jnp.where((kpos <= qpos) & (kpos > qpos - WINDOW), s_win, NEG)
    p = jnp.exp2(s - jnp.max(s, axis=0, keepdims=True))
    acc_w = _nn(vwt_ref[0, 0, :, pl.ds(w0, WIN_SPAN)], p.astype(BF16))
    o_w = acc_w[:HEAD_DIM] / acc_w[HEAD_DIM:HEAD_DIM + 1]

    def group(gj, nkeys, carry, s_all, causal):
        m, acc = carry
        g0 = pl.multiple_of(gj * nkeys, nkeys)
        if s_all is None:
            s_all = _nn(ks_ref[pl.ds(g0, nkeys), :], qt_pad)
        for c in range(nkeys // SEL_CHUNK):
            k0 = pl.multiple_of(g0 + SEL_CHUNK * c, SEL_CHUNK)
            blocks = []
            for r in range(SEL_CHUNK // SEL_BLOCK):
                row = SEL_CHUNK * c + SEL_BLOCK * r
                s_blk = s_all[row:row + SEL_BLOCK]
                if causal:
                    kpos = k0 + SEL_BLOCK * r + lax.broadcasted_iota(jnp.int32, (SEL_BLOCK, 1), 0)
                    s_blk = jnp.where(kpos <= qpos, s_blk, NEG)
                b_row = bias_scr[pl.ds(gj * (nkeys // SEL_BLOCK) + row // SEL_BLOCK, 1), :]
                blocks.append((s_blk, jnp.concatenate([b_row] * GROUP, axis=1)))
            m_new = m
            for s_blk, bias in blocks:
                m_new = jnp.maximum(m_new, jnp.max(s_blk, axis=0, keepdims=True) + bias)
            alpha = jnp.exp2(m - m_new)
            p = jnp.concatenate([jnp.exp2(s_blk - (m_new - bias)) for s_blk, bias in blocks], axis=0)
            acc = alpha * acc + _nn(vst_ref[0, 0, :, pl.ds(k0, SEL_CHUNK)], p.astype(BF16))
            m = m_new
        return m, acc

    init = (jnp.full((1, cols), -3e38, F32), jnp.zeros((VT_ROWS, cols), F32))
    carry = group(n_below, SEL_GROUP, init, s_diag, True)
    carry = lax.fori_loop(0, n_below // 2, lambda gj, c: group(gj, 2 * SEL_GROUP, c, None, False), carry)
    _, acc_s = lax.cond(n_below % 2 == 1, lambda c: group(n_below - 1, SEL_GROUP, c, None, False), lambda c: c, carry)
    o_s = acc_s[:HEAD_DIM] / acc_s[HEAD_DIM:HEAD_DIM + 1]

    gt = g_ref[...].T
    outs = []
    for h in range(GROUP):
        c = slice(h * Q_TILE, (h + 1) * Q_TILE)
        outs.append(gt[3 * h:3 * h + 1] * o_c[:, c] + gt[3 * h + 1:3 * h + 2] * o_s[:, c]
                    + gt[3 * h + 2:3 * h + 3] * o_w[:, c])
    for t in range(GROUP // 2):
        o_ref[:, LANES * t:LANES * (t + 1)] = jnp.concatenate(outs[2 * t:2 * t + 2], axis=0).T


def _pattn(q, gates, kcmp, vcmp_t, kskw, vt, ovt, nb, seq):
    nq = seq // Q_TILE
    nch = seq // CMP_STRIDE
    qspec = pl.BlockSpec((Q_TILE, GROUP * HEAD_DIM), lambda b, k, i: (b * nq + i, k))
    gspec = pl.BlockSpec((Q_TILE, LANES), lambda b, k, i: (b * nq + i, k))
    kspec = lambda which: pl.BlockSpec((seq, KV_DIM), lambda b, k, i: (b, which))
    vspec = lambda which: pl.BlockSpec((1, 1, VT_ROWS, seq), lambda b, k, i: (which, b, k, 0))
    return pl.pallas_call(
        functools.partial(_pattn_body, seq=seq),
        grid=(nb, N_KV, nq),
        in_specs=[qspec, gspec,
                  pl.BlockSpec((1, nch, KV_DIM), lambda b, k, i: (b, 0, 0)),
                  pl.BlockSpec((1, HEAD_DIM, nch), lambda b, k, i: (b, k, 0)),
                  kspec(0), kspec(1), vspec(0), vspec(1),
                  pl.BlockSpec((NS_PAD, nch), lambda b, k, i: (0, 0))],
        out_specs=qspec,
        out_shape=jax.ShapeDtypeStruct((nb * seq, ATT_DIM), F32),
        scratch_shapes=[pltpu.VMEM((NS_PAD, Q_TILE), F32)],
        compiler_params=_params(("arbitrary", "arbitrary", "arbitrary")),
    )(q, gates, kcmp, vcmp_t, kskw, kskw, vt, vt, ovt)


def _sattn_body(pages_ref, qp_ref, g_ref, kc_ref, vc_ref, ks_hbm, vs_hbm, kw_ref, vw_ref,
                ksn_ref, vsn_ref, kwn_ref, vwn_ref, ovt_ref, gsum_ref, et_ref, en_ref, o_ref,
                kbuf, vbuf, sems, *, past, dec, ntop):
    b = pl.program_id(0)
    nchp = kc_ref.shape[1]
    n_pages = kbuf.shape[1]
    lw = kw_ref.shape[3]

    def page_copies(seq_idx, slot):
        return [pltpu.make_async_copy(src.at[pages_ref[seq_idx, pg]], buf.at[slot, pg], sems.at[a, slot])
                for a, (src, buf) in enumerate(((ks_hbm, kbuf), (vs_hbm, vbuf))) for pg in range(n_pages)]

    slot = b % 2

    @pl.when(b == 0)
    def _():
        for cp in page_copies(0, 0):
            cp.start()

    @pl.when(b + 1 < pl.num_programs(0))
    def _():
        for cp in page_copies(b + 1, 1 - slot):
            cp.start()

    qp = qp_ref[0]
    nrow = qp.shape[0]
    tok_r = lax.broadcasted_iota(jnp.int32, (nrow, 1), 0) % dec
    tok_l = lax.broadcasted_iota(jnp.int32, (1, LANES), 1) % dec
    new_j = lax.broadcasted_iota(jnp.int32, (1, LANES), 1)
    new_mask = (new_j < dec) & (new_j <= tok_r)

    def pad_rows(x):
        return jnp.concatenate([x.astype(F32), jnp.zeros((LANES - dec, KV_DIM), F32)], axis=0).astype(BF16)

    def softmax2(s_p, s_n):
        m = jnp.maximum(jnp.max(s_p, axis=-1, keepdims=True), jnp.max(s_n, axis=-1, keepdims=True))
        p_p, p_n = jnp.exp(s_p - m), jnp.exp(s_n - m)
        l = jnp.sum(p_p, axis=-1, keepdims=True) + jnp.sum(p_n, axis=-1, keepdims=True)
        return p_p.astype(BF16), p_n.astype(BF16), l

    s = _nn(qp, kc_ref[0].T.astype(BF16))
    sw_p = _nn(qp, kw_ref[0, 0].astype(BF16))
    sw_n = _nt(qp, pad_rows(kwn_ref[0]))
    for cp in page_copies(b, slot):
        cp.wait()
    kt = jnp.concatenate([kbuf[slot, pg] for pg in range(n_pages)], axis=1).astype(BF16)
    ss_p = _nn(qp, kt)
    ss_n = _nt(qp, pad_rows(ksn_ref[0]))

    cmask = lax.broadcasted_iota(jnp.int32, (1, nchp), 1) < nchp - 1
    s = jnp.where(cmask, s, NEG)
    p = jnp.where(cmask, jnp.exp(s - jnp.max(s, axis=-1, keepdims=True)), 0.0)
    pc = p / jnp.sum(p, axis=-1, keepdims=True)
    o_c = _nn(pc.astype(BF16), vc_ref[0].astype(BF16))

    gsum = gsum_ref[...]
    pcs = sum(_nn(gsum, part) for part in _split3(pc))
    pcs = jnp.concatenate([pcs, jnp.zeros((LANES - nrow, nchp), F32)], axis=0)
    ovt = ovt_ref[...]
    imp_t = sum(_nt(ovt, part) for part in _split3(pcs))
    bias = _select_bias(imp_t, past + tok_l, ntop).T[:nrow].astype(BF16)

    kpos = (past - lw) + lax.broadcasted_iota(jnp.int32, (1, lw), 1)
    p_p, p_n, l = softmax2(jnp.where(kpos > past + tok_r - WINDOW, sw_p, NEG), jnp.where(new_mask, sw_n, NEG))
    o_w = (_nt(p_p, vw_ref[0, 0].astype(BF16)) + _nn(p_n, pad_rows(vwn_ref[0]))) / l

    vt = jnp.concatenate([vbuf[slot, pg] for pg in range(n_pages)], axis=1).astype(BF16)
    s_p = ss_p + _nn(bias, et_ref[...])
    s_n = jnp.where(new_mask, ss_n + _nt(bias, en_ref[...]), NEG)
    p_p, p_n, l = softmax2(s_p, s_n)
    o_s = (_nt(p_p, vt) + _nn(p_n, pad_rows(vsn_ref[0]))) / l

    g = g_ref[0]
    o_ref[0] = g[:, 0:1] * o_c + g[:, 1:2] * o_s + g[:, 2:3] * o_w


def _sattn(pages, qp, g, kc, vc, ks, vs, win_k, win_v, layer, ksn, vsn, kwn, vwn, ovt, gsum, e_t, e_new, past, dec):
    nseq, nrow = qp.shape[:2]
    nchp = kc.shape[1]
    n_pages = pages.shape[1]
    page = ks.shape[2]
    lw = win_k.shape[3]
    ntop = min(SEL_TOPK, -(-(past + dec) // SEL_BLOCK))
    seq3 = lambda n, w: pl.BlockSpec((1, n, w), lambda b, pt: (b, 0, 0))
    fixed = lambda r, c: pl.BlockSpec((r, c), lambda b, pt: (0, 0))
    hbm = pl.BlockSpec(memory_space=pl.ANY)
    win = pl.BlockSpec((1, 1, KV_DIM, lw), lambda b, pt: (layer, b, 0, 0))
    return pl.pallas_call(
        functools.partial(_sattn_body, past=past, dec=dec, ntop=ntop),
        grid_spec=pltpu.PrefetchScalarGridSpec(
            num_scalar_prefetch=1,
            grid=(nseq,),
            in_specs=[seq3(nrow, KV_DIM), seq3(nrow, SUBLANES), seq3(nchp, KV_DIM), seq3(nchp, KV_DIM),
                      hbm, hbm, win, win,
                      seq3(dec, KV_DIM), seq3(dec, KV_DIM), seq3(dec, KV_DIM), seq3(dec, KV_DIM),
                      fixed(NS_PAD, nchp), fixed(nrow, nrow), fixed(NS_PAD, past), fixed(LANES, NS_PAD)],
            out_specs=seq3(nrow, KV_DIM),
            scratch_shapes=[pltpu.VMEM((2, n_pages, KV_DIM, page), F32), pltpu.VMEM((2, n_pages, KV_DIM, page), F32),
                            pltpu.SemaphoreType.DMA((2, 2))]),
        out_shape=jax.ShapeDtypeStruct((nseq, nrow, KV_DIM), F32),
        compiler_params=_params(("arbitrary",)),
    )(pages, qp, g, kc, vc, ks, vs, win_k, win_v, ksn, vsn, kwn, vwn, ovt, gsum, e_t, e_new)


def _mix_body(x_ref, o_ref, u_ref, uprev_ref, cb_ref, sga_ref, sgb_ref, s1_ref, s2_ref,
              cw_ref, wa_ref, wc_ref, wo_ref, y_ref, ubuf, *, tm, slen):
    i = pl.program_id(0)
    u = u_ref[...]
    ubuf[0:SUBLANES, :] = uprev_ref[...]
    ubuf[SUBLANES:, :] = u
    pos = (i * tm + lax.broadcasted_iota(jnp.int32, (tm, 1), 0)) % slen
    prev1 = jnp.where(pos >= 1, ubuf[SUBLANES - 1:SUBLANES - 1 + tm, :], s1_ref[...])
    prev2 = jnp.where(pos >= 2, ubuf[SUBLANES - 2:SUBLANES - 2 + tm, :], s2_ref[...])
    cw = cw_ref[...]
    y_conv = cw[0:1] * prev2 + cw[1:2] * prev1 + cw[2:3] * u
    a_br = _nn(o_ref[...].astype(BF16), wa_ref[...])
    c_br = _nn((cb_ref[...] * y_conv).astype(BF16), wc_ref[...])
    mix = sga_ref[...] * a_br + sgb_ref[...] * c_br
    y_ref[...] = x_ref[...] + _nn(mix.astype(BF16), wo_ref[...])


def _mix(x, o_att, u, cb, sga, sgb, side1, side2, conv_w, wa, wc, wo, layer, tm, slen):
    t, d = x.shape
    row = lambda width: pl.BlockSpec((tm, width), lambda i: (i, 0))
    fixed = lambda shape: pl.BlockSpec(shape, lambda i: (0, 0))
    n_side = side1.shape[0] // tm
    side = pl.BlockSpec((tm, CONV_DIM), lambda i: (i % n_side, 0))
    prev = pl.BlockSpec((SUBLANES, CONV_DIM), lambda i: (jnp.maximum(i * (tm // SUBLANES) - 1, 0), 0))
    return pl.pallas_call(
        functools.partial(_mix_body, tm=tm, slen=slen),
        grid=(t // tm,),
        in_specs=[row(d), row(ATT_DIM), row(CONV_DIM), prev, row(CONV_DIM), row(d), row(d), side, side,
                  fixed((SUBLANES, CONV_DIM)), _layer_block(layer, (ATT_DIM, d)), _layer_block(layer, (CONV_DIM, d)),
                  _layer_block(layer, (d, d))],
        out_specs=row(d),
        out_shape=jax.ShapeDtypeStruct((t, d), F32),
        scratch_shapes=[pltpu.VMEM((tm + SUBLANES, CONV_DIM), F32)],
        compiler_params=_params(("arbitrary",)),
    )(x, o_att, u, u, cb, sga, sgb, side1, side2, conv_w, wa, wc, wo)


def _ffn_body(x_ref, g_ref, win_ref, wout_ref, gf_ref, y_ref, *, d_ff, fc, final):
    x = x_ref[...]
    h = x * lax.rsqrt(jnp.mean(x * x, axis=-1, keepdims=True) + EPS)
    hb = (h * g_ref[...]).astype(BF16)
    acc = x
    for c in range(d_ff // fc):
        gate = _nn(hb, win_ref[:, c * fc:(c + 1) * fc])
        up = _nn(hb, win_ref[:, d_ff + c * fc:d_ff + (c + 1) * fc])
        act = (gate * jax.nn.sigmoid(gate) * up).astype(BF16)
        acc = acc + _nn(act, wout_ref[c * fc:(c + 1) * fc, :])
    if final:
        acc = acc * lax.rsqrt(jnp.mean(acc * acc, axis=-1, keepdims=True) + EPS) * gf_ref[...]
    y_ref[...] = acc


def _ffn(x, g, w_in, w_out, g_final, layer, tm, final):
    t, d = x.shape
    d_ff = w_out.shape[1]
    fc = 256 if d_ff % 256 == 0 else LANES
    row = pl.BlockSpec((tm, d), lambda i: (i, 0))
    fixed = lambda shape: pl.BlockSpec(shape, lambda i: (0, 0))
    return pl.pallas_call(
        functools.partial(_ffn_body, d_ff=d_ff, fc=fc, final=final),
        grid=(t // tm,),
        in_specs=[row, fixed((1, d)), _layer_block(layer, (d, 2 * d_ff)), _layer_block(layer, (d_ff, d)), fixed((1, d))],
        out_specs=row,
        out_shape=jax.ShapeDtypeStruct((t, d), F32),
        compiler_params=_params(("arbitrary",)),
    )(x, g, w_in, w_out, g_final)


def _rope_tables(pos):
    inv = 1.0 / (ROPE_THETA ** (jnp.arange(ROPE_HALF, dtype=F32) * (2.0 / ROPE_DIM)))
    ang = pos.astype(F32)[:, None] * inv[None, :]
    cos, sin = jnp.cos(ang), jnp.sin(ang)
    t = pos.shape[0]
    pad = jnp.zeros((t, HEAD_DIM - ROPE_DIM), F32)
    rc = jnp.concatenate([cos, cos, pad + 1.0], axis=1)
    rs1 = jnp.concatenate([-sin, jnp.zeros_like(sin), pad], axis=1)
    rs2 = jnp.concatenate([jnp.zeros_like(sin), sin, pad], axis=1)
    rep = LANES // HEAD_DIM
    return tuple(jnp.tile(a, (1, rep)) for a in (rc, rs1, rs2))


def _pack_w_in(w):
    wt = w.transpose(0, 2, 1)
    depth, _, d = wt.shape
    o_gl = ATT_DIM + 6 * KV_DIM
    gl = wt[:, o_gl:o_gl + 3 * N_HEADS].reshape(depth, N_KV, 3 * GROUP, d)
    gl = jnp.pad(gl, ((0, 0), (0, 0), (0, LANES - 3 * GROUP), (0, 0))).reshape(depth, N_KV * LANES, d)
    return jnp.concatenate([wt[:, :o_gl], gl, wt[:, o_gl + 3 * N_HEADS:]], axis=1).astype(BF16)


def _pack_cmp(pe, w1, w2):
    eye = jnp.eye(N_KV, dtype=F32)
    pe2 = jnp.broadcast_to(pe.reshape(2, CMP_STRIDE, 1, HEAD_DIM), (2, CMP_STRIDE, N_KV, HEAD_DIM)).reshape(2, CHUNK)
    w1r = w1.reshape(2, CMP_STRIDE, HEAD_DIM, CMP_HIDDEN)
    w1p = jnp.einsum('pjdh,kc->pjkdch', w1r, eye).reshape(2, CHUNK, HID2).astype(BF16)
    w2p = jnp.einsum('hd,kc->khcd', w2, eye).reshape(HID2, KV_DIM).astype(BF16)
    return pe2, w1p, w2p


def _overlap_t(nch):
    ci = jnp.arange(nch)[None, :] * CMP_STRIDE
    sj = jnp.arange(NS_PAD)[:, None] * SEL_BLOCK
    return ((ci <= sj + SEL_BLOCK - 1) & (ci + CMP_BLOCK - 1 >= sj)).astype(BF16)


def _dim_major(a):
    lead = a.shape[:-3]
    n = len(lead)
    return a.transpose(*range(n), n + 1, n + 2, n).reshape(*lead, KV_DIM, a.shape[-3])


def _pos_major(a):
    lead = a.shape[:-2]
    n = len(lead)
    return a.reshape(*lead, N_KV, HEAD_DIM, a.shape[-1]).transpose(*range(n), n + 2, n, n + 1)


def _prompt_attention(q, gates, kcvc, kskw, vt, cmpk, cmpv, nb, seq):
    nch = seq // CMP_STRIDE
    abk = _cmp_ab_rows(kcvc, nb * seq, 0, cmpk[0], cmpk[1], seq)
    abv = _cmp_ab_rows(kcvc, nb * seq, 1, cmpv[0], cmpv[1], seq)
    kcmp, vcmp = _cmp_fin(abk, abv, cmpk[2], cmpv[2], nch, nch)
    kcmp = kcmp.reshape(nb, nch, KV_DIM).astype(BF16)
    vcmp_t = vcmp.reshape(nb, nch, KV_DIM).transpose(0, 2, 1).astype(BF16)
    return _pattn(q, gates, kcmp, vcmp_t, kskw, vt, _overlap_t(nch), nb, seq)


def _sample_cache(cmpk, cmpv, pool_k, pool_v, page_table, layer):
    ndb, n_pages = page_table.shape
    n_pool, _, page = pool_k.shape[1:]
    cpp = page // CMP_STRIDE
    nchp = n_pages * cpp
    pp = math.gcd(n_pool, POOL_PAGES)
    abk = _cmp_ab_pages(pool_k, layer, cmpk[0], cmpk[1], pp).reshape(n_pool, cpp, 2 * HID2)[page_table]
    abv = _cmp_ab_pages(pool_v, layer, cmpv[0], cmpv[1], pp).reshape(n_pool, cpp, 2 * HID2)[page_table]
    kcmp, vcmp = _cmp_fin(abk.reshape(ndb * nchp, 2 * HID2), abv.reshape(ndb * nchp, 2 * HID2), cmpk[2], cmpv[2],
                          nchp, math.gcd(ndb, SUBLANES) * nchp)
    return kcmp.reshape(ndb, nchp, KV_DIM), vcmp.reshape(ndb, nchp, KV_DIM)


def _sample_attention(q, gates, new4, cache, slc_k, slc_v, page_table, win_k, win_v, layer, dec):
    kcmp, vcmp = cache
    ndb, nchp = kcmp.shape[:2]
    n_pool, _, page = slc_k.shape[1:]
    past = page_table.shape[1] * page
    pages = (layer * n_pool + page_table).astype(jnp.int32)
    nrow = N_KV * GROUP * dec
    eye_kv = jnp.eye(N_KV, dtype=F32)
    q_s = q.reshape(ndb, dec, N_KV, GROUP, HEAD_DIM).transpose(0, 2, 3, 1, 4)
    qp = (q_s[:, :, :, :, None] * eye_kv[None, :, None, None, :, None]).reshape(ndb, nrow, KV_DIM).astype(BF16)
    g_s = gates.reshape(ndb, dec, N_KV, LANES)[..., :3 * GROUP].reshape(ndb, dec, N_KV, GROUP, 3)
    g_s = jnp.pad(g_s.transpose(0, 2, 3, 1, 4).reshape(ndb, nrow, 3), ((0, 0), (0, 0), (0, SUBLANES - 3)))
    e_t = jax.nn.one_hot(jnp.arange(past) // SEL_BLOCK, NS_PAD, dtype=BF16).T
    e_new = jax.nn.one_hot((past + jnp.arange(LANES)) // SEL_BLOCK, NS_PAD, dtype=BF16)
    rid = jnp.arange(nrow)
    gsum = ((rid[:, None] // (GROUP * dec) == rid[None, :] // (GROUP * dec))
            & (rid[:, None] % dec == rid[None, :] % dec)).astype(BF16)
    ksn, vsn, kwn, vwn = (a.reshape(ndb, dec, KV_DIM) for a in new4)
    o_t = _sattn(pages, qp, g_s, kcmp, vcmp, slc_k.reshape(-1, KV_DIM, page), slc_v.reshape(-1, KV_DIM, page),
                 win_k, win_v, layer,
                 ksn, vsn, kwn, vwn, _overlap_t(nchp), gsum, e_t, e_new, past, dec)
    o_t = o_t.reshape(ndb, N_KV, GROUP, dec, N_KV, HEAD_DIM)
    return jnp.einsum('bkgtcd,kc->btkgd', o_t, eye_kv).reshape(ndb * dec, ATT_DIM)


def kernel(x_prompt, x_sample, cache_cmp_k, cache_cmp_v, cache_slc_k, cache_slc_v, state_win_k, state_win_v,
           state_conv, page_table, w_in, cmp_pe, cmp_w1, cmp_w2, w_att_out, conv_w, w_conv_out, w_o,
           norm_mix, norm_ffn, w_ffn_in, w_ffn_out, norm_final):
    nb, seq, d = x_prompt.shape
    ndb, dec, _ = x_sample.shape
    depth, n_pool, page = cache_cmp_k.shape[:3]
    n_pages = page_table.shape[1]
    past = n_pages * page
    lw = state_win_k.shape[2]
    tp, ts = nb * seq, ndb * dec
    tm_p, tm_s = math.gcd(ROW_TILE, seq), math.gcd(ROW_TILE, ts)
    nchp = past // CMP_STRIDE
    keep = min(WINDOW, seq)
    assert seq % SEL_GROUP == 0 and seq >= WIN_SPAN and seq // SEL_BLOCK <= NS_PAD
    assert dec <= SUBLANES and N_KV * GROUP * dec <= LANES and (N_KV * GROUP * dec) % (2 * SUBLANES) == 0
    assert tm_s % SUBLANES == 0 and tm_s % dec == 0
    assert (past + dec - CMP_BLOCK) // CMP_STRIDE + 1 == nchp - 1
    assert -(-(past + dec) // SEL_BLOCK) <= NS_PAD and past >= WINDOW and lw == WINDOW

    rope_p = _rope_tables(jnp.arange(seq, dtype=jnp.int32))
    rope_s = _rope_tables(jnp.tile(past + jnp.arange(dec, dtype=jnp.int32), ndb))
    pool_k, pool_v, slc_k, slc_v, win_k, win_v = (
        _dim_major(a) for a in (cache_cmp_k, cache_cmp_v, cache_slc_k, cache_slc_v, state_win_k, state_win_v))
    no_side = jnp.zeros((tm_p, CONV_DIM), F32)
    zero = jnp.zeros((ndb, dec - 1, CONV_DIM), F32)

    cmp_w = [[_pack_cmp(cmp_pe[l, a], cmp_w1[l, a], cmp_w2[l, a]) for a in range(2)] for l in range(depth)]
    caches = [_sample_cache(*cmp_w[l], pool_k, pool_v, page_table, l) for l in range(depth)]

    w_inp = _pack_w_in(w_in)
    w_mix = (w_att_out.astype(BF16), w_conv_out.astype(BF16), w_o.astype(BF16))
    w_ffn = (w_ffn_in.astype(BF16), w_ffn_out.astype(BF16), norm_final[None, :])

    xp, xs = x_prompt.reshape(tp, d), x_sample.reshape(ts, d)
    st_p = [[] for _ in range(7)]
    st_s = [[] for _ in range(7)]
    for l in range(depth):
        g_l = norm_mix[l][None, :]
        g_f = norm_ffn[l][None, :]
        cmpk, cmpv = cmp_w[l]
        cw = jnp.pad(conv_w[l], ((0, SUBLANES - CONV_WIDTH), (0, 0)))
        final = l == depth - 1

        q, kcvc, kskw, vt, stp, gates, u, cb, sga, sgb = _inproj(xp, g_l, w_inp, l, rope_p, nb, tm_p)
        o_att = _prompt_attention(q, gates, kcvc, kskw, vt, cmpk, cmpv, nb, seq)
        xp = _mix(xp, o_att, u, cb, sga, sgb, no_side, no_side, cw, *w_mix, l, tm_p, seq)
        xp = _ffn(xp, g_f, *w_ffn, l, tm_p, final)
        for j in range(4):
            st_p[j].append(stp[j])
        st_p[4].append(stp[4][:, :, seq - keep:])
        st_p[5].append(stp[5][:, :, seq - keep:])
        st_p[6].append(jnp.stack([u[(b + 1) * seq - (CONV_WIDTH - 1):(b + 1) * seq] for b in range(nb)]))

        q, kcvc, kskw, vt, sts, gates, u, cb, sga, sgb = _inproj(xs, g_l, w_inp, l, rope_s, 1, tm_s)
        new_t = sts[:, 0].reshape(6, KV_DIM, ndb, dec).transpose(0, 2, 1, 3)
        new_rows = lambda j: sts[j, 0].T
        o_att = _sample_attention(q, gates, [kskw[:, :KV_DIM], new_rows(3), kskw[:, KV_DIM:], new_rows(5)],
                                  caches[l], slc_k, slc_v, page_table, win_k, win_v, l, dec)
        cbuf = state_conv[l]
        side1 = jnp.concatenate([cbuf[:, 1:2], zero], axis=1).reshape(ts, CONV_DIM)
        side2 = jnp.concatenate([cbuf[:, 0:2], zero[:, 1:]], axis=1).reshape(ts, CONV_DIM)
        xs = _mix(xs, o_att, u, cb, sga, sgb, side1, side2, cw, *w_mix, l, tm_s, dec)
        xs = _ffn(xs, g_f, *w_ffn, l, tm_s, final)
        for j in range(6):
            st_s[j].append(new_t[j])
        st_s[6].append(jnp.concatenate([cbuf, u.reshape(ndb, dec, CONV_DIM)], axis=1)[:, -(CONV_WIDTH - 1):])

    kv_p = [_pos_major(jnp.stack(a)) for a in st_p[:6]]
    kv_s = [jnp.stack(a) for a in st_s[:6]]
    kv_s[4] = jnp.concatenate([win_k[:, :, :, dec:], kv_s[4]], axis=-1)
    kv_s[5] = jnp.concatenate([win_v[:, :, :, dec:], kv_s[5]], axis=-1)
    kv_s = [_pos_major(a) for a in kv_s]
    return (xp.reshape(nb, seq, d), xs.reshape(ndb, dec, d), *kv_p, jnp.stack(st_p[6]),
            *kv_s, jnp.stack(st_s[6]))
```

```python
import functools
import math

import jax
import jax.numpy as jnp
from jax import lax
from jax.experimental import pallas as pl
from jax.experimental.pallas import tpu as pltpu

N_HEADS = 8
HEAD_DIM = 64
N_KV = 2
GROUP = N_HEADS // N_KV
ATT_DIM = N_HEADS * HEAD_DIM
KV_DIM = N_KV * HEAD_DIM
ROPE_DIM = HEAD_DIM // 4
ROPE_HALF = ROPE_DIM // 2
ROPE_THETA = 500000.0
CMP_BLOCK = 32
CMP_STRIDE = 16
CMP_HIDDEN = 2 * HEAD_DIM
SEL_BLOCK = 64
SEL_TOPK = 16
N_LOCAL_BLOCKS = 2
WINDOW = 512
CONV_DIM = 512
CONV_WIDTH = 3
EPS = 1e-6
NEG = -1e30
FORCE_SCORE = 1e4

LANES = 128
SUBLANES = 8
Q_TILE = 256
SEL_CHUNK = 256
SEL_GROUP = 4 * SEL_CHUNK
LOG2E = 1.4426950408889634
VT_ROWS = HEAD_DIM + 16
WIN_SPAN = WINDOW + Q_TILE
ROW_TILE = 512
POOL_PAGES = 64
NS_PAD = LANES
VMEM_LIMIT = 56 * 1024 * 1024

C_Q = 0
C_KV = C_Q + ATT_DIM
C_GL = C_KV + 6 * KV_DIM
C_CU = C_GL + N_KV * LANES
C_CB = C_CU + CONV_DIM
C_CC = C_CB + CONV_DIM
F32 = jnp.float32
BF16 = jnp.bfloat16


def _nt(a, b):
    return lax.dot_general(a, b, (((1,), (1,)), ((), ())), preferred_element_type=F32)


def _nn(a, b):
    return jnp.dot(a, b, preferred_element_type=F32)


def _split3(x):
    hi = x.astype(BF16)
    r1 = x - hi.astype(F32)
    mid = r1.astype(BF16)
    lo = (r1 - mid.astype(F32)).astype(BF16)
    return hi, mid, lo


def _params(sem):
    return pltpu.CompilerParams(dimension_semantics=sem, vmem_limit_bytes=VMEM_LIMIT)


def _inproj_body(x_ref, g_ref, w_ref, rc_ref, rs1_ref, rs2_ref,
                 q_ref, kcvc_ref, kskw_ref, vt_ref, st_ref, gate_ref, u_ref, cb_ref, sga_ref, sgb_ref, *, d_model):
    x = x_ref[...]
    h = x * lax.rsqrt(jnp.mean(x * x, axis=-1, keepdims=True) + EPS)
    hb = (h * g_ref[...]).astype(BF16)
    rc, rs1, rs2 = rc_ref[...], rs1_ref[...], rs2_ref[...]

    def mm(lo, width):
        return _nt(hb, w_ref[lo:lo + width, :])

    def rope(z):
        return z * rc + pltpu.roll(z, LANES - ROPE_HALF, 1) * rs1 + pltpu.roll(z, ROPE_HALF, 1) * rs2

    for j in range(ATT_DIM // 256):
        z = mm(C_Q + 256 * j, 256)
        for t in range(2):
            q_ref[:, 256 * j + LANES * t:256 * j + LANES * (t + 1)] = (
                rope(z[:, LANES * t:LANES * (t + 1)]) * (HEAD_DIM ** -0.5))
    for j in range(3):
        z = mm(C_KV + 256 * j, 256)
        k, v = rope(z[:, :LANES]), z[:, LANES:]
        kt, vt = k.T, v.T
        st_ref[2 * j, 0] = kt
        st_ref[2 * j + 1, 0] = vt
        if j == 0:
            kcvc_ref[:, :LANES] = k
            kcvc_ref[:, LANES:] = v
        else:
            kskw_ref[:, LANES * (j - 1):LANES * j] = k.astype(BF16)
            ones_row = (lax.broadcasted_iota(jnp.int32, (VT_ROWS - HEAD_DIM, vt.shape[1]), 0) == 0).astype(BF16)
            for kvh in range(N_KV):
                vt_ref[j - 1, 0, VT_ROWS * kvh:VT_ROWS * kvh + HEAD_DIM] = vt[HEAD_DIM * kvh:HEAD_DIM * (kvh + 1)].astype(BF16)
                vt_ref[j - 1, 0, VT_ROWS * kvh + HEAD_DIM:VT_ROWS * (kvh + 1)] = ones_row
    gate_ref[...] = jax.nn.sigmoid(mm(C_GL, N_KV * LANES))
    for j in range(CONV_DIM // 256):
        u_ref[:, 256 * j:256 * (j + 1)] = mm(C_CC + 256 * j, 256) * mm(C_CU + 256 * j, 256)
        cb_ref[:, 256 * j:256 * (j + 1)] = mm(C_CB + 256 * j, 256)
    c_ga = C_CC + CONV_DIM
    c_gb = c_ga + d_model
    for j in range(d_model // 256):
        sga_ref[:, 256 * j:256 * (j + 1)] = jax.nn.sigmoid(mm(c_ga + 256 * j, 256))
        sgb_ref[:, 256 * j:256 * (j + 1)] = jax.nn.sigmoid(mm(c_gb + 256 * j, 256))


def _layer_block(layer, shape):
    return pl.BlockSpec((None,) + tuple(shape), lambda i: (layer,) + (0,) * len(shape))


def _inproj(x, g, w, layer, rope, nseq, tm):
    t, d = x.shape
    slen = t // nseq
    tps = slen // tm
    nrows = w.shape[1]
    row = lambda width: pl.BlockSpec((tm, width), lambda i: (i, 0))
    fixed = lambda shape: pl.BlockSpec(shape, lambda i: (0, 0))
    tab = pl.BlockSpec((tm, LANES), lambda i: (i % tps, 0))
    tr = lambda n, rows: pl.BlockSpec((n, 1, rows, tm), lambda i: (0, i // tps, 0, i % tps))
    sds = jax.ShapeDtypeStruct
    return pl.pallas_call(
        functools.partial(_inproj_body, d_model=d),
        grid=(t // tm,),
        in_specs=[row(d), fixed((1, d)), _layer_block(layer, (nrows, d)), tab, tab, tab],
        out_specs=[row(ATT_DIM), row(2 * KV_DIM), row(2 * KV_DIM), tr(2, N_KV * VT_ROWS), tr(6, KV_DIM), row(N_KV * LANES),
                   row(CONV_DIM), row(CONV_DIM), row(d), row(d)],
        out_shape=[sds((t, ATT_DIM), F32), sds((t, 2 * KV_DIM), F32), sds((t, 2 * KV_DIM), BF16),
                   sds((2, nseq, N_KV * VT_ROWS, slen), BF16), sds((6, nseq, KV_DIM, slen), F32), sds((t, N_KV * LANES), F32),
                   sds((t, CONV_DIM), F32), sds((t, CONV_DIM), F32), sds((t, d), F32), sds((t, d), F32)],
        compiler_params=_params(("arbitrary",)),
    )(x, g, w, *rope)


CHUNK = CMP_STRIDE * KV_DIM
HID2 = N_KV * CMP_HIDDEN


def _cmp_partials(src, pe_ref, w_ref, ab_ref):
    n = src.shape[0] // CMP_STRIDE
    acc_a = jnp.zeros((n, HID2), F32)
    acc_b = jnp.zeros((n, HID2), F32)
    for jp in range(CMP_STRIDE // 2):
        x = jnp.concatenate([src[pl.ds(2 * jp, n, stride=CMP_STRIDE), :],
                             src[pl.ds(2 * jp + 1, n, stride=CMP_STRIDE), :]], axis=1)
        cols = slice(2 * KV_DIM * jp, 2 * KV_DIM * (jp + 1))
        acc_a = acc_a + _nn((x + pe_ref[0:1, cols]).astype(BF16), w_ref[0, cols, :])
        acc_b = acc_b + _nn((x + pe_ref[1:2, cols]).astype(BF16), w_ref[1, cols, :])
    ab_ref[:, :HID2] = acc_a
    ab_ref[:, HID2:] = acc_b


def _cmp_ab_rows_body(x_ref, pe_ref, w_ref, ab_ref):
    _cmp_partials(x_ref, pe_ref, w_ref, ab_ref)


def _cmp_ab_rows(x, rows, col, pe, w, tr):
    return pl.pallas_call(
        _cmp_ab_rows_body,
        grid=(rows // tr,),
        in_specs=[pl.BlockSpec((tr, KV_DIM), lambda i: (i, col)),
                  pl.BlockSpec((2, CHUNK), lambda i: (0, 0)),
                  pl.BlockSpec((2, CHUNK, HID2), lambda i: (0, 0, 0))],
        out_specs=pl.BlockSpec((tr // CMP_STRIDE, 2 * HID2), lambda i: (i, 0)),
        out_shape=jax.ShapeDtypeStruct((rows // CMP_STRIDE, 2 * HID2), F32),
        compiler_params=_params(("arbitrary",)),
    )(x, pe, w)


def _cmp_ab_pages_body(x_ref, pe_ref, w_ref, ab_ref, rows_scr):
    page = x_ref.shape[3]
    for p in range(x_ref.shape[1]):
        rows_scr[page * p:page * (p + 1), :] = x_ref[0, p].T
    _cmp_partials(rows_scr, pe_ref, w_ref, ab_ref)


def _cmp_ab_pages(pool, layer, pe, w, pp):
    _, n_pool, _, page = pool.shape
    cpp = page // CMP_STRIDE
    return pl.pallas_call(
        _cmp_ab_pages_body,
        grid=(n_pool // pp,),
        in_specs=[pl.BlockSpec((1, pp, KV_DIM, page), lambda i: (layer, i, 0, 0)),
                  pl.BlockSpec((2, CHUNK), lambda i: (0, 0)),
                  pl.BlockSpec((2, CHUNK, HID2), lambda i: (0, 0, 0))],
        out_specs=pl.BlockSpec((pp * cpp, 2 * HID2), lambda i: (i, 0)),
        out_shape=jax.ShapeDtypeStruct((n_pool * cpp, 2 * HID2), F32),
        scratch_shapes=[pltpu.VMEM((pp * page, KV_DIM), F32)],
        compiler_params=_params(("arbitrary",)),
    )(pool, pe, w)


def _cmp_fin_body(abk_ref, abv_ref, w2k_ref, w2v_ref, ok_ref, ov_ref, *, nch):
    rows = abk_ref.shape[0]
    valid = lax.broadcasted_iota(jnp.int32, (rows, 1), 0) % nch < nch - 1

    def fin(ab_ref, w2_ref, o_ref):
        ab = ab_ref[...]
        hid = ab[:, :HID2] + pltpu.roll(ab[:, HID2:], rows - 1, 0)
        act = hid * jax.nn.sigmoid(hid)
        o_ref[...] = jnp.where(valid, _nn(act.astype(BF16), w2_ref[...]), 0.0)

    fin(abk_ref, w2k_ref, ok_ref)
    fin(abv_ref, w2v_ref, ov_ref)


def _cmp_fin(abk, abv, w2k, w2v, nch, tr):
    rows = abk.shape[0]
    ab_spec = pl.BlockSpec((tr, 2 * HID2), lambda i: (i, 0))
    w_spec = pl.BlockSpec((HID2, KV_DIM), lambda i: (0, 0))
    o_spec = pl.BlockSpec((tr, KV_DIM), lambda i: (i, 0))
    return pl.pallas_call(
        functools.partial(_cmp_fin_body, nch=nch),
        grid=(rows // tr,),
        in_specs=[ab_spec, ab_spec, w_spec, w_spec],
        out_specs=[o_spec, o_spec],
        out_shape=[jax.ShapeDtypeStruct((rows, KV_DIM), F32)] * 2,
        compiler_params=_params(("arbitrary",)),
    )(abk, abv, w2k, w2v)


def _select_bias(imp_t, qpos, ntop):
    shape = imp_t.shape
    s_io = lax.broadcasted_iota(jnp.int32, shape, 0)
    cur = qpos // SEL_BLOCK
    avail = s_io <= cur
    forced = (s_io == 0) | (avail & (s_io > cur - N_LOCAL_BLOCKS))
    val = jnp.where(avail, jnp.where(forced, FORCE_SCORE, imp_t), NEG)

    def pick_one(_, carry):
        val, sel = carry
        m = jnp.max(val, axis=0, keepdims=True)
        idx = jnp.min(jnp.where(val == m, s_io, NS_PAD), axis=0, keepdims=True)
        pick = s_io == idx
        return jnp.where(pick, -jnp.inf, val), jnp.where(pick, 1.0, sel)

    _, sel = lax.fori_loop(0, ntop, pick_one, (val, jnp.zeros(shape, F32)), unroll=True)
    return jnp.where((sel > 0.0) & avail, 0.0, NEG)


def _pattn_body(q_ref, g_ref, kc_ref, vct_ref, ks_ref, kw_ref, vst_ref, vwt_ref, ovt_ref, o_ref, bias_scr, *, seq):
    kv = pl.program_id(1)
    i = pl.program_id(2)
    nch = seq // CMP_STRIDE
    cols = GROUP * Q_TILE
    q = q_ref[...]
    qt = jnp.concatenate([q[:, :LANES].T, q[:, LANES:].T], axis=0)
    qt4 = jnp.concatenate([qt[HEAD_DIM * h:HEAD_DIM * (h + 1)] for h in range(GROUP)], axis=1)
    row_kv = lax.broadcasted_iota(jnp.int32, (KV_DIM, 1), 0) // HEAD_DIM
    qt_pad = jnp.where(row_kv == kv, jnp.concatenate([qt4] * N_KV, axis=0) * LOG2E, 0.0).astype(BF16)
    qpos1 = i * Q_TILE + lax.broadcasted_iota(jnp.int32, (1, Q_TILE), 1)
    qpos = jnp.concatenate([qpos1] * GROUP, axis=1)

    n_below = (i * Q_TILE) // SEL_GROUP
    w0 = pl.multiple_of(jnp.maximum(i * Q_TILE - WINDOW, 0), Q_TILE)
    s = _nn(kc_ref[0], qt_pad)
    s_win = _nn(kw_ref[pl.ds(w0, WIN_SPAN), :], qt_pad)
    s_diag = _nn(ks_ref[pl.ds(pl.multiple_of(n_below * SEL_GROUP, SEL_GROUP), SEL_GROUP), :], qt_pad)

    cmask = lax.broadcasted_iota(jnp.int32, (nch, 1), 0) * CMP_STRIDE + (CMP_BLOCK - 1) <= qpos
    s = jnp.where(cmask, s, NEG)
    p = jnp.where(cmask, jnp.exp2(s - jnp.max(s, axis=0, keepdims=True)), 0.0)
    l = jnp.sum(p, axis=0, keepdims=True)
    pc = p * jnp.where(l > 0.0, 1.0 / l, 0.0)
    o_c = _nn(vct_ref[0], pc.astype(BF16))

    pcs = pc[:, :Q_TILE]
    for h in range(1, GROUP):
        pcs = pcs + pc[:, h * Q_TILE:(h + 1) * Q_TILE]
    ovt = ovt_ref[...]
    imp_t = sum(_nn(ovt, part) for part in _split3(pcs))
    bias_scr[...] = _select_bias(imp_t, qpos1, min(SEL_TOPK, seq // SEL_BLOCK))

    kpos = w0 + lax.broadcasted_iota(jnp.int32, (WIN_SPAN, 1), 0)
    s = jnp.where((kpos <= qpos) & (kpos > qpos - WINDOW), s_win, NEG)
    p = jnp.exp2(s - jnp.max(s, axis=0, keepdims=True))
    acc_w = _nn(vwt_ref[0, 0, :, pl.ds(w0, WIN_SPAN)], p.astype(BF16))
    o_w = acc_w[:HEAD_DIM] / acc_w[HEAD_DIM:HEAD_DIM + 1]

    def group(gj, nkeys, carry, s_all, causal):
        m, acc = carry
        g0 = pl.multiple_of(gj * nkeys, nkeys)
        if s_all is None:
            s_all = _nn(ks_ref[pl.ds(g0, nkeys), :], qt_pad)
        for c in range(nkeys // SEL_CHUNK):
            k0 = pl.multiple_of(g0 + SEL_CHUNK * c, SEL_CHUNK)
            blocks = []
            for r in range(SEL_CHUNK // SEL_BLOCK):
                row = SEL_CHUNK * c + SEL_BLOCK * r
                s_blk = s_all[row:row + SEL_BLOCK]
                if causal:
                    kpos = k0 + SEL_BLOCK * r + lax.broadcasted_iota(jnp.int32, (SEL_BLOCK, 1), 0)
                    s_blk = jnp.where(kpos <= qpos, s_blk, NEG)
                b_row = bias_scr[pl.ds(gj * (nkeys // SEL_BLOCK) + row // SEL_BLOCK, 1), :]
                blocks.append((s_blk, jnp.concatenate([b_row] * GROUP, axis=1)))
            m_new = m
            for s_blk, bias in blocks:
                m_new = jnp.maximum(m_new, jnp.max(s_blk, axis=0, keepdims=True) + bias)
            alpha = jnp.exp2(m - m_new)
            p = jnp.concatenate([jnp.exp2(s_blk - (m_new - bias)) for s_blk, bias in blocks], axis=0)
            acc = alpha * acc + _nn(vst_ref[0, 0, :, pl.ds(k0, SEL_CHUNK)], p.astype(BF16))
            m = m_new
        return m, acc

    init = (jnp.full((1, cols), -3e38, F32), jnp.zeros((VT_ROWS, cols), F32))
    carry = group(n_below, SEL_GROUP, init, s_diag, True)
    carry = lax.fori_loop(0, n_below // 2, lambda gj, c: group(gj, 2 * SEL_GROUP, c, None, False), carry)
    _, acc_s = lax.cond(n_below % 2 == 1, lambda c: group(n_below - 1, SEL_GROUP, c, None, False), lambda c: c, carry)
    o_s = acc_s[:HEAD_DIM] / acc_s[HEAD_DIM:HEAD_DIM + 1]

    gt = g_ref[...].T
    outs = []
    for h in range(GROUP):
        c = slice(h * Q_TILE, (h + 1) * Q_TILE)
        outs.append(gt[3 * h:3 * h + 1] * o_c[:, c] + gt[3 * h + 1:3 * h + 2] * o_s[:, c]
                    + gt[3 * h + 2:3 * h + 3] * o_w[:, c])
    for t in range(GROUP // 2):
        o_ref[:, LANES * t:LANES * (t + 1)] = jnp.concatenate(outs[2 * t:2 * t + 2], axis=0).T


def _pattn(q, gates, kcmp, vcmp_t, kskw, vt, ovt, nb, seq):
    nq = seq // Q_TILE
    nch = seq // CMP_STRIDE
    qspec = pl.BlockSpec((Q_TILE, GROUP * HEAD_DIM), lambda b, k, i: (b * nq + i, k))
    gspec = pl.BlockSpec((Q_TILE, LANES), lambda b, k, i: (b * nq + i, k))
    kspec = lambda which: pl.BlockSpec((seq, KV_DIM), lambda b, k, i: (b, which))
    vspec = lambda which: pl.BlockSpec((1, 1, VT_ROWS, seq), lambda b, k, i: (which, b, k, 0))
    return pl.pallas_call(
        functools.partial(_pattn_body, seq=seq),
        grid=(nb, N_KV, nq),
        in_specs=[qspec, gspec,
                  pl.BlockSpec((1, nch, KV_DIM), lambda b, k, i: (b, 0, 0)),
                  pl.BlockSpec((1, HEAD_DIM, nch), lambda b, k, i: (b, k, 0)),
                  kspec(0), kspec(1), vspec(0), vspec(1),
                  pl.BlockSpec((NS_PAD, nch), lambda b, k, i: (0, 0))],
        out_specs=qspec,
        out_shape=jax.ShapeDtypeStruct((nb * seq, ATT_DIM), F32),
        scratch_shapes=[pltpu.VMEM((NS_PAD, Q_TILE), F32)],
        compiler_params=_params(("arbitrary", "arbitrary", "arbitrary")),
    )(q, gates, kcmp, vcmp_t, kskw, kskw, vt, vt, ovt)


def _sattn_body(pages_ref, qp_ref, g_ref, w2k_ref, w2v_ref, ks_hbm, vs_hbm, abk_hbm, abv_hbm, kw_ref, vw_ref,
                ksn_ref, vsn_ref, kwn_ref, vwn_ref, ovt_ref, gsum_ref, et_ref, en_ref, o_ref,
                kbuf, vbuf, abk_buf, abv_buf, sems, *, past, dec, ntop, slc_row0):
    b = pl.program_id(0)
    n_pages = kbuf.shape[1]
    nchp = n_pages * abk_buf.shape[2]
    lw = kw_ref.shape[3]

    def page_copies(seq_idx, slot):
        streams = ((ks_hbm, kbuf, slc_row0), (vs_hbm, vbuf, slc_row0), (abk_hbm, abk_buf, 0), (abv_hbm, abv_buf, 0))
        return [pltpu.make_async_copy(src.at[row0 + pages_ref[seq_idx, pg]], buf.at[slot, pg], sems.at[a, slot])
                for a, (src, buf, row0) in enumerate(streams) for pg in range(n_pages)]

    slot = b % 2

    @pl.when(b == 0)
    def _():
        for cp in page_copies(0, 0):
            cp.start()

    @pl.when(b + 1 < pl.num_programs(0))
    def _():
        for cp in page_copies(b + 1, 1 - slot):
            cp.start()

    qp = qp_ref[0]
    nrow = qp.shape[0]
    tok_r = lax.broadcasted_iota(jnp.int32, (nrow, 1), 0) % dec
    tok_l = lax.broadcasted_iota(jnp.int32, (1, LANES), 1) % dec
    new_j = lax.broadcasted_iota(jnp.int32, (1, LANES), 1)
    new_mask = (new_j < dec) & (new_j <= tok_r)

    def pad_rows(x):
        return jnp.concatenate([x.astype(F32), jnp.zeros((LANES - dec, KV_DIM), F32)], axis=0).astype(BF16)

    def softmax2(s_p, s_n):
        m = jnp.maximum(jnp.max(s_p, axis=-1, keepdims=True), jnp.max(s_n, axis=-1, keepdims=True))
        p_p, p_n = jnp.exp(s_p - m), jnp.exp(s_n - m)
        l = jnp.sum(p_p, axis=-1, keepdims=True) + jnp.sum(p_n, axis=-1, keepdims=True)
        return p_p.astype(BF16), p_n.astype(BF16), l

    def summaries(ab_buf, w2_ref):
        ab = jnp.concatenate([ab_buf[slot, pg] for pg in range(n_pages)], axis=0)
        hid = ab[:, :HID2] + pltpu.roll(ab[:, HID2:], nchp - 1, 0)
        out = _nn((hid * jax.nn.sigmoid(hid)).astype(BF16), w2_ref[...])
        return jnp.where(lax.broadcasted_iota(jnp.int32, (nchp, 1), 0) < nchp - 1, out, 0.0)

    sw_p = _nn(qp, kw_ref[0, 0].astype(BF16))
    sw_n = _nt(qp, pad_rows(kwn_ref[0]))
    for cp in page_copies(b, slot):
        cp.wait()
    kc, vc = summaries(abk_buf, w2k_ref), summaries(abv_buf, w2v_ref)
    s = _nn(qp, kc.T.astype(BF16))
    kt = jnp.concatenate([kbuf[slot, pg] for pg in range(n_pages)], axis=1).astype(BF16)
    ss_p = _nn(qp, kt)
    ss_n = _nt(qp, pad_rows(ksn_ref[0]))

    cmask = lax.broadcasted_iota(jnp.int32, (1, nchp), 1) < nchp - 1
    s = jnp.where(cmask, s, NEG)
    p = jnp.where(cmask, jnp.exp(s - jnp.max(s, axis=-1, keepdims=True)), 0.0)
    pc = p / jnp.sum(p, axis=-1, keepdims=True)
    o_c = _nn(pc.astype(BF16), vc.astype(BF16))

    gsum = gsum_ref[...]
    pcs = sum(_nn(gsum, part) for part in _split3(pc))
    pcs = jnp.concatenate([pcs, jnp.zeros((LANES - nrow, nchp), F32)], axis=0)
    ovt = ovt_ref[...]
    imp_t = sum(_nt(ovt, part) for part in _split3(pcs))
    bias = _select_bias(imp_t, past + tok_l, ntop).T[:nrow].astype(BF16)

    kpos = (past - lw) + lax.broadcasted_iota(jnp.int32, (1, lw), 1)
    p_p, p_n, l = softmax2(jnp.where(kpos > past + tok_r - WINDOW, sw_p, NEG), jnp.where(new_mask, sw_n, NEG))
    o_w = (_nt(p_p, vw_ref[0, 0].astype(BF16)) + _nn(p_n, pad_rows(vwn_ref[0]))) / l

    vt = jnp.concatenate([vbuf[slot, pg] for pg in range(n_pages)], axis=1).astype(BF16)
    s_p = ss_p + _nn(bias, et_ref[...])
    s_n = jnp.where(new_mask, ss_n + _nt(bias, en_ref[...]), NEG)
    p_p, p_n, l = softmax2(s_p, s_n)
    o_s = (_nt(p_p, vt) + _nn(p_n, pad_rows(vsn_ref[0]))) / l

    g = g_ref[0]
    o_ref[0] = g[:, 0:1] * o_c + g[:, 1:2] * o_s + g[:, 2:3] * o_w


def _sattn(pages, qp, g, w2k, w2v, ks, vs, abk, abv, win_k, win_v, layer, ksn, vsn, kwn, vwn, ovt, gsum, e_t, e_new,
           past, dec):
    nseq, nrow = qp.shape[:2]
    n_pages = pages.shape[1]
    page = ks.shape[2]
    n_pool, cpp = abk.shape[:2]
    nchp = n_pages * cpp
    lw = win_k.shape[3]
    ntop = min(SEL_TOPK, -(-(past + dec) // SEL_BLOCK))
    seq3 = lambda n, w: pl.BlockSpec((1, n, w), lambda b, pt: (b, 0, 0))
    fixed = lambda r, c: pl.BlockSpec((r, c), lambda b, pt: (0, 0))
    hbm = pl.BlockSpec(memory_space=pl.ANY)
    win = pl.BlockSpec((1, 1, KV_DIM, lw), lambda b, pt: (layer, b, 0, 0))
    return pl.pallas_call(
        functools.partial(_sattn_body, past=past, dec=dec, ntop=ntop, slc_row0=layer * n_pool),
        grid_spec=pltpu.PrefetchScalarGridSpec(
            num_scalar_prefetch=1,
            grid=(nseq,),
            in_specs=[seq3(nrow, KV_DIM), seq3(nrow, SUBLANES), fixed(HID2, KV_DIM), fixed(HID2, KV_DIM),
                      hbm, hbm, hbm, hbm, win, win,
                      seq3(dec, KV_DIM), seq3(dec, KV_DIM), seq3(dec, KV_DIM), seq3(dec, KV_DIM),
                      fixed(NS_PAD, nchp), fixed(nrow, nrow), fixed(NS_PAD, past), fixed(LANES, NS_PAD)],
            out_specs=seq3(nrow, KV_DIM),
            scratch_shapes=[pltpu.VMEM((2, n_pages, KV_DIM, page), F32), pltpu.VMEM((2, n_pages, KV_DIM, page), F32),
                            pltpu.VMEM((2, n_pages, cpp, 2 * HID2), F32), pltpu.VMEM((2, n_pages, cpp, 2 * HID2), F32),
                            pltpu.SemaphoreType.DMA((4, 2))]),
        out_shape=jax.ShapeDtypeStruct((nseq, nrow, KV_DIM), F32),
        compiler_params=_params(("arbitrary",)),
    )(pages, qp, g, w2k, w2v, ks, vs, abk, abv, win_k, win_v, ksn, vsn, kwn, vwn, ovt, gsum, e_t, e_new)


def _win_shift_body(win_ref, new_ref, o_ref, *, dec):
    rows = win_ref.shape[1] * win_ref.shape[2]
    lw = win_ref.shape[3]
    shifted = pltpu.roll(win_ref[0].reshape(rows, lw), lw - dec, 1)
    o_ref[0, :, :, :lw - LANES] = shifted[:, :lw - LANES].reshape(o_ref.shape[1], o_ref.shape[2], lw - LANES)
    lane = lax.broadcasted_iota(jnp.int32, (1, LANES), 1)
    tail = jnp.where(lane >= LANES - dec, new_ref[0].reshape(rows, LANES), shifted[:, lw - LANES:])
    o_ref[0, :, :, lw - LANES:] = tail.reshape(o_ref.shape[1], o_ref.shape[2], LANES)


def _win_shift(win, new_tail, dec, nseq_blk):
    depth, n, _, lw = win.shape
    blk = lambda w: pl.BlockSpec((1, nseq_blk, KV_DIM, w), lambda l, i: (l, i, 0, 0))
    return pl.pallas_call(
        functools.partial(_win_shift_body, dec=dec),
        grid=(depth, n // nseq_blk),
        in_specs=[blk(lw), blk(LANES)],
        out_specs=blk(lw),
        out_shape=jax.ShapeDtypeStruct(win.shape, F32),
        compiler_params=_params(("arbitrary", "arbitrary")),
    )(win, new_tail)


def _mix_body(x_ref, o_ref, u_ref, uprev_ref, cb_ref, sga_ref, sgb_ref, s1_ref, s2_ref,
              cw_ref, wa_ref, wc_ref, wo_ref, y_ref, ubuf, *, tm, slen):
    i = pl.program_id(0)
    u = u_ref[...]
    ubuf[0:SUBLANES, :] = uprev_ref[...]
    ubuf[SUBLANES:, :] = u
    pos = (i * tm + lax.broadcasted_iota(jnp.int32, (tm, 1), 0)) % slen
    prev1 = jnp.where(pos >= 1, ubuf[SUBLANES - 1:SUBLANES - 1 + tm, :], s1_ref[...])
    prev2 = jnp.where(pos >= 2, ubuf[SUBLANES - 2:SUBLANES - 2 + tm, :], s2_ref[...])
    cw = cw_ref[...]
    y_conv = cw[0:1] * prev2 + cw[1:2] * prev1 + cw[2:3] * u
    a_br = _nn(o_ref[...].astype(BF16), wa_ref[...])
    c_br = _nn((cb_ref[...] * y_conv).astype(BF16), wc_ref[...])
    mix = sga_ref[...] * a_br + sgb_ref[...] * c_br
    y_ref[...] = x_ref[...] + _nn(mix.astype(BF16), wo_ref[...])


def _mix(x, o_att, u, cb, sga, sgb, side1, side2, conv_w, wa, wc, wo, layer, tm, slen):
    t, d = x.shape
    row = lambda width: pl.BlockSpec((tm, width), lambda i: (i, 0))
    fixed = lambda shape: pl.BlockSpec(shape, lambda i: (0, 0))
    n_side = side1.shape[0] // tm
    side = pl.BlockSpec((tm, CONV_DIM), lambda i: (i % n_side, 0))
    prev = pl.BlockSpec((SUBLANES, CONV_DIM), lambda i: (jnp.maximum(i * (tm // SUBLANES) - 1, 0), 0))
    return pl.pallas_call(
        functools.partial(_mix_body, tm=tm, slen=slen),
        grid=(t // tm,),
        in_specs=[row(d), row(ATT_DIM), row(CONV_DIM), prev, row(CONV_DIM), row(d), row(d), side, side,
                  fixed((SUBLANES, CONV_DIM)), _layer_block(layer, (ATT_DIM, d)), _layer_block(layer, (CONV_DIM, d)),
                  _layer_block(layer, (d, d))],
        out_specs=row(d),
        out_shape=jax.ShapeDtypeStruct((t, d), F32),
        scratch_shapes=[pltpu.VMEM((tm + SUBLANES, CONV_DIM), F32)],
        compiler_params=_params(("arbitrary",)),
    )(x, o_att, u, u, cb, sga, sgb, side1, side2, conv_w, wa, wc, wo)


def _ffn_body(x_ref, g_ref, win_ref, wout_ref, gf_ref, y_ref, *, d_ff, fc, final):
    x = x_ref[...]
    h = x * lax.rsqrt(jnp.mean(x * x, axis=-1, keepdims=True) + EPS)
    hb = (h * g_ref[...]).astype(BF16)
    acc = x
    for c in range(d_ff // fc):
        gate = _nn(hb, win_ref[:, c * fc:(c + 1) * fc])
        up = _nn(hb, win_ref[:, d_ff + c * fc:d_ff + (c + 1) * fc])
        act = (gate * jax.nn.sigmoid(gate) * up).astype(BF16)
        acc = acc + _nn(act, wout_ref[c * fc:(c + 1) * fc, :])
    if final:
        acc = acc * lax.rsqrt(jnp.mean(acc * acc, axis=-1, keepdims=True) + EPS) * gf_ref[...]
    y_ref[...] = acc


def _ffn(x, g, w_in, w_out, g_final, layer, tm, final):
    t, d = x.shape
    d_ff = w_out.shape[1]
    fc = 256 if d_ff % 256 == 0 else LANES
    row = pl.BlockSpec((tm, d), lambda i: (i, 0))
    fixed = lambda shape: pl.BlockSpec(shape, lambda i: (0, 0))
    return pl.pallas_call(
        functools.partial(_ffn_body, d_ff=d_ff, fc=fc, final=final),
        grid=(t // tm,),
        in_specs=[row, fixed((1, d)), _layer_block(layer, (d, 2 * d_ff)), _layer_block(layer, (d_ff, d)), fixed((1, d))],
        out_specs=row,
        out_shape=jax.ShapeDtypeStruct((t, d), F32),
        compiler_params=_params(("arbitrary",)),
    )(x, g, w_in, w_out, g_final)


def _rope_tables(pos):
    inv = 1.0 / (ROPE_THETA ** (jnp.arange(ROPE_HALF, dtype=F32) * (2.0 / ROPE_DIM)))
    ang = pos.astype(F32)[:, None] * inv[None, :]
    cos, sin = jnp.cos(ang), jnp.sin(ang)
    t = pos.shape[0]
    pad = jnp.zeros((t, HEAD_DIM - ROPE_DIM), F32)
    rc = jnp.concatenate([cos, cos, pad + 1.0], axis=1)
    rs1 = jnp.concatenate([-sin, jnp.zeros_like(sin), pad], axis=1)
    rs2 = jnp.concatenate([jnp.zeros_like(sin), sin, pad], axis=1)
    rep = LANES // HEAD_DIM
    return tuple(jnp.tile(a, (1, rep)) for a in (rc, rs1, rs2))


def _pack_w_in(w):
    wt = w.transpose(0, 2, 1)
    depth, _, d = wt.shape
    o_gl = ATT_DIM + 6 * KV_DIM
    gl = wt[:, o_gl:o_gl + 3 * N_HEADS].reshape(depth, N_KV, 3 * GROUP, d)
    gl = jnp.pad(gl, ((0, 0), (0, 0), (0, LANES - 3 * GROUP), (0, 0))).reshape(depth, N_KV * LANES, d)
    return jnp.concatenate([wt[:, :o_gl], gl, wt[:, o_gl + 3 * N_HEADS:]], axis=1).astype(BF16)


def _pack_cmp(pe, w1, w2):
    eye = jnp.eye(N_KV, dtype=F32)
    pe2 = jnp.broadcast_to(pe.reshape(2, CMP_STRIDE, 1, HEAD_DIM), (2, CMP_STRIDE, N_KV, HEAD_DIM)).reshape(2, CHUNK)
    w1r = w1.reshape(2, CMP_STRIDE, HEAD_DIM, CMP_HIDDEN)
    w1p = jnp.einsum('pjdh,kc->pjkdch', w1r, eye).reshape(2, CHUNK, HID2).astype(BF16)
    w2p = jnp.einsum('hd,kc->khcd', w2, eye).reshape(HID2, KV_DIM).astype(BF16)
    return pe2, w1p, w2p


def _overlap_t(nch):
    ci = jnp.arange(nch)[None, :] * CMP_STRIDE
    sj = jnp.arange(NS_PAD)[:, None] * SEL_BLOCK
    return ((ci <= sj + SEL_BLOCK - 1) & (ci + CMP_BLOCK - 1 >= sj)).astype(BF16)


def _dim_major(a):
    lead = a.shape[:-3]
    n = len(lead)
    return a.transpose(*range(n), n + 1, n + 2, n).reshape(*lead, KV_DIM, a.shape[-3])


def _pos_major(a):
    lead = a.shape[:-2]
    n = len(lead)
    return a.reshape(*lead, N_KV, HEAD_DIM, a.shape[-1]).transpose(*range(n), n + 2, n, n + 1)


def _prompt_attention(q, gates, kcvc, kskw, vt, cmpk, cmpv, nb, seq):
    nch = seq // CMP_STRIDE
    abk = _cmp_ab_rows(kcvc, nb * seq, 0, cmpk[0], cmpk[1], seq)
    abv = _cmp_ab_rows(kcvc, nb * seq, 1, cmpv[0], cmpv[1], seq)
    kcmp, vcmp = _cmp_fin(abk, abv, cmpk[2], cmpv[2], nch, nch)
    kcmp = kcmp.reshape(nb, nch, KV_DIM).astype(BF16)
    vcmp_t = vcmp.reshape(nb, nch, KV_DIM).transpose(0, 2, 1).astype(BF16)
    return _pattn(q, gates, kcmp, vcmp_t, kskw, vt, _overlap_t(nch), nb, seq)


def _sample_cache(cmpk, cmpv, pool_k, pool_v, layer):
    n_pool, _, page = pool_k.shape[1:]
    cpp = page // CMP_STRIDE
    pp = math.gcd(n_pool, POOL_PAGES)
    return (_cmp_ab_pages(pool_k, layer, cmpk[0], cmpk[1], pp).reshape(n_pool, cpp, 2 * HID2),
            _cmp_ab_pages(pool_v, layer, cmpv[0], cmpv[1], pp).reshape(n_pool, cpp, 2 * HID2))


def _sample_attention(q, gates, new4, cache, w2k, w2v, slc_k, slc_v, page_table, win_k, win_v, layer, dec):
    abk, abv = cache
    ndb, n_pages = page_table.shape
    n_pool, _, page = slc_k.shape[1:]
    past = n_pages * page
    nchp = past // CMP_STRIDE
    pages = page_table.astype(jnp.int32)
    nrow = N_KV * GROUP * dec
    eye_kv = jnp.eye(N_KV, dtype=F32)
    q_s = q.reshape(ndb, dec, N_KV, GROUP, HEAD_DIM).transpose(0, 2, 3, 1, 4)
    qp = (q_s[:, :, :, :, None] * eye_kv[None, :, None, None, :, None]).reshape(ndb, nrow, KV_DIM).astype(BF16)
    g_s = gates.reshape(ndb, dec, N_KV, LANES)[..., :3 * GROUP].reshape(ndb, dec, N_KV, GROUP, 3)
    g_s = jnp.pad(g_s.transpose(0, 2, 3, 1, 4).reshape(ndb, nrow, 3), ((0, 0), (0, 0), (0, SUBLANES - 3)))
    e_t = jax.nn.one_hot(jnp.arange(past) // SEL_BLOCK, NS_PAD, dtype=BF16).T
    e_new = jax.nn.one_hot((past + jnp.arange(LANES)) // SEL_BLOCK, NS_PAD, dtype=BF16)
    rid = jnp.arange(nrow)
    gsum = ((rid[:, None] // (GROUP * dec) == rid[None, :] // (GROUP * dec))
            & (rid[:, None] % dec == rid[None, :] % dec)).astype(BF16)
    ksn, vsn, kwn, vwn = (a.reshape(ndb, dec, KV_DIM) for a in new4)
    o_t = _sattn(pages, qp, g_s, w2k, w2v, slc_k.reshape(-1, KV_DIM, page), slc_v.reshape(-1, KV_DIM, page),
                 abk, abv, win_k, win_v, layer,
                 ksn, vsn, kwn, vwn, _overlap_t(nchp), gsum, e_t, e_new, past, dec)
    o_t = o_t.reshape(ndb, N_KV, GROUP, dec, N_KV, HEAD_DIM)
    return jnp.einsum('bkgtcd,kc->btkgd', o_t, eye_kv).reshape(ndb * dec, ATT_DIM)


def kernel(x_prompt, x_sample, cache_cmp_k, cache_cmp_v, cache_slc_k, cache_slc_v, state_win_k, state_win_v,
           state_conv, page_table, w_in, cmp_pe, cmp_w1, cmp_w2, w_att_out, conv_w, w_conv_out, w_o,
           norm_mix, norm_ffn, w_ffn_in, w_ffn_out, norm_final):
    nb, seq, d = x_prompt.shape
    ndb, dec, _ = x_sample.shape
    depth, n_pool, page = cache_cmp_k.shape[:3]
    n_pages = page_table.shape[1]
    past = n_pages * page
    lw = state_win_k.shape[2]
    tp, ts = nb * seq, ndb * dec
    tm_p, tm_s = math.gcd(ROW_TILE, seq), math.gcd(ROW_TILE, ts)
    nchp = past // CMP_STRIDE
    keep = min(WINDOW, seq)
    assert seq % SEL_GROUP == 0 and seq >= WIN_SPAN and seq // SEL_BLOCK <= NS_PAD
    assert dec <= SUBLANES and N_KV * GROUP * dec <= LANES and (N_KV * GROUP * dec) % (2 * SUBLANES) == 0
    assert tm_s % SUBLANES == 0 and tm_s % dec == 0
    assert (past + dec - CMP_BLOCK) // CMP_STRIDE + 1 == nchp - 1
    assert -(-(past + dec) // SEL_BLOCK) <= NS_PAD and past >= WINDOW and lw == WINDOW

    rope_p = _rope_tables(jnp.arange(seq, dtype=jnp.int32))
    rope_s = _rope_tables(jnp.tile(past + jnp.arange(dec, dtype=jnp.int32), ndb))
    pool_k, pool_v, slc_k, slc_v, win_k, win_v = (
        _dim_major(a) for a in (cache_cmp_k, cache_cmp_v, cache_slc_k, cache_slc_v, state_win_k, state_win_v))
    no_side = jnp.zeros((tm_p, CONV_DIM), F32)
    zero = jnp.zeros((ndb, dec - 1, CONV_DIM), F32)

    cmp_w = [[_pack_cmp(cmp_pe[l, a], cmp_w1[l, a], cmp_w2[l, a]) for a in range(2)] for l in range(depth)]
    caches = [_sample_cache(*cmp_w[l], pool_k, pool_v, l) for l in range(depth)]

    w_inp = _pack_w_in(w_in)
    w_mix = (w_att_out.astype(BF16), w_conv_out.astype(BF16), w_o.astype(BF16))
    w_ffn = (w_ffn_in.astype(BF16), w_ffn_out.astype(BF16), norm_final[None, :])

    xp, xs = x_prompt.reshape(tp, d), x_sample.reshape(ts, d)
    st_p = [[] for _ in range(7)]
    st_s = [[] for _ in range(7)]
    for l in range(depth):
        g_l = norm_mix[l][None, :]
        g_f = norm_ffn[l][None, :]
        cmpk, cmpv = cmp_w[l]
        cw = jnp.pad(conv_w[l], ((0, SUBLANES - CONV_WIDTH), (0, 0)))
        final = l == depth - 1

        q, kcvc, kskw, vt, stp, gates, u, cb, sga, sgb = _inproj(xp, g_l, w_inp, l, rope_p, nb, tm_p)
        o_att = _prompt_attention(q, gates, kcvc, kskw, vt, cmpk, cmpv, nb, seq)
        xp = _mix(xp, o_att, u, cb, sga, sgb, no_side, no_side, cw, *w_mix, l, tm_p, seq)
        xp = _ffn(xp, g_f, *w_ffn, l, tm_p, final)
        for j in range(4):
            st_p[j].append(stp[j])
        st_p[4].append(stp[4][:, :, seq - keep:])
        st_p[5].append(stp[5][:, :, seq - keep:])
        st_p[6].append(jnp.stack([u[(b + 1) * seq - (CONV_WIDTH - 1):(b + 1) * seq] for b in range(nb)]))

        q, kcvc, kskw, vt, sts, gates, u, cb, sga, sgb = _inproj(xs, g_l, w_inp, l, rope_s, 1, tm_s)
        new_t = sts[:, 0].reshape(6, KV_DIM, ndb, dec).transpose(0, 2, 1, 3)
        new_rows = lambda j: sts[j, 0].T
        o_att = _sample_attention(q, gates, [kskw[:, :KV_DIM], new_rows(3), kskw[:, KV_DIM:], new_rows(5)],
                                  caches[l], cmpk[2], cmpv[2], slc_k, slc_v, page_table, win_k, win_v, l, dec)
        cbuf = state_conv[l]
        side1 = jnp.concatenate([cbuf[:, 1:2], zero], axis=1).reshape(ts, CONV_DIM)
        side2 = jnp.concatenate([cbuf[:, 0:2], zero[:, 1:]], axis=1).reshape(ts, CONV_DIM)
        xs = _mix(xs, o_att, u, cb, sga, sgb, side1, side2, cw, *w_mix, l, tm_s, dec)
        xs = _ffn(xs, g_f, *w_ffn, l, tm_s, final)
        for j in range(6):
            st_s[j].append(new_t[j])
        st_s[6].append(jnp.concatenate([cbuf, u.reshape(ndb, dec, CONV_DIM)], axis=1)[:, -(CONV_WIDTH - 1):])

    kv_p = [_pos_major(jnp.stack(a)) for a in st_p[:6]]
    kv_s = [jnp.stack(a) for a in st_s[:6]]
    lane_pad = ((0, 0), (0, 0), (0, 0), (LANES - dec, 0))
    kv_s[4] = _win_shift(win_k, jnp.pad(kv_s[4], lane_pad), dec, math.gcd(ndb, SUBLANES))
    kv_s[5] = _win_shift(win_v, jnp.pad(kv_s[5], lane_pad), dec, math.gcd(ndb, SUBLANES))
    kv_s = [_pos_major(a) for a in kv_s]
    return (xp.reshape(nb, seq, d), xs.reshape(ndb, dec, d), *kv_p, jnp.stack(st_p[6]),
            *kv_s, jnp.stack(st_s[6]))
```

```python
import functools
import math

import jax
import jax.numpy as jnp
from jax import lax
from jax.experimental import pallas as pl
from jax.experimental.pallas import tpu as pltpu

N_HEADS = 8
HEAD_DIM = 64
N_KV = 2
GROUP = N_HEADS // N_KV
ATT_DIM = N_HEADS * HEAD_DIM
KV_DIM = N_KV * HEAD_DIM
ROPE_DIM = HEAD_DIM // 4
ROPE_HALF = ROPE_DIM // 2
ROPE_THETA = 500000.0
CMP_BLOCK = 32
CMP_STRIDE = 16
CMP_HIDDEN = 2 * HEAD_DIM
SEL_BLOCK = 64
SEL_TOPK = 16
N_LOCAL_BLOCKS = 2
WINDOW = 512
CONV_DIM = 512
CONV_WIDTH = 3
EPS = 1e-6
NEG = -1e30
FORCE_SCORE = 1e4

LANES = 128
SUBLANES = 8
Q_TILE = 256
SEL_CHUNK = 256
SEL_GROUP = 4 * SEL_CHUNK
LOG2E = 1.4426950408889634
VT_ROWS = HEAD_DIM + 16
WIN_SPAN = WINDOW + Q_TILE
ROW_TILE = 512
POOL_PAGES = 128
NS_PAD = LANES
VMEM_LIMIT = 56 * 1024 * 1024

C_Q = 0
C_KV = C_Q + ATT_DIM
C_GL = C_KV + 6 * KV_DIM
C_CU = C_GL + N_KV * LANES
C_CB = C_CU + CONV_DIM
C_CC = C_CB + CONV_DIM
F32 = jnp.float32
BF16 = jnp.bfloat16


def _nt(a, b):
    return lax.dot_general(a, b, (((1,), (1,)), ((), ())), preferred_element_type=F32)


def _nn(a, b):
    return jnp.dot(a, b, preferred_element_type=F32)


def _split3(x):
    hi = x.astype(BF16)
    r1 = x - hi.astype(F32)
    mid = r1.astype(BF16)
    lo = (r1 - mid.astype(F32)).astype(BF16)
    return hi, mid, lo


def _params(sem):
    return pltpu.CompilerParams(dimension_semantics=sem, vmem_limit_bytes=VMEM_LIMIT)


def _inproj_body(x_ref, g_ref, w_ref, rc_ref, rs1_ref, rs2_ref,
                 q_ref, kcvc_ref, kskw_ref, vt_ref, st_ref, gate_ref, u_ref, cb_ref, sga_ref, sgb_ref, *, d_model):
    x = x_ref[...]
    h = x * lax.rsqrt(jnp.mean(x * x, axis=-1, keepdims=True) + EPS)
    hb = (h * g_ref[...]).astype(BF16)
    rc, rs1, rs2 = rc_ref[...], rs1_ref[...], rs2_ref[...]

    def mm(lo, width):
        return _nt(hb, w_ref[lo:lo + width, :])

    def rope(z):
        return z * rc + pltpu.roll(z, LANES - ROPE_HALF, 1) * rs1 + pltpu.roll(z, ROPE_HALF, 1) * rs2

    for j in range(ATT_DIM // 256):
        z = mm(C_Q + 256 * j, 256)
        for t in range(2):
            q_ref[:, 256 * j + LANES * t:256 * j + LANES * (t + 1)] = (
                rope(z[:, LANES * t:LANES * (t + 1)]) * (HEAD_DIM ** -0.5))
    for j in range(3):
        z = mm(C_KV + 256 * j, 256)
        k, v = rope(z[:, :LANES]), z[:, LANES:]
        kt, vt = k.T, v.T
        st_ref[2 * j, 0] = kt
        st_ref[2 * j + 1, 0] = vt
        if j == 0:
            kcvc_ref[:, :LANES] = k
            kcvc_ref[:, LANES:] = v
        else:
            kskw_ref[:, LANES * (j - 1):LANES * j] = k.astype(BF16)
            ones_row = (lax.broadcasted_iota(jnp.int32, (VT_ROWS - HEAD_DIM, vt.shape[1]), 0) == 0).astype(BF16)
            for kvh in range(N_KV):
                vt_ref[j - 1, 0, VT_ROWS * kvh:VT_ROWS * kvh + HEAD_DIM] = vt[HEAD_DIM * kvh:HEAD_DIM * (kvh + 1)].astype(BF16)
                vt_ref[j - 1, 0, VT_ROWS * kvh + HEAD_DIM:VT_ROWS * (kvh + 1)] = ones_row
    gate_ref[...] = jax.nn.sigmoid(mm(C_GL, N_KV * LANES))
    for j in range(CONV_DIM // 256):
        u_ref[:, 256 * j:256 * (j + 1)] = mm(C_CC + 256 * j, 256) * mm(C_CU + 256 * j, 256)
        cb_ref[:, 256 * j:256 * (j + 1)] = mm(C_CB + 256 * j, 256)
    c_ga = C_CC + CONV_DIM
    c_gb = c_ga + d_model
    for j in range(d_model // 256):
        sga_ref[:, 256 * j:256 * (j + 1)] = jax.nn.sigmoid(mm(c_ga + 256 * j, 256))
        sgb_ref[:, 256 * j:256 * (j + 1)] = jax.nn.sigmoid(mm(c_gb + 256 * j, 256))


def _layer_block(layer, shape):
    return pl.BlockSpec((None,) + tuple(shape), lambda i: (layer,) + (0,) * len(shape))


def _inproj(x, g, w, layer, rope, nseq, tm):
    t, d = x.shape
    slen = t // nseq
    tps = slen // tm
    nrows = w.shape[1]
    row = lambda width: pl.BlockSpec((tm, width), lambda i: (i, 0))
    fixed = lambda shape: pl.BlockSpec(shape, lambda i: (0, 0))
    tab = pl.BlockSpec((tm, LANES), lambda i: (i % tps, 0))
    tr = lambda n, rows: pl.BlockSpec((n, 1, rows, tm), lambda i: (0, i // tps, 0, i % tps))
    sds = jax.ShapeDtypeStruct
    return pl.pallas_call(
        functools.partial(_inproj_body, d_model=d),
        grid=(t // tm,),
        in_specs=[row(d), fixed((1, d)), _layer_block(layer, (nrows, d)), tab, tab, tab],
        out_specs=[row(ATT_DIM), row(2 * KV_DIM), row(2 * KV_DIM), tr(2, N_KV * VT_ROWS), tr(6, KV_DIM), row(N_KV * LANES),
                   row(CONV_DIM), row(CONV_DIM), row(d), row(d)],
        out_shape=[sds((t, ATT_DIM), F32), sds((t, 2 * KV_DIM), F32), sds((t, 2 * KV_DIM), BF16),
                   sds((2, nseq, N_KV * VT_ROWS, slen), BF16), sds((6, nseq, KV_DIM, slen), F32), sds((t, N_KV * LANES), F32),
                   sds((t, CONV_DIM), F32), sds((t, CONV_DIM), F32), sds((t, d), F32), sds((t, d), F32)],
        compiler_params=_params(("arbitrary",)),
    )(x, g, w, *rope)


CHUNK = CMP_STRIDE * KV_DIM
HID2 = N_KV * CMP_HIDDEN


def _cmp_partials(src, pe_ref, w_ref, ab_ref):
    n = src.shape[0] // CMP_STRIDE
    acc_a = jnp.zeros((n, HID2), F32)
    acc_b = jnp.zeros((n, HID2), F32)
    for jp in range(CMP_STRIDE // 2):
        x = jnp.concatenate([src[pl.ds(2 * jp, n, stride=CMP_STRIDE), :],
                             src[pl.ds(2 * jp + 1, n, stride=CMP_STRIDE), :]], axis=1)
        cols = slice(2 * KV_DIM * jp, 2 * KV_DIM * (jp + 1))
        acc_a = acc_a + _nn((x + pe_ref[0:1, cols]).astype(BF16), w_ref[0, cols, :])
        acc_b = acc_b + _nn((x + pe_ref[1:2, cols]).astype(BF16), w_ref[1, cols, :])
    ab_ref[:, :HID2] = acc_a
    ab_ref[:, HID2:] = acc_b


def _cmp_ab_rows_body(x_ref, pe_ref, w_ref, ab_ref):
    _cmp_partials(x_ref, pe_ref, w_ref, ab_ref)


def _cmp_ab_rows(x, rows, col, pe, w, tr):
    return pl.pallas_call(
        _cmp_ab_rows_body,
        grid=(rows // tr,),
        in_specs=[pl.BlockSpec((tr, KV_DIM), lambda i: (i, col)),
                  pl.BlockSpec((2, CHUNK), lambda i: (0, 0)),
                  pl.BlockSpec((2, CHUNK, HID2), lambda i: (0, 0, 0))],
        out_specs=pl.BlockSpec((tr // CMP_STRIDE, 2 * HID2), lambda i: (i, 0)),
        out_shape=jax.ShapeDtypeStruct((rows // CMP_STRIDE, 2 * HID2), F32),
        compiler_params=_params(("arbitrary",)),
    )(x, pe, w)


def _cmp_ab_pages_body(x_ref, pe_ref, w_ref, ab_ref, rows_scr):
    page = x_ref.shape[3]
    for p in range(x_ref.shape[1]):
        rows_scr[page * p:page * (p + 1), :] = x_ref[0, p].T
    _cmp_partials(rows_scr, pe_ref, w_ref, ab_ref)


def _cmp_ab_pages(pool, layer, pe, w, pp):
    _, n_pool, _, page = pool.shape
    cpp = page // CMP_STRIDE
    return pl.pallas_call(
        _cmp_ab_pages_body,
        grid=(n_pool // pp,),
        in_specs=[pl.BlockSpec((1, pp, KV_DIM, page), lambda i: (layer, i, 0, 0)),
                  pl.BlockSpec((2, CHUNK), lambda i: (0, 0)),
                  pl.BlockSpec((2, CHUNK, HID2), lambda i: (0, 0, 0))],
        out_specs=pl.BlockSpec((pp * cpp, 2 * HID2), lambda i: (i, 0)),
        out_shape=jax.ShapeDtypeStruct((n_pool * cpp, 2 * HID2), F32),
        scratch_shapes=[pltpu.VMEM((pp * page, KV_DIM), F32)],
        compiler_params=_params(("arbitrary",)),
    )(pool, pe, w)


def _cmp_fin_body(abk_ref, abv_ref, w2k_ref, w2v_ref, ok_ref, ov_ref, *, nch):
    rows = abk_ref.shape[0]
    valid = lax.broadcasted_iota(jnp.int32, (rows, 1), 0) % nch < nch - 1

    def fin(ab_ref, w2_ref, o_ref):
        ab = ab_ref[...]
        hid = ab[:, :HID2] + pltpu.roll(ab[:, HID2:], rows - 1, 0)
        act = hid * jax.nn.sigmoid(hid)
        o_ref[...] = jnp.where(valid, _nn(act.astype(BF16), w2_ref[...]), 0.0)

    fin(abk_ref, w2k_ref, ok_ref)
    fin(abv_ref, w2v_ref, ov_ref)


def _cmp_fin(abk, abv, w2k, w2v, nch, tr):
    rows = abk.shape[0]
    ab_spec = pl.BlockSpec((tr, 2 * HID2), lambda i: (i, 0))
    w_spec = pl.BlockSpec((HID2, KV_DIM), lambda i: (0, 0))
    o_spec = pl.BlockSpec((tr, KV_DIM), lambda i: (i, 0))
    return pl.pallas_call(
        functools.partial(_cmp_fin_body, nch=nch),
        grid=(rows // tr,),
        in_specs=[ab_spec, ab_spec, w_spec, w_spec],
        out_specs=[o_spec, o_spec],
        out_shape=[jax.ShapeDtypeStruct((rows, KV_DIM), F32)] * 2,
        compiler_params=_params(("arbitrary",)),
    )(abk, abv, w2k, w2v)


def _select_bias(imp_t, qpos, ntop):
    shape = imp_t.shape
    s_io = lax.broadcasted_iota(jnp.int32, shape, 0)
    cur = qpos // SEL_BLOCK
    avail = s_io <= cur
    forced = (s_io == 0) | (avail & (s_io > cur - N_LOCAL_BLOCKS))
    val = jnp.where(avail, jnp.where(forced, FORCE_SCORE, imp_t), NEG)

    def pick_one(_, carry):
        val, sel = carry
        m = jnp.max(val, axis=0, keepdims=True)
        idx = jnp.min(jnp.where(val == m, s_io, NS_PAD), axis=0, keepdims=True)
        pick = s_io == idx
        return jnp.where(pick, -jnp.inf, val), jnp.where(pick, 1.0, sel)

    _, sel = lax.fori_loop(0, ntop, pick_one, (val, jnp.zeros(shape, F32)), unroll=True)
    return jnp.where((sel > 0.0) & avail, 0.0, NEG)


def _pattn_body(q_ref, g_ref, kc_ref, vct_ref, ks_ref, kw_ref, vst_ref, vwt_ref, ovt_ref, o_ref, bias_scr, *, seq):
    kv = pl.program_id(1)
    i = pl.program_id(2)
    nch = seq // CMP_STRIDE
    cols = GROUP * Q_TILE
    q = q_ref[...]
    qt = jnp.concatenate([q[:, :LANES].T, q[:, LANES:].T], axis=0)
    qt4 = jnp.concatenate([qt[HEAD_DIM * h:HEAD_DIM * (h + 1)] for h in range(GROUP)], axis=1)
    row_kv = lax.broadcasted_iota(jnp.int32, (KV_DIM, 1), 0) // HEAD_DIM
    qt_pad = jnp.where(row_kv == kv, jnp.concatenate([qt4] * N_KV, axis=0) * LOG2E, 0.0).astype(BF16)
    qpos1 = i * Q_TILE + lax.broadcasted_iota(jnp.int32, (1, Q_TILE), 1)
    qpos = jnp.concatenate([qpos1] * GROUP, axis=1)

    n_below = (i * Q_TILE) // SEL_GROUP
    w0 = pl.multiple_of(jnp.maximum(i * Q_TILE - WINDOW, 0), Q_TILE)
    s = _nn(kc_ref[0], qt_pad)
    s_win = _nn(kw_ref[pl.ds(w0, WIN_SPAN), :], qt_pad)
    s_diag = _nn(ks_ref[pl.ds(pl.multiple_of(n_below * SEL_GROUP, SEL_GROUP), SEL_GROUP), :], qt_pad)

    cmask = lax.broadcasted_iota(jnp.int32, (nch, 1), 0) * CMP_STRIDE + (CMP_BLOCK - 1) <= qpos
    s = jnp.where(cmask, s, NEG)
    p = jnp.where(cmask, jnp.exp2(s - jnp.max(s, axis=0, keepdims=True)), 0.0)
    l = jnp.sum(p, axis=0, keepdims=True)
    pc = p * jnp.where(l > 0.0, 1.0 / l, 0.0)
    o_c = _nn(vct_ref[0], pc.astype(BF16))

    pcs = pc[:, :Q_TILE]
    for h in range(1, GROUP):
        pcs = pcs + pc[:, h * Q_TILE:(h + 1) * Q_TILE]
    ovt = ovt_ref[...]
    imp_t = sum(_nn(ovt, part) for part in _split3(pcs))
    bias_scr[...] = _select_bias(imp_t, qpos1, min(SEL_TOPK, seq // SEL_BLOCK))

    kpos = w0 + lax.broadcasted_iota(jnp.int32, (WIN_SPAN, 1), 0)
    s = jnp.concatenate([
        jnp.where((kpos[:Q_TILE] <= qpos) & (kpos[:Q_TILE] > qpos - WINDOW), s_win[:Q_TILE], NEG),
        jnp.where(kpos[Q_TILE:] <= qpos, s_win[Q_TILE:], NEG)], axis=0)
    p = jnp.exp2(s - jnp.max(s, axis=0, keepdims=True))
    acc_w = _nn(vwt_ref[0, 0, :, pl.ds(w0, WIN_SPAN)], p.astype(BF16))
    o_w = acc_w[:HEAD_DIM] / acc_w[HEAD_DIM:HEAD_DIM + 1]

    def group(gj, nkeys, carry, s_all, causal):
        m, acc = carry
        g0 = pl.multiple_of(gj * nkeys, nkeys)
        if s_all is None:
            s_all = _nn(ks_ref[pl.ds(g0, nkeys), :], qt_pad)
        for c in range(nkeys // SEL_CHUNK):
            k0 = pl.multiple_of(g0 + SEL_CHUNK * c, SEL_CHUNK)
            blocks = []
            for r in range(SEL_CHUNK // SEL_BLOCK):
                row = SEL_CHUNK * c + SEL_BLOCK * r
                s_blk = s_all[row:row + SEL_BLOCK]
                if causal:
                    kpos = k0 + SEL_BLOCK * r + lax.broadcasted_iota(jnp.int32, (SEL_BLOCK, 1), 0)
                    s_blk = jnp.where(kpos <= qpos, s_blk, NEG)
                b_row = bias_scr[pl.ds(gj * (nkeys // SEL_BLOCK) + row // SEL_BLOCK, 1), :]
                blocks.append((s_blk, jnp.concatenate([b_row] * GROUP, axis=1)))
            m_new = m
            for s_blk, bias in blocks:
                m_new = jnp.maximum(m_new, jnp.max(s_blk, axis=0, keepdims=True) + bias)
            alpha = jnp.exp2(m - m_new)
            p = jnp.concatenate([jnp.exp2(s_blk - (m_new - bias)) for s_blk, bias in blocks], axis=0)
            acc = alpha * acc + _nn(vst_ref[0, 0, :, pl.ds(k0, SEL_CHUNK)], p.astype(BF16))
            m = m_new
        return m, acc

    init = (jnp.full((1, cols), -3e38, F32), jnp.zeros((VT_ROWS, cols), F32))
    carry = group(n_below, SEL_GROUP, init, s_diag, True)
    carry = lax.fori_loop(0, n_below // 2, lambda gj, c: group(gj, 2 * SEL_GROUP, c, None, False), carry)
    _, acc_s = lax.cond(n_below % 2 == 1, lambda c: group(n_below - 1, SEL_GROUP, c, None, False), lambda c: c, carry)
    o_s = acc_s[:HEAD_DIM] / acc_s[HEAD_DIM:HEAD_DIM + 1]

    gt = g_ref[...].T
    outs = []
    for h in range(GROUP):
        c = slice(h * Q_TILE, (h + 1) * Q_TILE)
        outs.append(gt[3 * h:3 * h + 1] * o_c[:, c] + gt[3 * h + 1:3 * h + 2] * o_s[:, c]
                    + gt[3 * h + 2:3 * h + 3] * o_w[:, c])
    for t in range(GROUP // 2):
        o_ref[:, LANES * t:LANES * (t + 1)] = jnp.concatenate(outs[2 * t:2 * t + 2], axis=0).T


def _pattn(q, gates, kcmp, vcmp_t, kskw, vt, ovt, nb, seq):
    nq = seq // Q_TILE
    nch = seq // CMP_STRIDE
    qspec = pl.BlockSpec((Q_TILE, GROUP * HEAD_DIM), lambda b, k, i: (b * nq + i, k))
    gspec = pl.BlockSpec((Q_TILE, LANES), lambda b, k, i: (b * nq + i, k))
    kspec = lambda which: pl.BlockSpec((seq, KV_DIM), lambda b, k, i: (b, which))
    vspec = lambda which: pl.BlockSpec((1, 1, VT_ROWS, seq), lambda b, k, i: (which, b, k, 0))
    return pl.pallas_call(
        functools.partial(_pattn_body, seq=seq),
        grid=(nb, N_KV, nq),
        in_specs=[qspec, gspec,
                  pl.BlockSpec((1, nch, KV_DIM), lambda b, k, i: (b, 0, 0)),
                  pl.BlockSpec((1, HEAD_DIM, nch), lambda b, k, i: (b, k, 0)),
                  kspec(0), kspec(1), vspec(0), vspec(1),
                  pl.BlockSpec((NS_PAD, nch), lambda b, k, i: (0, 0))],
        out_specs=qspec,
        out_shape=jax.ShapeDtypeStruct((nb * seq, ATT_DIM), F32),
        scratch_shapes=[pltpu.VMEM((NS_PAD, Q_TILE), F32)],
        compiler_params=_params(("arbitrary", "arbitrary", "arbitrary")),
    )(q, gates, kcmp, vcmp_t, kskw, kskw, vt, vt, ovt)


def _sattn_body(pages_ref, qp_ref, g_ref, w2k_ref, w2v_ref, ks_hbm, vs_hbm, abk_hbm, abv_hbm, kw_ref, vw_ref,
                ksn_ref, vsn_ref, kwn_ref, vwn_ref, ovt_ref, gsum_ref, et_ref, en_ref, o_ref,
                kbuf, vbuf, abk_buf, abv_buf, sems, *, past, dec, ntop, slc_row0):
    b = pl.program_id(0)
    n_pages = kbuf.shape[1]
    nchp = n_pages * abk_buf.shape[2]
    lw = kw_ref.shape[3]

    def page_copies(seq_idx, slot):
        streams = ((ks_hbm, kbuf, slc_row0), (vs_hbm, vbuf, slc_row0), (abk_hbm, abk_buf, 0), (abv_hbm, abv_buf, 0))
        return [pltpu.make_async_copy(src.at[row0 + pages_ref[seq_idx, pg]], buf.at[slot, pg], sems.at[a, slot])
                for a, (src, buf, row0) in enumerate(streams) for pg in range(n_pages)]

    slot = b % 2

    @pl.when(b == 0)
    def _():
        for cp in page_copies(0, 0):
            cp.start()

    last = pl.num_programs(0) - 1
    nxt = jnp.minimum(b + 1, last)
    for cp in page_copies(nxt, 1 - slot):
        cp.start()

    qp = qp_ref[0]
    nrow = qp.shape[0]
    tok_r = lax.broadcasted_iota(jnp.int32, (nrow, 1), 0) % dec
    tok_l = lax.broadcasted_iota(jnp.int32, (1, LANES), 1) % dec
    new_j = lax.broadcasted_iota(jnp.int32, (1, LANES), 1)
    new_mask = (new_j < dec) & (new_j <= tok_r)

    def pad_rows(x):
        return jnp.concatenate([x.astype(F32), jnp.zeros((LANES - dec, KV_DIM), F32)], axis=0).astype(BF16)

    def softmax2(s_p, s_n):
        m = jnp.maximum(jnp.max(s_p, axis=-1, keepdims=True), jnp.max(s_n, axis=-1, keepdims=True))
        p_p, p_n = jnp.exp(s_p - m), jnp.exp(s_n - m)
        l = jnp.sum(p_p, axis=-1, keepdims=True) + jnp.sum(p_n, axis=-1, keepdims=True)
        return p_p.astype(BF16), p_n.astype(BF16), l

    def summaries(ab_buf, w2_ref):
        ab = jnp.concatenate([ab_buf[slot, pg] for pg in range(n_pages)], axis=0)
        hid = ab[:, :HID2] + pltpu.roll(ab[:, HID2:], nchp - 1, 0)
        out = _nn((hid * jax.nn.sigmoid(hid)).astype(BF16), w2_ref[...])
        return jnp.where(lax.broadcasted_iota(jnp.int32, (nchp, 1), 0) < nchp - 1, out, 0.0)

    sw_p = _nn(qp, kw_ref[0, 0].astype(BF16))
    sw_n = _nt(qp, pad_rows(kwn_ref[0]))
    for cp in page_copies(b, slot):
        cp.wait()
    kc, vc = summaries(abk_buf, w2k_ref), summaries(abv_buf, w2v_ref)
    s = _nn(qp, kc.T.astype(BF16))
    kt = jnp.concatenate([kbuf[slot, pg] for pg in range(n_pages)], axis=1).astype(BF16)
    ss_p = _nn(qp, kt)
    ss_n = _nt(qp, pad_rows(ksn_ref[0]))

    cmask = lax.broadcasted_iota(jnp.int32, (1, nchp), 1) < nchp - 1
    s = jnp.where(cmask, s, NEG)
    p = jnp.where(cmask, jnp.exp(s - jnp.max(s, axis=-1, keepdims=True)), 0.0)
    pc = p / jnp.sum(p, axis=-1, keepdims=True)
    o_c = _nn(pc.astype(BF16), vc.astype(BF16))

    gsum = gsum_ref[...]
    pcs = sum(_nn(gsum, part) for part in _split3(pc))
    pcs = jnp.concatenate([pcs, jnp.zeros((LANES - nrow, nchp), F32)], axis=0)
    ovt = ovt_ref[...]
    imp_t = sum(_nt(ovt, part) for part in _split3(pcs))
    bias = _select_bias(imp_t, past + tok_l, ntop).T[:nrow].astype(BF16)

    kpos = (past - lw) + lax.broadcasted_iota(jnp.int32, (1, lw), 1)
    p_p, p_n, l = softmax2(jnp.where(kpos > past + tok_r - WINDOW, sw_p, NEG), jnp.where(new_mask, sw_n, NEG))
    o_w = (_nt(p_p, vw_ref[0, 0].astype(BF16)) + _nn(p_n, pad_rows(vwn_ref[0]))) / l

    vt = jnp.concatenate([vbuf[slot, pg] for pg in range(n_pages)], axis=1).astype(BF16)
    s_p = ss_p + _nn(bias, et_ref[...])
    s_n = jnp.where(new_mask, ss_n + _nt(bias, en_ref[...]), NEG)
    p_p, p_n, l = softmax2(s_p, s_n)
    o_s = (_nt(p_p, vt) + _nn(p_n, pad_rows(vsn_ref[0]))) / l

    g = g_ref[0]
    o_ref[0] = g[:, 0:1] * o_c + g[:, 1:2] * o_s + g[:, 2:3] * o_w

    @pl.when(b == last)
    def _():
        for cp in page_copies(nxt, 1 - slot):
            cp.wait()


def _sattn(pages, qp, g, w2k, w2v, ks, vs, abk, abv, win_k, win_v, layer, ksn, vsn, kwn, vwn, ovt, gsum, e_t, e_new,
           past, dec):
    nseq, nrow = qp.shape[:2]
    n_pages = pages.shape[1]
    page = ks.shape[2]
    n_pool, cpp = abk.shape[:2]
    nchp = n_pages * cpp
    lw = win_k.shape[3]
    ntop = min(SEL_TOPK, -(-(past + dec) // SEL_BLOCK))
    seq3 = lambda n, w: pl.BlockSpec((1, n, w), lambda b, pt: (b, 0, 0))
    fixed = lambda r, c: pl.BlockSpec((r, c), lambda b, pt: (0, 0))
    hbm = pl.BlockSpec(memory_space=pl.ANY)
    win = pl.BlockSpec((1, 1, KV_DIM, lw), lambda b, pt: (layer, b, 0, 0))
    return pl.pallas_call(
        functools.partial(_sattn_body, past=past, dec=dec, ntop=ntop, slc_row0=layer * n_pool),
        grid_spec=pltpu.PrefetchScalarGridSpec(
            num_scalar_prefetch=1,
            grid=(nseq,),
            in_specs=[seq3(nrow, KV_DIM), seq3(nrow, SUBLANES), fixed(HID2, KV_DIM), fixed(HID2, KV_DIM),
                      hbm, hbm, hbm, hbm, win, win,
                      seq3(dec, KV_DIM), seq3(dec, KV_DIM), seq3(dec, KV_DIM), seq3(dec, KV_DIM),
                      fixed(NS_PAD, nchp), fixed(nrow, nrow), fixed(NS_PAD, past), fixed(LANES, NS_PAD)],
            out_specs=seq3(nrow, KV_DIM),
            scratch_shapes=[pltpu.VMEM((2, n_pages, KV_DIM, page), F32), pltpu.VMEM((2, n_pages, KV_DIM, page), F32),
                            pltpu.VMEM((2, n_pages, cpp, 2 * HID2), F32), pltpu.VMEM((2, n_pages, cpp, 2 * HID2), F32),
                            pltpu.SemaphoreType.DMA((4, 2))]),
        out_shape=jax.ShapeDtypeStruct((nseq, nrow, KV_DIM), F32),
        compiler_params=_params(("arbitrary",)),
    )(pages, qp, g, w2k, w2v, ks, vs, abk, abv, win_k, win_v, ksn, vsn, kwn, vwn, ovt, gsum, e_t, e_new)


def _win_shift_body(win_ref, new_ref, o_ref, *, dec):
    rows = win_ref.shape[1] * win_ref.shape[2]
    lw = win_ref.shape[3]
    shifted = pltpu.roll(win_ref[0].reshape(rows, lw), lw - dec, 1)
    o_ref[0, :, :, :lw - LANES] = shifted[:, :lw - LANES].reshape(o_ref.shape[1], o_ref.shape[2], lw - LANES)
    lane = lax.broadcasted_iota(jnp.int32, (1, LANES), 1)
    tail = jnp.where(lane >= LANES - dec, new_ref[0].reshape(rows, LANES), shifted[:, lw - LANES:])
    o_ref[0, :, :, lw - LANES:] = tail.reshape(o_ref.shape[1], o_ref.shape[2], LANES)


def _win_shift(win, new_tail, dec, nseq_blk):
    depth, n, _, lw = win.shape
    blk = lambda w: pl.BlockSpec((1, nseq_blk, KV_DIM, w), lambda l, i: (l, i, 0, 0))
    return pl.pallas_call(
        functools.partial(_win_shift_body, dec=dec),
        grid=(depth, n // nseq_blk),
        in_specs=[blk(lw), blk(LANES)],
        out_specs=blk(lw),
        out_shape=jax.ShapeDtypeStruct(win.shape, F32),
        compiler_params=_params(("arbitrary", "arbitrary")),
    )(win, new_tail)


def _mix_body(x_ref, o_ref, u_ref, uprev_ref, cb_ref, sga_ref, sgb_ref, s1_ref, s2_ref,
              cw_ref, wa_ref, wc_ref, wo_ref, y_ref, ubuf, *, tm, slen):
    i = pl.program_id(0)
    u = u_ref[...]
    ubuf[0:SUBLANES, :] = uprev_ref[...]
    ubuf[SUBLANES:, :] = u
    pos = (i * tm + lax.broadcasted_iota(jnp.int32, (tm, 1), 0)) % slen
    prev1 = jnp.where(pos >= 1, ubuf[SUBLANES - 1:SUBLANES - 1 + tm, :], s1_ref[...])
    prev2 = jnp.where(pos >= 2, ubuf[SUBLANES - 2:SUBLANES - 2 + tm, :], s2_ref[...])
    cw = cw_ref[...]
    y_conv = cw[0:1] * prev2 + cw[1:2] * prev1 + cw[2:3] * u
    a_br = _nn(o_ref[...].astype(BF16), wa_ref[...])
    c_br = _nn((cb_ref[...] * y_conv).astype(BF16), wc_ref[...])
    mix = sga_ref[...] * a_br + sgb_ref[...] * c_br
    y_ref[...] = x_ref[...] + _nn(mix.astype(BF16), wo_ref[...])


def _mix(x, o_att, u, cb, sga, sgb, side1, side2, conv_w, wa, wc, wo, layer, tm, slen):
    t, d = x.shape
    row = lambda width: pl.BlockSpec((tm, width), lambda i: (i, 0))
    fixed = lambda shape: pl.BlockSpec(shape, lambda i: (0, 0))
    n_side = side1.shape[0] // tm
    side = pl.BlockSpec((tm, CONV_DIM), lambda i: (i % n_side, 0))
    prev = pl.BlockSpec((SUBLANES, CONV_DIM), lambda i: (jnp.maximum(i * (tm // SUBLANES) - 1, 0), 0))
    return pl.pallas_call(
        functools.partial(_mix_body, tm=tm, slen=slen),
        grid=(t // tm,),
        in_specs=[row(d), row(ATT_DIM), row(CONV_DIM), prev, row(CONV_DIM), row(d), row(d), side, side,
                  fixed((SUBLANES, CONV_DIM)), _layer_block(layer, (ATT_DIM, d)), _layer_block(layer, (CONV_DIM, d)),
                  _layer_block(layer, (d, d))],
        out_specs=row(d),
        out_shape=jax.ShapeDtypeStruct((t, d), F32),
        scratch_shapes=[pltpu.VMEM((tm + SUBLANES, CONV_DIM), F32)],
        compiler_params=_params(("arbitrary",)),
    )(x, o_att, u, u, cb, sga, sgb, side1, side2, conv_w, wa, wc, wo)


def _ffn_body(x_ref, g_ref, win_ref, wout_ref, gf_ref, y_ref, *, d_ff, fc, final):
    x = x_ref[...]
    h = x * lax.rsqrt(jnp.mean(x * x, axis=-1, keepdims=True) + EPS)
    hb = (h * g_ref[...]).astype(BF16)
    acc = x
    for c in range(d_ff // fc):
        gate = _nn(hb, win_ref[:, c * fc:(c + 1) * fc])
        up = _nn(hb, win_ref[:, d_ff + c * fc:d_ff + (c + 1) * fc])
        act = (gate * jax.nn.sigmoid(gate) * up).astype(BF16)
        acc = acc + _nn(act, wout_ref[c * fc:(c + 1) * fc, :])
    if final:
        acc = acc * lax.rsqrt(jnp.mean(acc * acc, axis=-1, keepdims=True) + EPS) * gf_ref[...]
    y_ref[...] = acc


def _ffn(x, g, w_in, w_out, g_final, layer, tm, final):
    t, d = x.shape
    d_ff = w_out.shape[1]
    fc = 256 if d_ff % 256 == 0 else LANES
    row = pl.BlockSpec((tm, d), lambda i: (i, 0))
    fixed = lambda shape: pl.BlockSpec(shape, lambda i: (0, 0))
    return pl.pallas_call(
        functools.partial(_ffn_body, d_ff=d_ff, fc=fc, final=final),
        grid=(t // tm,),
        in_specs=[row, fixed((1, d)), _layer_block(layer, (d, 2 * d_ff)), _layer_block(layer, (d_ff, d)), fixed((1, d))],
        out_specs=row,
        out_shape=jax.ShapeDtypeStruct((t, d), F32),
        compiler_params=_params(("arbitrary",)),
    )(x, g, w_in, w_out, g_final)


def _rope_tables(pos):
    inv = 1.0 / (ROPE_THETA ** (jnp.arange(ROPE_HALF, dtype=F32) * (2.0 / ROPE_DIM)))
    ang = pos.astype(F32)[:, None] * inv[None, :]
    cos, sin = jnp.cos(ang), jnp.sin(ang)
    t = pos.shape[0]
    pad = jnp.zeros((t, HEAD_DIM - ROPE_DIM), F32)
    rc = jnp.concatenate([cos, cos, pad + 1.0], axis=1)
    rs1 = jnp.concatenate([-sin, jnp.zeros_like(sin), pad], axis=1)
    rs2 = jnp.concatenate([jnp.zeros_like(sin), sin, pad], axis=1)
    rep = LANES // HEAD_DIM
    return tuple(jnp.tile(a, (1, rep)) for a in (rc, rs1, rs2))


def _pack_w_in(w):
    wt = w.transpose(0, 2, 1)
    depth, _, d = wt.shape
    o_gl = ATT_DIM + 6 * KV_DIM
    gl = wt[:, o_gl:o_gl + 3 * N_HEADS].reshape(depth, N_KV, 3 * GROUP, d)
    gl = jnp.pad(gl, ((0, 0), (0, 0), (0, LANES - 3 * GROUP), (0, 0))).reshape(depth, N_KV * LANES, d)
    return jnp.concatenate([wt[:, :o_gl], gl, wt[:, o_gl + 3 * N_HEADS:]], axis=1).astype(BF16)


def _pack_cmp(pe, w1, w2):
    eye = jnp.eye(N_KV, dtype=F32)
    pe2 = jnp.broadcast_to(pe.reshape(2, CMP_STRIDE, 1, HEAD_DIM), (2, CMP_STRIDE, N_KV, HEAD_DIM)).reshape(2, CHUNK)
    w1r = w1.reshape(2, CMP_STRIDE, HEAD_DIM, CMP_HIDDEN)
    w1p = jnp.einsum('pjdh,kc->pjkdch', w1r, eye).reshape(2, CHUNK, HID2).astype(BF16)
    w2p = jnp.einsum('hd,kc->khcd', w2, eye).reshape(HID2, KV_DIM).astype(BF16)
    return pe2, w1p, w2p


def _overlap_t(nch):
    ci = jnp.arange(nch)[None, :] * CMP_STRIDE
    sj = jnp.arange(NS_PAD)[:, None] * SEL_BLOCK
    return ((ci <= sj + SEL_BLOCK - 1) & (ci + CMP_BLOCK - 1 >= sj)).astype(BF16)


def _dim_major(a):
    lead = a.shape[:-3]
    n = len(lead)
    return a.transpose(*range(n), n + 1, n + 2, n).reshape(*lead, KV_DIM, a.shape[-3])


def _pos_major(a):
    lead = a.shape[:-2]
    n = len(lead)
    return a.reshape(*lead, N_KV, HEAD_DIM, a.shape[-1]).transpose(*range(n), n + 2, n, n + 1)


def _prompt_attention(q, gates, kcvc, kskw, vt, cmpk, cmpv, nb, seq):
    nch = seq // CMP_STRIDE
    abk = _cmp_ab_rows(kcvc, nb * seq, 0, cmpk[0], cmpk[1], seq)
    abv = _cmp_ab_rows(kcvc, nb * seq, 1, cmpv[0], cmpv[1], seq)
    kcmp, vcmp = _cmp_fin(abk, abv, cmpk[2], cmpv[2], nch, nch)
    kcmp = kcmp.reshape(nb, nch, KV_DIM).astype(BF16)
    vcmp_t = vcmp.reshape(nb, nch, KV_DIM).transpose(0, 2, 1).astype(BF16)
    return _pattn(q, gates, kcmp, vcmp_t, kskw, vt, _overlap_t(nch), nb, seq)


def _sample_cache(cmpk, cmpv, pool_k, pool_v, layer):
    n_pool, _, page = pool_k.shape[1:]
    cpp = page // CMP_STRIDE
    pp = math.gcd(n_pool, POOL_PAGES)
    return (_cmp_ab_pages(pool_k, layer, cmpk[0], cmpk[1], pp).reshape(n_pool, cpp, 2 * HID2),
            _cmp_ab_pages(pool_v, layer, cmpv[0], cmpv[1], pp).reshape(n_pool, cpp, 2 * HID2))


def _sample_attention(q, gates, new4, cache, w2k, w2v, slc_k, slc_v, page_table, win_k, win_v, layer, dec):
    abk, abv = cache
    ndb, n_pages = page_table.shape
    n_pool, _, page = slc_k.shape[1:]
    past = n_pages * page
    nchp = past // CMP_STRIDE
    pages = page_table.astype(jnp.int32)
    nrow = N_KV * GROUP * dec
    eye_kv = jnp.eye(N_KV, dtype=F32)
    q_s = q.reshape(ndb, dec, N_KV, GROUP, HEAD_DIM).transpose(0, 2, 3, 1, 4)
    qp = (q_s[:, :, :, :, None] * eye_kv[None, :, None, None, :, None]).reshape(ndb, nrow, KV_DIM).astype(BF16)
    g_s = gates.reshape(ndb, dec, N_KV, LANES)[..., :3 * GROUP].reshape(ndb, dec, N_KV, GROUP, 3)
    g_s = jnp.pad(g_s.transpose(0, 2, 3, 1, 4).reshape(ndb, nrow, 3), ((0, 0), (0, 0), (0, SUBLANES - 3)))
    e_t = jax.nn.one_hot(jnp.arange(past) // SEL_BLOCK, NS_PAD, dtype=BF16).T
    e_new = jax.nn.one_hot((past + jnp.arange(LANES)) // SEL_BLOCK, NS_PAD, dtype=BF16)
    rid = jnp.arange(nrow)
    gsum = ((rid[:, None] // (GROUP * dec) == rid[None, :] // (GROUP * dec))
            & (rid[:, None] % dec == rid[None, :] % dec)).astype(BF16)
    ksn, vsn, kwn, vwn = (a.reshape(ndb, dec, KV_DIM) for a in new4)
    o_t = _sattn(pages, qp, g_s, w2k, w2v, slc_k.reshape(-1, KV_DIM, page), slc_v.reshape(-1, KV_DIM, page),
                 abk, abv, win_k, win_v, layer,
                 ksn, vsn, kwn, vwn, _overlap_t(nchp), gsum, e_t, e_new, past, dec)
    o_t = o_t.reshape(ndb, N_KV, GROUP, dec, N_KV, HEAD_DIM)
    return jnp.einsum('bkgtcd,kc->btkgd', o_t, eye_kv).reshape(ndb * dec, ATT_DIM)


def kernel(x_prompt, x_sample, cache_cmp_k, cache_cmp_v, cache_slc_k, cache_slc_v, state_win_k, state_win_v,
           state_conv, page_table, w_in, cmp_pe, cmp_w1, cmp_w2, w_att_out, conv_w, w_conv_out, w_o,
           norm_mix, norm_ffn, w_ffn_in, w_ffn_out, norm_final):
    nb, seq, d = x_prompt.shape
    ndb, dec, _ = x_sample.shape
    depth, n_pool, page = cache_cmp_k.shape[:3]
    n_pages = page_table.shape[1]
    past = n_pages * page
    lw = state_win_k.shape[2]
    tp, ts = nb * seq, ndb * dec
    tm_p, tm_s = math.gcd(ROW_TILE, seq), math.gcd(ROW_TILE, ts)
    nchp = past // CMP_STRIDE
    keep = min(WINDOW, seq)
    assert seq % SEL_GROUP == 0 and seq >= WIN_SPAN and seq // SEL_BLOCK <= NS_PAD
    assert dec <= SUBLANES and N_KV * GROUP * dec <= LANES and (N_KV * GROUP * dec) % (2 * SUBLANES) == 0
    assert tm_s % SUBLANES == 0 and tm_s % dec == 0
    assert (past + dec - CMP_BLOCK) // CMP_STRIDE + 1 == nchp - 1
    assert -(-(past + dec) // SEL_BLOCK) <= NS_PAD and past >= WINDOW and lw == WINDOW

    rope_p = _rope_tables(jnp.arange(seq, dtype=jnp.int32))
    rope_s = _rope_tables(jnp.tile(past + jnp.arange(dec, dtype=jnp.int32), ndb))
    pool_k, pool_v, slc_k, slc_v, win_k, win_v = (
        _dim_major(a) for a in (cache_cmp_k, cache_cmp_v, cache_slc_k, cache_slc_v, state_win_k, state_win_v))
    no_side = jnp.zeros((tm_p, CONV_DIM), F32)
    zero = jnp.zeros((ndb, dec - 1, CONV_DIM), F32)

    cmp_w = [[_pack_cmp(cmp_pe[l, a], cmp_w1[l, a], cmp_w2[l, a]) for a in range(2)] for l in range(depth)]
    caches = [_sample_cache(*cmp_w[l], pool_k, pool_v, l) for l in range(depth)]

    w_inp = _pack_w_in(w_in)
    w_mix = (w_att_out.astype(BF16), w_conv_out.astype(BF16), w_o.astype(BF16))
    w_ffn = (w_ffn_in.astype(BF16), w_ffn_out.astype(BF16), norm_final[None, :])

    xp, xs = x_prompt.reshape(tp, d), x_sample.reshape(ts, d)
    st_p = [[] for _ in range(7)]
    st_s = [[] for _ in range(7)]
    for l in range(depth):
        g_l = norm_mix[l][None, :]
        g_f = norm_ffn[l][None, :]
        cmpk, cmpv = cmp_w[l]
        cw = jnp.pad(conv_w[l], ((0, SUBLANES - CONV_WIDTH), (0, 0)))
        final = l == depth - 1

        q, kcvc, kskw, vt, stp, gates, u, cb, sga, sgb = _inproj(xp, g_l, w_inp, l, rope_p, nb, tm_p)
        o_att = _prompt_attention(q, gates, kcvc, kskw, vt, cmpk, cmpv, nb, seq)
        xp = _mix(xp, o_att, u, cb, sga, sgb, no_side, no_side, cw, *w_mix, l, tm_p, seq)
        xp = _ffn(xp, g_f, *w_ffn, l, tm_p, final)
        for j in range(4):
            st_p[j].append(stp[j])
        st_p[4].append(stp[4][:, :, seq - keep:])
        st_p[5].append(stp[5][:, :, seq - keep:])
        st_p[6].append(jnp.stack([u[(b + 1) * seq - (CONV_WIDTH - 1):(b + 1) * seq] for b in range(nb)]))

        q, kcvc, kskw, vt, sts, gates, u, cb, sga, sgb = _inproj(xs, g_l, w_inp, l, rope_s, 1, tm_s)
        new_t = sts[:, 0].reshape(6, KV_DIM, ndb, dec).transpose(0, 2, 1, 3)
        new_rows = lambda j: sts[j, 0].T
        o_att = _sample_attention(q, gates, [kskw[:, :KV_DIM], new_rows(3), kskw[:, KV_DIM:], new_rows(5)],
                                  caches[l], cmpk[2], cmpv[2], slc_k, slc_v, page_table, win_k, win_v, l, dec)
        cbuf = state_conv[l]
        side1 = jnp.concatenate([cbuf[:, 1:2], zero], axis=1).reshape(ts, CONV_DIM)
        side2 = jnp.concatenate([cbuf[:, 0:2], zero[:, 1:]], axis=1).reshape(ts, CONV_DIM)
        xs = _mix(xs, o_att, u, cb, sga, sgb, side1, side2, cw, *w_mix, l, tm_s, dec)
        xs = _ffn(xs, g_f, *w_ffn, l, tm_s, final)
        for j in range(6):
            st_s[j].append(new_t[j])
        st_s[6].append(jnp.concatenate([cbuf, u.reshape(ndb, dec, CONV_DIM)], axis=1)[:, -(CONV_WIDTH - 1):])

    kv_p = [_pos_major(jnp.stack(a)) for a in st_p[:6]]
    kv_s = [jnp.stack(a) for a in st_s[:6]]
    lane_pad = ((0, 0), (0, 0), (0, 0), (LANES - dec, 0))
    kv_s[4] = _win_shift(win_k, jnp.pad(kv_s[4], lane_pad), dec, math.gcd(ndb, SUBLANES))
    kv_s[5] = _win_shift(win_v, jnp.pad(kv_s[5], lane_pad), dec, math.gcd(ndb, SUBLANES))
    kv_s = [_pos_major(a) for a in kv_s]
    return (xp.reshape(nb, seq, d), xs.reshape(ndb, dec, d), *kv_p, jnp.stack(st_p[6]),
            *kv_s, jnp.stack(st_s[6]))
```

```python
import functools
import math

import jax
import jax.numpy as jnp
from jax import lax
from jax.experimental import pallas as pl
from jax.experimental.pallas import tpu as pltpu

N_HEADS = 8
HEAD_DIM = 64
N_KV = 2
GROUP = N_HEADS // N_KV
ATT_DIM = N_HEADS * HEAD_DIM
KV_DIM = N_KV * HEAD_DIM
ROPE_DIM = HEAD_DIM // 4
ROPE_HALF = ROPE_DIM // 2
ROPE_THETA = 500000.0
CMP_BLOCK = 32
CMP_STRIDE = 16
CMP_HIDDEN = 2 * HEAD_DIM
SEL_BLOCK = 64
SEL_TOPK = 16
N_LOCAL_BLOCKS = 2
WINDOW = 512
CONV_DIM = 512
CONV_WIDTH = 3
EPS = 1e-6
NEG = -1e30
FORCE_SCORE = 1e4

LANES = 128
SUBLANES = 8
Q_TILE = 256
SEL_CHUNK = 256
SEL_GROUP = 4 * SEL_CHUNK
LOG2E = 1.4426950408889634
VT_ROWS = HEAD_DIM + 16
WIN_SPAN = WINDOW + Q_TILE
ROW_TILE = 512
POOL_PAGES = 128
NS_PAD = LANES
VMEM_LIMIT = 56 * 1024 * 1024

C_Q = 0
C_KV = C_Q + ATT_DIM
C_GL = C_KV + 6 * KV_DIM
C_CU = C_GL + N_KV * LANES
C_CB = C_CU + CONV_DIM
C_CC = C_CB + CONV_DIM
F32 = jnp.float32
BF16 = jnp.bfloat16


def _nt(a, b):
    return lax.dot_general(a, b, (((1,), (1,)), ((), ())), preferred_element_type=F32)


def _nn(a, b):
    return jnp.dot(a, b, preferred_element_type=F32)


def _split3(x):
    hi = x.astype(BF16)
    r1 = x - hi.astype(F32)
    mid = r1.astype(BF16)
    lo = (r1 - mid.astype(F32)).astype(BF16)
    return hi, mid, lo


def _params(sem):
    return pltpu.CompilerParams(dimension_semantics=sem, vmem_limit_bytes=VMEM_LIMIT)


def _inproj_body(x_ref, g_ref, w_ref, rc_ref, rs1_ref, rs2_ref,
                 q_ref, kcvc_ref, kskw_ref, vt_ref, st_ref, gate_ref, u_ref, cb_ref, sga_ref, sgb_ref, qt_ref, gt_ref,
                 *, d_model):
    x = x_ref[...]
    h = x * lax.rsqrt(jnp.mean(x * x, axis=-1, keepdims=True) + EPS)
    hb = (h * g_ref[...]).astype(BF16)
    rc, rs1, rs2 = rc_ref[...], rs1_ref[...], rs2_ref[...]

    def mm(lo, width):
        return _nt(hb, w_ref[lo:lo + width, :])

    def rope(z):
        return z * rc + pltpu.roll(z, LANES - ROPE_HALF, 1) * rs1 + pltpu.roll(z, ROPE_HALF, 1) * rs2

    for j in range(ATT_DIM // 256):
        z = mm(C_Q + 256 * j, 256)
        for t in range(2):
            lanes = slice(256 * j + LANES * t, 256 * j + LANES * (t + 1))
            q_tile = rope(z[:, LANES * t:LANES * (t + 1)]) * (HEAD_DIM ** -0.5)
            q_ref[:, lanes] = q_tile
            qt_ref[lanes, :] = q_tile.T
    for j in range(3):
        z = mm(C_KV + 256 * j, 256)
        k, v = rope(z[:, :LANES]), z[:, LANES:]
        kt, vt = k.T, v.T
        st_ref[2 * j, 0] = kt
        st_ref[2 * j + 1, 0] = vt
        if j == 0:
            kcvc_ref[:, :LANES] = k
            kcvc_ref[:, LANES:] = v
        else:
            kskw_ref[:, LANES * (j - 1):LANES * j] = k.astype(BF16)
            ones_row = (lax.broadcasted_iota(jnp.int32, (VT_ROWS - HEAD_DIM, vt.shape[1]), 0) == 0).astype(BF16)
            for kvh in range(N_KV):
                vt_ref[j - 1, 0, VT_ROWS * kvh:VT_ROWS * kvh + HEAD_DIM] = vt[HEAD_DIM * kvh:HEAD_DIM * (kvh + 1)].astype(BF16)
                vt_ref[j - 1, 0, VT_ROWS * kvh + HEAD_DIM:VT_ROWS * (kvh + 1)] = ones_row
    gates = jax.nn.sigmoid(mm(C_GL, N_KV * LANES))
    gate_ref[...] = gates
    for kvh in range(N_KV):
        gt_ref[LANES * kvh:LANES * (kvh + 1), :] = gates[:, LANES * kvh:LANES * (kvh + 1)].T
    for j in range(CONV_DIM // 256):
        u_ref[:, 256 * j:256 * (j + 1)] = mm(C_CC + 256 * j, 256) * mm(C_CU + 256 * j, 256)
        cb_ref[:, 256 * j:256 * (j + 1)] = mm(C_CB + 256 * j, 256)
    c_ga = C_CC + CONV_DIM
    c_gb = c_ga + d_model
    for j in range(d_model // 256):
        sga_ref[:, 256 * j:256 * (j + 1)] = jax.nn.sigmoid(mm(c_ga + 256 * j, 256))
        sgb_ref[:, 256 * j:256 * (j + 1)] = jax.nn.sigmoid(mm(c_gb + 256 * j, 256))


def _layer_block(layer, shape):
    return pl.BlockSpec((None,) + tuple(shape), lambda i: (layer,) + (0,) * len(shape))


def _inproj(x, g, w, layer, rope, nseq, tm):
    t, d = x.shape
    slen = t // nseq
    tps = slen // tm
    nrows = w.shape[1]
    row = lambda width: pl.BlockSpec((tm, width), lambda i: (i, 0))
    fixed = lambda shape: pl.BlockSpec(shape, lambda i: (0, 0))
    tab = pl.BlockSpec((tm, LANES), lambda i: (i % tps, 0))
    tr = lambda n, rows: pl.BlockSpec((n, 1, rows, tm), lambda i: (0, i // tps, 0, i % tps))
    sds = jax.ShapeDtypeStruct
    return pl.pallas_call(
        functools.partial(_inproj_body, d_model=d),
        grid=(t // tm,),
        in_specs=[row(d), fixed((1, d)), _layer_block(layer, (nrows, d)), tab, tab, tab],
        out_specs=[row(ATT_DIM), row(2 * KV_DIM), row(2 * KV_DIM), tr(2, N_KV * VT_ROWS), tr(6, KV_DIM), row(N_KV * LANES),
                   row(CONV_DIM), row(CONV_DIM), row(d), row(d),
                   pl.BlockSpec((ATT_DIM, tm), lambda i: (0, i)), pl.BlockSpec((N_KV * LANES, tm), lambda i: (0, i))],
        out_shape=[sds((t, ATT_DIM), F32), sds((t, 2 * KV_DIM), F32), sds((t, 2 * KV_DIM), BF16),
                   sds((2, nseq, N_KV * VT_ROWS, slen), BF16), sds((6, nseq, KV_DIM, slen), F32), sds((t, N_KV * LANES), F32),
                   sds((t, CONV_DIM), F32), sds((t, CONV_DIM), F32), sds((t, d), F32), sds((t, d), F32),
                   sds((ATT_DIM, t), F32), sds((N_KV * LANES, t), F32)],
        compiler_params=_params(("arbitrary",)),
    )(x, g, w, *rope)


CHUNK = CMP_STRIDE * KV_DIM
HID2 = N_KV * CMP_HIDDEN


def _cmp_partials(src, pe_ref, w_ref, ab_ref):
    n = src.shape[0] // CMP_STRIDE
    acc_a = jnp.zeros((n, HID2), F32)
    acc_b = jnp.zeros((n, HID2), F32)
    for jp in range(CMP_STRIDE // 2):
        x = jnp.concatenate([src[pl.ds(2 * jp, n, stride=CMP_STRIDE), :],
                             src[pl.ds(2 * jp + 1, n, stride=CMP_STRIDE), :]], axis=1)
        cols = slice(2 * KV_DIM * jp, 2 * KV_DIM * (jp + 1))
        acc_a = acc_a + _nn((x + pe_ref[0:1, cols]).astype(BF16), w_ref[0, cols, :])
        acc_b = acc_b + _nn((x + pe_ref[1:2, cols]).astype(BF16), w_ref[1, cols, :])
    ab_ref[:, :HID2] = acc_a
    ab_ref[:, HID2:] = acc_b


def _cmp_ab_rows_body(x_ref, pe_ref, w_ref, ab_ref):
    _cmp_partials(x_ref, pe_ref, w_ref, ab_ref)


def _cmp_ab_rows(x, rows, col, pe, w, tr):
    return pl.pallas_call(
        _cmp_ab_rows_body,
        grid=(rows // tr,),
        in_specs=[pl.BlockSpec((tr, KV_DIM), lambda i: (i, col)),
                  pl.BlockSpec((2, CHUNK), lambda i: (0, 0)),
                  pl.BlockSpec((2, CHUNK, HID2), lambda i: (0, 0, 0))],
        out_specs=pl.BlockSpec((tr // CMP_STRIDE, 2 * HID2), lambda i: (i, 0)),
        out_shape=jax.ShapeDtypeStruct((rows // CMP_STRIDE, 2 * HID2), F32),
        compiler_params=_params(("arbitrary",)),
    )(x, pe, w)


def _cmp_ab_pages_body(x_ref, pe_ref, w_ref, ab_ref, rows_scr):
    page = x_ref.shape[3]
    for p in range(x_ref.shape[1]):
        rows_scr[page * p:page * (p + 1), :] = x_ref[0, p].T
    _cmp_partials(rows_scr, pe_ref, w_ref, ab_ref)


def _cmp_ab_pages(pool, layer, pe, w, pp):
    _, n_pool, _, page = pool.shape
    cpp = page // CMP_STRIDE
    return pl.pallas_call(
        _cmp_ab_pages_body,
        grid=(n_pool // pp,),
        in_specs=[pl.BlockSpec((1, pp, KV_DIM, page), lambda i: (layer, i, 0, 0)),
                  pl.BlockSpec((2, CHUNK), lambda i: (0, 0)),
                  pl.BlockSpec((2, CHUNK, HID2), lambda i: (0, 0, 0))],
        out_specs=pl.BlockSpec((pp * cpp, 2 * HID2), lambda i: (i, 0)),
        out_shape=jax.ShapeDtypeStruct((n_pool * cpp, 2 * HID2), F32),
        scratch_shapes=[pltpu.VMEM((pp * page, KV_DIM), F32)],
        compiler_params=_params(("arbitrary",)),
    )(pool, pe, w)


def _cmp_fin_body(abk_ref, abv_ref, w2k_ref, w2v_ref, ok_ref, ov_ref, *, nch):
    rows = abk_ref.shape[0]
    valid = lax.broadcasted_iota(jnp.int32, (rows, 1), 0) % nch < nch - 1

    def fin(ab_ref, w2_ref, o_ref):
        ab = ab_ref[...]
        hid = ab[:, :HID2] + pltpu.roll(ab[:, HID2:], rows - 1, 0)
        act = hid * jax.nn.sigmoid(hid)
        o_ref[...] = jnp.where(valid, _nn(act.astype(BF16), w2_ref[...]), 0.0)

    fin(abk_ref, w2k_ref, ok_ref)
    fin(abv_ref, w2v_ref, ov_ref)


def _cmp_fin(abk, abv, w2k, w2v, nch, tr):
    rows = abk.shape[0]
    ab_spec = pl.BlockSpec((tr, 2 * HID2), lambda i: (i, 0))
    w_spec = pl.BlockSpec((HID2, KV_DIM), lambda i: (0, 0))
    o_spec = pl.BlockSpec((tr, KV_DIM), lambda i: (i, 0))
    return pl.pallas_call(
        functools.partial(_cmp_fin_body, nch=nch),
        grid=(rows // tr,),
        in_specs=[ab_spec, ab_spec, w_spec, w_spec],
        out_specs=[o_spec, o_spec],
        out_shape=[jax.ShapeDtypeStruct((rows, KV_DIM), F32)] * 2,
        compiler_params=_params(("arbitrary",)),
    )(abk, abv, w2k, w2v)


def _select_bias(imp_t, qpos, ntop):
    shape = imp_t.shape
    s_io = lax.broadcasted_iota(jnp.int32, shape, 0)
    cur = qpos // SEL_BLOCK
    avail = s_io <= cur
    forced = (s_io == 0) | (avail & (s_io > cur - N_LOCAL_BLOCKS))
    val = jnp.where(avail, jnp.where(forced, FORCE_SCORE, imp_t), NEG)

    def pick_one(_, carry):
        val, sel = carry
        m = jnp.max(val, axis=0, keepdims=True)
        idx = jnp.min(jnp.where(val == m, s_io, NS_PAD), axis=0, keepdims=True)
        pick = s_io == idx
        return jnp.where(pick, -jnp.inf, val), jnp.where(pick, 1.0, sel)

    _, sel = lax.fori_loop(0, ntop, pick_one, (val, jnp.zeros(shape, F32)), unroll=True)
    return jnp.where((sel > 0.0) & avail, 0.0, NEG)


def _pattn_body(q_ref, g_ref, kc_ref, vct_ref, ks_ref, kw_ref, vst_ref, vwt_ref, ovt_ref, o_ref, bias_scr, *, seq):
    kv = pl.program_id(1)
    i = pl.program_id(2)
    nch = seq // CMP_STRIDE
    cols = GROUP * Q_TILE
    qt = q_ref[...]
    qt4 = jnp.concatenate([qt[HEAD_DIM * h:HEAD_DIM * (h + 1)] for h in range(GROUP)], axis=1)
    row_kv = lax.broadcasted_iota(jnp.int32, (KV_DIM, 1), 0) // HEAD_DIM
    qt_pad = jnp.where(row_kv == kv, jnp.concatenate([qt4] * N_KV, axis=0) * LOG2E, 0.0).astype(BF16)
    qpos1 = i * Q_TILE + lax.broadcasted_iota(jnp.int32, (1, Q_TILE), 1)
    qpos = jnp.concatenate([qpos1] * GROUP, axis=1)

    n_below = (i * Q_TILE) // SEL_GROUP
    w0 = pl.multiple_of(jnp.maximum(i * Q_TILE - WINDOW, 0), Q_TILE)
    s = _nn(kc_ref[0], qt_pad)
    s_win = _nn(kw_ref[pl.ds(w0, WIN_SPAN), :], qt_pad)
    s_diag = _nn(ks_ref[pl.ds(pl.multiple_of(n_below * SEL_GROUP, SEL_GROUP), SEL_GROUP), :], qt_pad)

    cmask = lax.broadcasted_iota(jnp.int32, (nch, 1), 0) * CMP_STRIDE + (CMP_BLOCK - 1) <= qpos
    s = jnp.where(cmask, s, NEG)
    p = jnp.where(cmask, jnp.exp2(s - jnp.max(s, axis=0, keepdims=True)), 0.0)
    l = jnp.sum(p, axis=0, keepdims=True)
    pc = p * jnp.where(l > 0.0, 1.0 / l, 0.0)
    o_c = _nn(vct_ref[0], pc.astype(BF16))

    pcs = pc[:, :Q_TILE]
    for h in range(1, GROUP):
        pcs = pcs + pc[:, h * Q_TILE:(h + 1) * Q_TILE]
    ovt = ovt_ref[...]
    imp_t = sum(_nn(ovt, part) for part in _split3(pcs))
    bias_scr[...] = _select_bias(imp_t, qpos1, min(SEL_TOPK, seq // SEL_BLOCK))

    kpos = w0 + lax.broadcasted_iota(jnp.int32, (WIN_SPAN, 1), 0)
    s = jnp.concatenate([
        jnp.where((kpos[:Q_TILE] <= qpos) & (kpos[:Q_TILE] > qpos - WINDOW), s_win[:Q_TILE], NEG),
        jnp.where(kpos[Q_TILE:] <= qpos, s_win[Q_TILE:], NEG)], axis=0)
    p = jnp.exp2(s - jnp.max(s, axis=0, keepdims=True))
    acc_w = _nn(vwt_ref[0, 0, :, pl.ds(w0, WIN_SPAN)], p.astype(BF16))
    o_w = acc_w[:HEAD_DIM] / acc_w[HEAD_DIM:HEAD_DIM + 1]

    def group(gj, nkeys, carry, s_all, causal):
        m, acc = carry
        g0 = pl.multiple_of(gj * nkeys, nkeys)
        if s_all is None:
            s_all = _nn(ks_ref[pl.ds(g0, nkeys), :], qt_pad)
        for c in range(nkeys // SEL_CHUNK):
            k0 = pl.multiple_of(g0 + SEL_CHUNK * c, SEL_CHUNK)
            blocks = []
            for r in range(SEL_CHUNK // SEL_BLOCK):
                row = SEL_CHUNK * c + SEL_BLOCK * r
                s_blk = s_all[row:row + SEL_BLOCK]
                if causal:
                    kpos = k0 + SEL_BLOCK * r + lax.broadcasted_iota(jnp.int32, (SEL_BLOCK, 1), 0)
                    s_blk = jnp.where(kpos <= qpos, s_blk, NEG)
                b_row = bias_scr[pl.ds(gj * (nkeys // SEL_BLOCK) + row // SEL_BLOCK, 1), :]
                blocks.append((s_blk, jnp.concatenate([b_row] * GROUP, axis=1)))
            m_new = m
            for s_blk, bias in blocks:
                m_new = jnp.maximum(m_new, jnp.max(s_blk, axis=0, keepdims=True) + bias)
            alpha = jnp.exp2(m - m_new)
            p = jnp.concatenate([jnp.exp2(s_blk - (m_new - bias)) for s_blk, bias in blocks], axis=0)
            acc = alpha * acc + _nn(vst_ref[0, 0, :, pl.ds(k0, SEL_CHUNK)], p.astype(BF16))
            m = m_new
        return m, acc

    init = (jnp.full((1, cols), -3e38, F32), jnp.zeros((VT_ROWS, cols), F32))
    carry = group(n_below, SEL_GROUP, init, s_diag, True)
    carry = lax.fori_loop(0, n_below // 2, lambda gj, c: group(gj, 2 * SEL_GROUP, c, None, False), carry)
    _, acc_s = lax.cond(n_below % 2 == 1, lambda c: group(n_below - 1, SEL_GROUP, c, None, False), lambda c: c, carry)
    o_s = acc_s[:HEAD_DIM] / acc_s[HEAD_DIM:HEAD_DIM + 1]

    gt = g_ref[...]
    outs = []
    for h in range(GROUP):
        c = slice(h * Q_TILE, (h + 1) * Q_TILE)
        outs.append(gt[3 * h:3 * h + 1] * o_c[:, c] + gt[3 * h + 1:3 * h + 2] * o_s[:, c]
                    + gt[3 * h + 2:3 * h + 3] * o_w[:, c])
    for t in range(GROUP // 2):
        o_ref[:, LANES * t:LANES * (t + 1)] = jnp.concatenate(outs[2 * t:2 * t + 2], axis=0).T


def _pattn(q_t, gates_t, kcmp, vcmp_t, kskw, vt, ovt, nb, seq):
    nq = seq // Q_TILE
    nch = seq // CMP_STRIDE
    ospec = pl.BlockSpec((Q_TILE, GROUP * HEAD_DIM), lambda b, k, i: (b * nq + i, k))
    qspec = pl.BlockSpec((GROUP * HEAD_DIM, Q_TILE), lambda b, k, i: (k, b * nq + i))
    gspec = pl.BlockSpec((LANES, Q_TILE), lambda b, k, i: (k, b * nq + i))
    kspec = lambda which: pl.BlockSpec((seq, KV_DIM), lambda b, k, i: (b, which))
    vspec = lambda which: pl.BlockSpec((1, 1, VT_ROWS, seq), lambda b, k, i: (which, b, k, 0))
    return pl.pallas_call(
        functools.partial(_pattn_body, seq=seq),
        grid=(nb, N_KV, nq),
        in_specs=[qspec, gspec,
                  pl.BlockSpec((1, nch, KV_DIM), lambda b, k, i: (b, 0, 0)),
                  pl.BlockSpec((1, HEAD_DIM, nch), lambda b, k, i: (b, k, 0)),
                  kspec(0), kspec(1), vspec(0), vspec(1),
                  pl.BlockSpec((NS_PAD, nch), lambda b, k, i: (0, 0))],
        out_specs=ospec,
        out_shape=jax.ShapeDtypeStruct((nb * seq, ATT_DIM), F32),
        scratch_shapes=[pltpu.VMEM((NS_PAD, Q_TILE), F32)],
        compiler_params=_params(("arbitrary", "arbitrary", "arbitrary")),
    )(q_t, gates_t, kcmp, vcmp_t, kskw, kskw, vt, vt, ovt)


def _sattn_body(pages_ref, qp_ref, g_ref, w2k_ref, w2v_ref, ks_hbm, vs_hbm, abk_hbm, abv_hbm, kw_ref, vw_ref,
                ksn_ref, vsn_ref, kwn_ref, vwn_ref, ovt_ref, gsum_ref, et_ref, en_ref, o_ref,
                kbuf, vbuf, abk_buf, abv_buf, sems, *, past, dec, ntop, slc_row0):
    b = pl.program_id(0)
    n_pages = kbuf.shape[1]
    nchp = n_pages * abk_buf.shape[2]
    lw = kw_ref.shape[3]

    def page_copies(seq_idx, slot):
        streams = ((ks_hbm, kbuf, slc_row0), (vs_hbm, vbuf, slc_row0), (abk_hbm, abk_buf, 0), (abv_hbm, abv_buf, 0))
        return [pltpu.make_async_copy(src.at[row0 + pages_ref[seq_idx, pg]], buf.at[slot, pg], sems.at[a, slot])
                for a, (src, buf, row0) in enumerate(streams) for pg in range(n_pages)]

    slot = b % 2

    @pl.when(b == 0)
    def _():
        for cp in page_copies(0, 0):
            cp.start()

    last = pl.num_programs(0) - 1
    nxt = jnp.minimum(b + 1, last)
    for cp in page_copies(nxt, 1 - slot):
        cp.start()

    qp = qp_ref[0]
    nrow = qp.shape[0]
    tok_r = lax.broadcasted_iota(jnp.int32, (nrow, 1), 0) % dec
    tok_l = lax.broadcasted_iota(jnp.int32, (1, LANES), 1) % dec
    new_j = lax.broadcasted_iota(jnp.int32, (1, LANES), 1)
    new_mask = (new_j < dec) & (new_j <= tok_r)

    def pad_rows(x):
        return jnp.concatenate([x.astype(F32), jnp.zeros((LANES - dec, KV_DIM), F32)], axis=0).astype(BF16)

    def softmax2(s_p, s_n):
        m = jnp.maximum(jnp.max(s_p, axis=-1, keepdims=True), jnp.max(s_n, axis=-1, keepdims=True))
        p_p, p_n = jnp.exp(s_p - m), jnp.exp(s_n - m)
        l = jnp.sum(p_p, axis=-1, keepdims=True) + jnp.sum(p_n, axis=-1, keepdims=True)
        return p_p.astype(BF16), p_n.astype(BF16), l

    def summaries(ab_buf, w2_ref):
        ab = jnp.concatenate([ab_buf[slot, pg] for pg in range(n_pages)], axis=0)
        hid = ab[:, :HID2] + pltpu.roll(ab[:, HID2:], nchp - 1, 0)
        out = _nn((hid * jax.nn.sigmoid(hid)).astype(BF16), w2_ref[...])
        return jnp.where(lax.broadcasted_iota(jnp.int32, (nchp, 1), 0) < nchp - 1, out, 0.0)

    sw_p = _nn(qp, kw_ref[0, 0].astype(BF16))
    sw_n = _nt(qp, pad_rows(kwn_ref[0]))
    for cp in page_copies(b, slot):
        cp.wait()
    kc, vc = summaries(abk_buf, w2k_ref), summaries(abv_buf, w2v_ref)
    s = _nn(qp, kc.T.astype(BF16))
    kt = jnp.concatenate([kbuf[slot, pg] for pg in range(n_pages)], axis=1).astype(BF16)
    ss_p = _nn(qp, kt)
    ss_n = _nt(qp, pad_rows(ksn_ref[0]))

    cmask = lax.broadcasted_iota(jnp.int32, (1, nchp), 1) < nchp - 1
    s = jnp.where(cmask, s, NEG)
    p = jnp.where(cmask, jnp.exp(s - jnp.max(s, axis=-1, keepdims=True)), 0.0)
    pc = p / jnp.sum(p, axis=-1, keepdims=True)
    o_c = _nn(pc.astype(BF16), vc.astype(BF16))

    gsum = gsum_ref[...]
    pcs = sum(_nn(gsum, part) for part in _split3(pc))
    pcs = jnp.concatenate([pcs, jnp.zeros((LANES - nrow, nchp), F32)], axis=0)
    ovt = ovt_ref[...]
    imp_t = sum(_nt(ovt, part) for part in _split3(pcs))
    bias = _select_bias(imp_t, past + tok_l, ntop).T[:nrow].astype(BF16)

    kpos = (past - lw) + lax.broadcasted_iota(jnp.int32, (1, lw), 1)
    p_p, p_n, l = softmax2(jnp.where(kpos > past + tok_r - WINDOW, sw_p, NEG), jnp.where(new_mask, sw_n, NEG))
    o_w = (_nt(p_p, vw_ref[0, 0].astype(BF16)) + _nn(p_n, pad_rows(vwn_ref[0]))) / l

    vt = jnp.concatenate([vbuf[slot, pg] for pg in range(n_pages)], axis=1).astype(BF16)
    s_p = ss_p + _nn(bias, et_ref[...])
    s_n = jnp.where(new_mask, ss_n + _nt(bias, en_ref[...]), NEG)
    p_p, p_n, l = softmax2(s_p, s_n)
    o_s = (_nt(p_p, vt) + _nn(p_n, pad_rows(vsn_ref[0]))) / l

    g = g_ref[0]
    o_ref[0] = g[:, 0:1] * o_c + g[:, 1:2] * o_s + g[:, 2:3] * o_w

    @pl.when(b == last)
    def _():
        for cp in page_copies(nxt, 1 - slot):
            cp.wait()


def _sattn(pages, qp, g, w2k, w2v, ks, vs, abk, abv, win_k, win_v, layer, ksn, vsn, kwn, vwn, ovt, gsum, e_t, e_new,
           past, dec):
    nseq, nrow = qp.shape[:2]
    n_pages = pages.shape[1]
    page = ks.shape[2]
    n_pool, cpp = abk.shape[:2]
    nchp = n_pages * cpp
    lw = win_k.shape[3]
    ntop = min(SEL_TOPK, -(-(past + dec) // SEL_BLOCK))
    seq3 = lambda n, w: pl.BlockSpec((1, n, w), lambda b, pt: (b, 0, 0))
    fixed = lambda r, c: pl.BlockSpec((r, c), lambda b, pt: (0, 0))
    hbm = pl.BlockSpec(memory_space=pl.ANY)
    win = pl.BlockSpec((1, 1, KV_DIM, lw), lambda b, pt: (layer, b, 0, 0))
    return pl.pallas_call(
        functools.partial(_sattn_body, past=past, dec=dec, ntop=ntop, slc_row0=layer * n_pool),
        grid_spec=pltpu.PrefetchScalarGridSpec(
            num_scalar_prefetch=1,
            grid=(nseq,),
            in_specs=[seq3(nrow, KV_DIM), seq3(nrow, SUBLANES), fixed(HID2, KV_DIM), fixed(HID2, KV_DIM),
                      hbm, hbm, hbm, hbm, win, win,
                      seq3(dec, KV_DIM), seq3(dec, KV_DIM), seq3(dec, KV_DIM), seq3(dec, KV_DIM),
                      fixed(NS_PAD, nchp), fixed(nrow, nrow), fixed(NS_PAD, past), fixed(LANES, NS_PAD)],
            out_specs=seq3(nrow, KV_DIM),
            scratch_shapes=[pltpu.VMEM((2, n_pages, KV_DIM, page), F32), pltpu.VMEM((2, n_pages, KV_DIM, page), F32),
                            pltpu.VMEM((2, n_pages, cpp, 2 * HID2), F32), pltpu.VMEM((2, n_pages, cpp, 2 * HID2), F32),
                            pltpu.SemaphoreType.DMA((4, 2))]),
        out_shape=jax.ShapeDtypeStruct((nseq, nrow, KV_DIM), F32),
        compiler_params=_params(("arbitrary",)),
    )(pages, qp, g, w2k, w2v, ks, vs, abk, abv, win_k, win_v, ksn, vsn, kwn, vwn, ovt, gsum, e_t, e_new)


def _win_shift_body(win_ref, new_ref, o_ref, *, dec):
    rows = win_ref.shape[1] * win_ref.shape[2]
    lw = win_ref.shape[3]
    shifted = pltpu.roll(win_ref[0].reshape(rows, lw), lw - dec, 1)
    o_ref[0, :, :, :lw - LANES] = shifted[:, :lw - LANES].reshape(o_ref.shape[1], o_ref.shape[2], lw - LANES)
    lane = lax.broadcasted_iota(jnp.int32, (1, LANES), 1)
    tail = jnp.where(lane >= LANES - dec, new_ref[0].reshape(rows, LANES), shifted[:, lw - LANES:])
    o_ref[0, :, :, lw - LANES:] = tail.reshape(o_ref.shape[1], o_ref.shape[2], LANES)


def _win_shift(win, new_tail, dec, nseq_blk):
    depth, n, _, lw = win.shape
    blk = lambda w: pl.BlockSpec((1, nseq_blk, KV_DIM, w), lambda l, i: (l, i, 0, 0))
    return pl.pallas_call(
        functools.partial(_win_shift_body, dec=dec),
        grid=(depth, n // nseq_blk),
        in_specs=[blk(lw), blk(LANES)],
        out_specs=blk(lw),
        out_shape=jax.ShapeDtypeStruct(win.shape, F32),
        compiler_params=_params(("arbitrary", "arbitrary")),
    )(win, new_tail)


def _mix_body(x_ref, o_ref, u_ref, uprev_ref, cb_ref, sga_ref, sgb_ref, s1_ref, s2_ref,
              cw_ref, wa_ref, wc_ref, wo_ref, y_ref, ubuf, *, tm, slen):
    i = pl.program_id(0)
    u = u_ref[...]
    ubuf[0:SUBLANES, :] = uprev_ref[...]
    ubuf[SUBLANES:, :] = u
    pos = (i * tm + lax.broadcasted_iota(jnp.int32, (tm, 1), 0)) % slen
    prev1 = jnp.where(pos >= 1, ubuf[SUBLANES - 1:SUBLANES - 1 + tm, :], s1_ref[...])
    prev2 = jnp.where(pos >= 2, ubuf[SUBLANES - 2:SUBLANES - 2 + tm, :], s2_ref[...])
    cw = cw_ref[...]
    y_conv = cw[0:1] * prev2 + cw[1:2] * prev1 + cw[2:3] * u
    a_br = _nn(o_ref[...].astype(BF16), wa_ref[...])
    c_br = _nn((cb_ref[...] * y_conv).astype(BF16), wc_ref[...])
    mix = sga_ref[...] * a_br + sgb_ref[...] * c_br
    y_ref[...] = x_ref[...] + _nn(mix.astype(BF16), wo_ref[...])


def _mix(x, o_att, u, cb, sga, sgb, side1, side2, conv_w, wa, wc, wo, layer, tm, slen):
    t, d = x.shape
    row = lambda width: pl.BlockSpec((tm, width), lambda i: (i, 0))
    fixed = lambda shape: pl.BlockSpec(shape, lambda i: (0, 0))
    n_side = side1.shape[0] // tm
    side = pl.BlockSpec((tm, CONV_DIM), lambda i: (i % n_side, 0))
    prev = pl.BlockSpec((SUBLANES, CONV_DIM), lambda i: (jnp.maximum(i * (tm // SUBLANES) - 1, 0), 0))
    return pl.pallas_call(
        functools.partial(_mix_body, tm=tm, slen=slen),
        grid=(t // tm,),
        in_specs=[row(d), row(ATT_DIM), row(CONV_DIM), prev, row(CONV_DIM), row(d), row(d), side, side,
                  fixed((SUBLANES, CONV_DIM)), _layer_block(layer, (ATT_DIM, d)), _layer_block(layer, (CONV_DIM, d)),
                  _layer_block(layer, (d, d))],
        out_specs=row(d),
        out_shape=jax.ShapeDtypeStruct((t, d), F32),
        scratch_shapes=[pltpu.VMEM((tm + SUBLANES, CONV_DIM), F32)],
        compiler_params=_params(("arbitrary",)),
    )(x, o_att, u, u, cb, sga, sgb, side1, side2, conv_w, wa, wc, wo)


def _ffn_body(x_ref, g_ref, win_ref, wout_ref, gf_ref, y_ref, *, d_ff, fc, final):
    x = x_ref[...]
    h = x * lax.rsqrt(jnp.mean(x * x, axis=-1, keepdims=True) + EPS)
    hb = (h * g_ref[...]).astype(BF16)
    acc = x
    for c in range(d_ff // fc):
        gate = _nn(hb, win_ref[:, c * fc:(c + 1) * fc])
        up = _nn(hb, win_ref[:, d_ff + c * fc:d_ff + (c + 1) * fc])
        act = (gate * jax.nn.sigmoid(gate) * up).astype(BF16)
        acc = acc + _nn(act, wout_ref[c * fc:(c + 1) * fc, :])
    if final:
        acc = acc * lax.rsqrt(jnp.mean(acc * acc, axis=-1, keepdims=True) + EPS) * gf_ref[...]
    y_ref[...] = acc


def _ffn(x, g, w_in, w_out, g_final, layer, tm, final):
    t, d = x.shape
    d_ff = w_out.shape[1]
    fc = 256 if d_ff % 256 == 0 else LANES
    row = pl.BlockSpec((tm, d), lambda i: (i, 0))
    fixed = lambda shape: pl.BlockSpec(shape, lambda i: (0, 0))
    return pl.pallas_call(
        functools.partial(_ffn_body, d_ff=d_ff, fc=fc, final=final),
        grid=(t // tm,),
        in_specs=[row, fixed((1, d)), _layer_block(layer, (d, 2 * d_ff)), _layer_block(layer, (d_ff, d)), fixed((1, d))],
        out_specs=row,
        out_shape=jax.ShapeDtypeStruct((t, d), F32),
        compiler_params=_params(("arbitrary",)),
    )(x, g, w_in, w_out, g_final)


def _rope_tables(pos):
    inv = 1.0 / (ROPE_THETA ** (jnp.arange(ROPE_HALF, dtype=F32) * (2.0 / ROPE_DIM)))
    ang = pos.astype(F32)[:, None] * inv[None, :]
    cos, sin = jnp.cos(ang), jnp.sin(ang)
    t = pos.shape[0]
    pad = jnp.zeros((t, HEAD_DIM - ROPE_DIM), F32)
    rc = jnp.concatenate([cos, cos, pad + 1.0], axis=1)
    rs1 = jnp.concatenate([-sin, jnp.zeros_like(sin), pad], axis=1)
    rs2 = jnp.concatenate([jnp.zeros_like(sin), sin, pad], axis=1)
    rep = LANES // HEAD_DIM
    return tuple(jnp.tile(a, (1, rep)) for a in (rc, rs1, rs2))


def _pack_w_in(w):
    wt = w.transpose(0, 2, 1)
    depth, _, d = wt.shape
    o_gl = ATT_DIM + 6 * KV_DIM
    gl = wt[:, o_gl:o_gl + 3 * N_HEADS].reshape(depth, N_KV, 3 * GROUP, d)
    gl = jnp.pad(gl, ((0, 0), (0, 0), (0, LANES - 3 * GROUP), (0, 0))).reshape(depth, N_KV * LANES, d)
    return jnp.concatenate([wt[:, :o_gl], gl, wt[:, o_gl + 3 * N_HEADS:]], axis=1).astype(BF16)


def _pack_cmp(pe, w1, w2):
    eye = jnp.eye(N_KV, dtype=F32)
    pe2 = jnp.broadcast_to(pe.reshape(2, CMP_STRIDE, 1, HEAD_DIM), (2, CMP_STRIDE, N_KV, HEAD_DIM)).reshape(2, CHUNK)
    w1r = w1.reshape(2, CMP_STRIDE, HEAD_DIM, CMP_HIDDEN)
    w1p = jnp.einsum('pjdh,kc->pjkdch', w1r, eye).reshape(2, CHUNK, HID2).astype(BF16)
    w2p = jnp.einsum('hd,kc->khcd', w2, eye).reshape(HID2, KV_DIM).astype(BF16)
    return pe2, w1p, w2p


def _overlap_t(nch):
    ci = jnp.arange(nch)[None, :] * CMP_STRIDE
    sj = jnp.arange(NS_PAD)[:, None] * SEL_BLOCK
    return ((ci <= sj + SEL_BLOCK - 1) & (ci + CMP_BLOCK - 1 >= sj)).astype(BF16)


def _dim_major(a):
    lead = a.shape[:-3]
    n = len(lead)
    return a.transpose(*range(n), n + 1, n + 2, n).reshape(*lead, KV_DIM, a.shape[-3])


def _pos_major(a):
    lead = a.shape[:-2]
    n = len(lead)
    return a.reshape(*lead, N_KV, HEAD_DIM, a.shape[-1]).transpose(*range(n), n + 2, n, n + 1)


def _prompt_attention(q, gates, kcvc, kskw, vt, cmpk, cmpv, nb, seq):
    nch = seq // CMP_STRIDE
    abk = _cmp_ab_rows(kcvc, nb * seq, 0, cmpk[0], cmpk[1], seq)
    abv = _cmp_ab_rows(kcvc, nb * seq, 1, cmpv[0], cmpv[1], seq)
    kcmp, vcmp = _cmp_fin(abk, abv, cmpk[2], cmpv[2], nch, nch)
    kcmp = kcmp.reshape(nb, nch, KV_DIM).astype(BF16)
    vcmp_t = vcmp.reshape(nb, nch, KV_DIM).transpose(0, 2, 1).astype(BF16)
    return _pattn(q, gates, kcmp, vcmp_t, kskw, vt, _overlap_t(nch), nb, seq)


def _sample_cache(cmpk, cmpv, pool_k, pool_v, layer):
    n_pool, _, page = pool_k.shape[1:]
    cpp = page // CMP_STRIDE
    pp = math.gcd(n_pool, POOL_PAGES)
    return (_cmp_ab_pages(pool_k, layer, cmpk[0], cmpk[1], pp).reshape(n_pool, cpp, 2 * HID2),
            _cmp_ab_pages(pool_v, layer, cmpv[0], cmpv[1], pp).reshape(n_pool, cpp, 2 * HID2))


def _sample_attention(q, gates, new4, cache, w2k, w2v, slc_k, slc_v, page_table, win_k, win_v, layer, dec):
    abk, abv = cache
    ndb, n_pages = page_table.shape
    n_pool, _, page = slc_k.shape[1:]
    past = n_pages * page
    nchp = past // CMP_STRIDE
    pages = page_table.astype(jnp.int32)
    nrow = N_KV * GROUP * dec
    eye_kv = jnp.eye(N_KV, dtype=F32)
    q_s = q.reshape(ndb, dec, N_KV, GROUP, HEAD_DIM).transpose(0, 2, 3, 1, 4)
    qp = (q_s[:, :, :, :, None] * eye_kv[None, :, None, None, :, None]).reshape(ndb, nrow, KV_DIM).astype(BF16)
    g_s = gates.reshape(ndb, dec, N_KV, LANES)[..., :3 * GROUP].reshape(ndb, dec, N_KV, GROUP, 3)
    g_s = jnp.pad(g_s.transpose(0, 2, 3, 1, 4).reshape(ndb, nrow, 3), ((0, 0), (0, 0), (0, SUBLANES - 3)))
    e_t = jax.nn.one_hot(jnp.arange(past) // SEL_BLOCK, NS_PAD, dtype=BF16).T
    e_new = jax.nn.one_hot((past + jnp.arange(LANES)) // SEL_BLOCK, NS_PAD, dtype=BF16)
    rid = jnp.arange(nrow)
    gsum = ((rid[:, None] // (GROUP * dec) == rid[None, :] // (GROUP * dec))
            & (rid[:, None] % dec == rid[None, :] % dec)).astype(BF16)
    ksn, vsn, kwn, vwn = (a.reshape(ndb, dec, KV_DIM) for a in new4)
    o_t = _sattn(pages, qp, g_s, w2k, w2v, slc_k.reshape(-1, KV_DIM, page), slc_v.reshape(-1, KV_DIM, page),
                 abk, abv, win_k, win_v, layer,
                 ksn, vsn, kwn, vwn, _overlap_t(nchp), gsum, e_t, e_new, past, dec)
    o_t = o_t.reshape(ndb, N_KV, GROUP, dec, N_KV, HEAD_DIM)
    return jnp.einsum('bkgtcd,kc->btkgd', o_t, eye_kv).reshape(ndb * dec, ATT_DIM)


def kernel(x_prompt, x_sample, cache_cmp_k, cache_cmp_v, cache_slc_k, cache_slc_v, state_win_k, state_win_v,
           state_conv, page_table, w_in, cmp_pe, cmp_w1, cmp_w2, w_att_out, conv_w, w_conv_out, w_o,
           norm_mix, norm_ffn, w_ffn_in, w_ffn_out, norm_final):
    nb, seq, d = x_prompt.shape
    ndb, dec, _ = x_sample.shape
    depth, n_pool, page = cache_cmp_k.shape[:3]
    n_pages = page_table.shape[1]
    past = n_pages * page
    lw = state_win_k.shape[2]
    tp, ts = nb * seq, ndb * dec
    tm_p, tm_s = math.gcd(ROW_TILE, seq), math.gcd(ROW_TILE, ts)
    nchp = past // CMP_STRIDE
    keep = min(WINDOW, seq)
    assert seq % SEL_GROUP == 0 and seq >= WIN_SPAN and seq // SEL_BLOCK <= NS_PAD
    assert dec <= SUBLANES and N_KV * GROUP * dec <= LANES and (N_KV * GROUP * dec) % (2 * SUBLANES) == 0
    assert tm_s % SUBLANES == 0 and tm_s % dec == 0
    assert (past + dec - CMP_BLOCK) // CMP_STRIDE + 1 == nchp - 1
    assert -(-(past + dec) // SEL_BLOCK) <= NS_PAD and past >= WINDOW and lw == WINDOW

    rope_p = _rope_tables(jnp.arange(seq, dtype=jnp.int32))
    rope_s = _rope_tables(jnp.tile(past + jnp.arange(dec, dtype=jnp.int32), ndb))
    pool_k, pool_v, slc_k, slc_v, win_k, win_v = (
        _dim_major(a) for a in (cache_cmp_k, cache_cmp_v, cache_slc_k, cache_slc_v, state_win_k, state_win_v))
    no_side = jnp.zeros((tm_p, CONV_DIM), F32)
    zero = jnp.zeros((ndb, dec - 1, CONV_DIM), F32)

    cmp_w = [[_pack_cmp(cmp_pe[l, a], cmp_w1[l, a], cmp_w2[l, a]) for a in range(2)] for l in range(depth)]
    caches = [_sample_cache(*cmp_w[l], pool_k, pool_v, l) for l in range(depth)]

    w_inp = _pack_w_in(w_in)
    w_mix = (w_att_out.astype(BF16), w_conv_out.astype(BF16), w_o.astype(BF16))
    w_ffn = (w_ffn_in.astype(BF16), w_ffn_out.astype(BF16), norm_final[None, :])

    xp, xs = x_prompt.reshape(tp, d), x_sample.reshape(ts, d)
    st_p = [[] for _ in range(7)]
    st_s = [[] for _ in range(7)]
    for l in range(depth):
        g_l = norm_mix[l][None, :]
        g_f = norm_ffn[l][None, :]
        cmpk, cmpv = cmp_w[l]
        cw = jnp.pad(conv_w[l], ((0, SUBLANES - CONV_WIDTH), (0, 0)))
        final = l == depth - 1

        _, kcvc, kskw, vt, stp, _, u, cb, sga, sgb, q_t, gates_t = _inproj(xp, g_l, w_inp, l, rope_p, nb, tm_p)
        o_att = _prompt_attention(q_t, gates_t, kcvc, kskw, vt, cmpk, cmpv, nb, seq)
        xp = _mix(xp, o_att, u, cb, sga, sgb, no_side, no_side, cw, *w_mix, l, tm_p, seq)
        xp = _ffn(xp, g_f, *w_ffn, l, tm_p, final)
        for j in range(4):
            st_p[j].append(stp[j])
        st_p[4].append(stp[4][:, :, seq - keep:])
        st_p[5].append(stp[5][:, :, seq - keep:])
        st_p[6].append(jnp.stack([u[(b + 1) * seq - (CONV_WIDTH - 1):(b + 1) * seq] for b in range(nb)]))

        q, kcvc, kskw, vt, sts, gates, u, cb, sga, sgb, _, _ = _inproj(xs, g_l, w_inp, l, rope_s, 1, tm_s)
        new_t = sts[:, 0].reshape(6, KV_DIM, ndb, dec).transpose(0, 2, 1, 3)
        new_rows = lambda j: sts[j, 0].T
        o_att = _sample_attention(q, gates, [kskw[:, :KV_DIM], new_rows(3), kskw[:, KV_DIM:], new_rows(5)],
                                  caches[l], cmpk[2], cmpv[2], slc_k, slc_v, page_table, win_k, win_v, l, dec)
        cbuf = state_conv[l]
        side1 = jnp.concatenate([cbuf[:, 1:2], zero], axis=1).reshape(ts, CONV_DIM)
        side2 = jnp.concatenate([cbuf[:, 0:2], zero[:, 1:]], axis=1).reshape(ts, CONV_DIM)
        xs = _mix(xs, o_att, u, cb, sga, sgb, side1, side2, cw, *w_mix, l, tm_s, dec)
        xs = _ffn(xs, g_f, *w_ffn, l, tm_s, final)
        for j in range(6):
            st_s[j].append(new_t[j])
        st_s[6].append(jnp.concatenate([cbuf, u.reshape(ndb, dec, CONV_DIM)], axis=1)[:, -(CONV_WIDTH - 1):])

    kv_p = [_pos_major(jnp.stack(a)) for a in st_p[:6]]
    kv_s = [jnp.stack(a) for a in st_s[:6]]
    lane_pad = ((0, 0), (0, 0), (0, 0), (LANES - dec, 0))
    kv_s[4] = _win_shift(win_k, jnp.pad(kv_s[4], lane_pad), dec, math.gcd(ndb, SUBLANES))
    kv_s[5] = _win_shift(win_v, jnp.pad(kv_s[5], lane_pad), dec, math.gcd(ndb, SUBLANES))
    kv_s = [_pos_major(a) for a in kv_s]
    return (xp.reshape(nb, seq, d), xs.reshape(ndb, dec, d), *kv_p, jnp.stack(st_p[6]),
            *kv_s, jnp.stack(st_s[6]))
```
